```python
import math
import jax, jax.numpy as jnp
from jax import lax
import numpy as np

D_MODEL = 4096
BATCH = 1
SEQ = 8192
DEPTH = 4

HEAD_DIM = 128
NSA_HEADS = 16
NSA_KV_HEADS = 4
NSA_GROUP = NSA_HEADS // NSA_KV_HEADS
NSA_WIDTH = NSA_HEADS * HEAD_DIM
NSA_KV_WIDTH = NSA_KV_HEADS * HEAD_DIM
CMP_BLOCK = 32
CMP_STRIDE = 16
CMP_HIDDEN = 256
SEL_BLOCK = 64
SEL_TOPK = 16
WINDOW = 512
Q_BLOCK = 128
GDN_HEADS = 16
GDN_DK = 128
GDN_DV = 128
GDN_KEY_WIDTH = GDN_HEADS * GDN_DK
GDN_VAL_WIDTH = GDN_HEADS * GDN_DV
GDN_QKV_WIDTH = 2 * GDN_KEY_WIDTH + GDN_VAL_WIDTH
GDN_CONV = 4
GDN_CHUNK = 64
SC_WIDTH = 2048
SC_CONV = 3
N_BRANCH = 3
NORM_EPS = 1e-6
NEG_INF = -1e30
FORCE_SCORE = 1e6

IN_SPLITS = (
    NSA_WIDTH,
    2 * NSA_KV_WIDTH,
    2 * NSA_KV_WIDTH,
    2 * NSA_KV_WIDTH,
    3 * NSA_HEADS,
    NSA_WIDTH,
    GDN_KEY_WIDTH,
    GDN_KEY_WIDTH,
    GDN_VAL_WIDTH,
    GDN_HEADS,
    GDN_HEADS,
    GDN_VAL_WIDTH,
    SC_WIDTH,
    SC_WIDTH,
    SC_WIDTH,
    SC_WIDTH,
    N_BRANCH * D_MODEL,
)
IN_WIDTH = sum(IN_SPLITS)

kernel_name = 'hybrid_nsa_gdn_shortconv_gated_merge'


def _split_points(sizes):
    pts, acc = [], 0
    for s in sizes[:-1]:
        acc += s
        pts.append(acc)
    return pts


def _rms_norm(x, w):
    xf = x.astype(jnp.float32)
    y = xf * lax.rsqrt(jnp.mean(xf * xf, axis=-1, keepdims=True) + NORM_EPS)
    return (y * w.astype(jnp.float32)).astype(x.dtype)


def _l2_normalize(x):
    return x * lax.rsqrt(jnp.sum(x * x, axis=-1, keepdims=True) + NORM_EPS)


def _causal_conv(x, w):
    k = w.shape[0]
    t = x.shape[1]
    xp = jnp.pad(x, ((0, 0), (k - 1, 0), (0, 0)))
    return sum(xp[:, j:j + t] * w[j] for j in range(k))


def _masked_softmax(s, mask):
    s = jnp.where(mask, s, NEG_INF)
    m = jnp.max(s, axis=-1, keepdims=True)
    e = jnp.exp(s - m) * mask.astype(jnp.float32)
    return e / jnp.maximum(jnp.sum(e, axis=-1, keepdims=True), 1e-30)


def _alibi_slopes():
    h = jnp.arange(1, NSA_HEADS + 1, dtype=jnp.float32)
    return jnp.exp2(-8.0 * h / NSA_HEADS).reshape(NSA_KV_HEADS, NSA_GROUP)


def _compress(x, pos_emb, w1, w2):
    b, t, hk, dh = x.shape
    ratio = CMP_BLOCK // CMP_STRIDE
    n_chunk = t // CMP_STRIDE
    n_cmp = n_chunk - ratio + 1
    xc = x.reshape(b, n_chunk, CMP_STRIDE, hk, dh)
    blocks = jnp.concatenate([xc[:, r:r + n_cmp] for r in range(ratio)], axis=2)
    blocks = blocks + pos_emb[:, None, :]
    hidden = jax.nn.silu(jnp.einsum('bnlgd,ldh->bngh', blocks, w1))
    return jnp.einsum('bngh,he->bnge', hidden, w2)


def _nsa_mixer(q, kv_cmp, kv_slc, kv_win, gate_logits, qk_norm, cmp_pos, cmp_w1, cmp_w2):
    b, t, _ = q.shape
    hk, g, dh = NSA_KV_HEADS, NSA_GROUP, HEAD_DIM
    f32 = jnp.float32
    scale = dh ** -0.5
    slopes = _alibi_slopes()
    pos = jnp.arange(t)
    q = _rms_norm(q.reshape(b, t, hk, g, dh), qk_norm[0])

    kvc = kv_cmp.reshape(b, t, 2, hk, dh)
    k_cmp = _rms_norm(_compress(kvc[:, :, 0], cmp_pos[0], cmp_w1[0], cmp_w2[0]), qk_norm[1])
    v_cmp = _compress(kvc[:, :, 1], cmp_pos[1], cmp_w1[1], cmp_w2[1])
    n_cmp = k_cmp.shape[1]
    cmp_start = jnp.arange(n_cmp) * CMP_STRIDE
    dist_c = pos[:, None] - (cmp_start + CMP_BLOCK - 1)[None, :]
    s_c = (jnp.einsum('btgzd,bngd->bgztn', q, k_cmp).astype(f32) * scale
           - slopes[:, :, None, None] * dist_c.astype(f32))
    p_cmp = _masked_softmax(s_c, dist_c >= 0)
    o_cmp = jnp.einsum('bgztn,bngd->btgzd', p_cmp.astype(v_cmp.dtype), v_cmp)

    n_blk = t // SEL_BLOCK
    blk_start = jnp.arange(n_blk) * SEL_BLOCK
    cover = ((cmp_start[:, None] <= blk_start[None, :] + SEL_BLOCK - 1)
             & (cmp_start[:, None] + CMP_BLOCK - 1 >= blk_start[None, :]))
    imp = jnp.einsum('bgztn,nj->bgtj', p_cmp, cover.astype(f32))
    cur = pos // SEL_BLOCK
    j = jnp.arange(n_blk)
    valid = j[None, :] <= cur[:, None]
    forced = (j[None, :] == 0) | (j[None, :] == cur[:, None]) | (j[None, :] == cur[:, None] - 1)
    imp = jnp.where(valid, jnp.where(forced, FORCE_SCORE, imp), -1.0)
    n_sel = min(SEL_TOPK, n_blk)
    top_val, top_idx = lax.top_k(imp, n_sel)
    sel_ok = top_val > -0.5

    kvs = kv_slc.reshape(b, t, 2, hk, dh)
    k_s = _rms_norm(kvs[:, :, 0], qk_norm[2])
    k_blk = k_s.reshape(b, n_blk, SEL_BLOCK, hk, dh).transpose(0, 3, 1, 2, 4)
    v_blk = kvs[:, :, 1].reshape(b, n_blk, SEL_BLOCK, hk, dh).transpose(0, 3, 1, 2, 4)
    q_b = jnp.moveaxis(q.reshape(b, n_blk, SEL_BLOCK, hk, g, dh), 1, 0)
    idx_b = jnp.moveaxis(top_idx.reshape(b, hk, n_blk, SEL_BLOCK, n_sel), 2, 0)
    ok_b = jnp.moveaxis(sel_ok.reshape(b, hk, n_blk, SEL_BLOCK, n_sel), 2, 0)
    bi = jnp.arange(b)[:, None, None, None]
    gi = jnp.arange(hk)[None, :, None, None]
    n_key = n_sel * SEL_BLOCK

    def sel_block(args):
        qb, ib, okb, blk = args
        kg = k_blk[bi, gi, ib].reshape(b, hk, SEL_BLOCK, n_key, dh)
        vg = v_blk[bi, gi, ib].reshape(b, hk, SEL_BLOCK, n_key, dh)
        kpos = (ib[..., None] * SEL_BLOCK + jnp.arange(SEL_BLOCK)).reshape(b, hk, SEL_BLOCK, n_key)
        qpos = blk * SEL_BLOCK + jnp.arange(SEL_BLOCK)
        dist = qpos[None, None, :, None] - kpos
        mask = jnp.repeat(okb, SEL_BLOCK, axis=-1) & (dist >= 0)
        s = (jnp.einsum('bqgzd,bgqkd->bgzqk', qb, kg).astype(f32) * scale
             - slopes[None, :, :, None, None] * dist[:, :, None].astype(f32))
        p = _masked_softmax(s, mask[:, :, None])
        return jnp.einsum('bgzqk,bgqkd->bqgzd', p.astype(vg.dtype), vg)

    o_slc = lax.map(sel_block, (q_b, idx_b, ok_b, jnp.arange(n_blk)))
    o_slc = jnp.moveaxis(o_slc, 0, 1).reshape(b, t, hk, g, dh)

    kvw = kv_win.reshape(b, t, 2, hk, dh)
    k_w = _rms_norm(kvw[:, :, 0], qk_norm[3])
    v_w = kvw[:, :, 1]
    n_q = t // Q_BLOCK
    n_prev = WINDOW // Q_BLOCK
    band_len = (n_prev + 1) * Q_BLOCK
    pad = ((0, 0), (WINDOW, 0), (0, 0), (0, 0))
    k_pad = jnp.pad(k_w, pad).reshape(b, n_q + n_prev, Q_BLOCK, hk, dh)
    v_pad = jnp.pad(v_w, pad).reshape(b, n_q + n_prev, Q_BLOCK, hk, dh)
    band = jnp.arange(n_q)[:, None] + jnp.arange(n_prev + 1)[None, :]
    k_band = k_pad[:, band].reshape(b, n_q, band_len, hk, dh)
    v_band = v_pad[:, band].reshape(b, n_q, band_len, hk, dh)
    q_w = q.reshape(b, n_q, Q_BLOCK, hk, g, dh)
    rel = jnp.arange(Q_BLOCK)[:, None] - jnp.arange(band_len)[None, :] + WINDOW
    kpos = (jnp.arange(n_q)[:, None] - n_prev) * Q_BLOCK + jnp.arange(band_len)[None, :]
    mask_w = ((rel >= 0) & (rel < WINDOW))[None] & (kpos >= 0)[:, None, :]
    s_w = (jnp.einsum('bnqgzd,bnkgd->bgznqk', q_w, k_band).astype(f32) * scale
           - slopes[:, :, None, None, None] * rel.astype(f32))
    p_w = _masked_softmax(s_w, mask_w)
    o_win = jnp.einsum('bgznqk,bnkgd->bnqgzd', p_w.astype(v_band.dtype), v_band).reshape(b, t, hk, g, dh)

    gl = jax.nn.sigmoid(gate_logits.reshape(b, t, hk, g, 3))
    o = gl[..., 0:1] * o_cmp + gl[..., 1:2] * o_slc + gl[..., 2:3] * o_win
    return o.reshape(b, t, NSA_WIDTH)


def _gated_deltanet(q, k, v, a, beta_logit, conv_w, a_log, dt_bias):
    b, t, _ = q.shape
    h, dk, dv, c = GDN_HEADS, GDN_DK, GDN_DV, GDN_CHUNK
    f32 = jnp.float32
    qkv = jax.nn.silu(_causal_conv(jnp.concatenate([q, k, v], axis=-1), conv_w))
    q, k, v = jnp.split(qkv.astype(f32), [GDN_KEY_WIDTH, 2 * GDN_KEY_WIDTH], axis=-1)
    q = _l2_normalize(q.reshape(b, t, h, dk)) * (dk ** -0.5)
    k = _l2_normalize(k.reshape(b, t, h, dk))
    v = v.reshape(b, t, h, dv)
    beta = jax.nn.sigmoid(beta_logit.astype(f32))
    gdecay = -jnp.exp(a_log.astype(f32)) * jax.nn.softplus(a.astype(f32) + dt_bias.astype(f32))

    n = t // c
    chunk = lambda x: x.reshape(b, n, c, h, -1).transpose(0, 3, 1, 2, 4)
    chunk_s = lambda x: x.reshape(b, n, c, h).transpose(0, 3, 1, 2)
    q, k, v = chunk(q), chunk(k), chunk(v)
    beta, gc = chunk_s(beta), jnp.cumsum(chunk_s(gdecay), axis=-1)

    tri = jnp.tril(jnp.ones((c, c), dtype=bool))
    strict = jnp.tril(jnp.ones((c, c), dtype=bool), k=-1)
    diff = gc[..., :, None] - gc[..., None, :]
    decay = jnp.where(tri, jnp.exp(jnp.where(tri, diff, 0.0)), 0.0)
    kb = k * beta[..., None]
    vb = v * beta[..., None]
    lmat = jnp.where(strict, jnp.einsum('bhncd,bhnsd->bhncs', kb, k) * decay, 0.0)
    amat = jnp.eye(c, dtype=f32) + lmat
    rhs = jnp.concatenate([vb, kb * jnp.exp(gc)[..., None]], axis=-1)
    sol = lax.linalg.triangular_solve(amat, rhs, left_side=True, lower=True)
    u, w = sol[..., :dv], sol[..., dv:]
    attn = jnp.einsum('bhncd,bhnsd->bhncs', q, k) * decay
    q_e = q * jnp.exp(gc)[..., None]
    g_last = gc[..., -1]
    k_e = k * jnp.exp(g_last[..., None] - gc)[..., None]

    def step(state, xs):
        qe, ke, uu, ww, at, gl = xs
        v_new = uu - jnp.einsum('bhcd,bhde->bhce', ww, state)
        o = jnp.einsum('bhcd,bhde->bhce', qe, state) + jnp.einsum('bhcs,bhse->bhce', at, v_new)
        state = state * jnp.exp(gl)[..., None, None] + jnp.einsum('bhcd,bhce->bhde', ke, v_new)
        return state, o

    xs = tuple(jnp.moveaxis(z, 2, 0) for z in (q_e, k_e, u, w, attn, g_last))
    s0 = jnp.zeros((b, h, dk, dv), f32)
    _, o = lax.scan(step, s0, xs)
    return o.transpose(1, 0, 3, 2, 4).reshape(b, t, h, dv)


def _hybrid_layer(x, norm_w, w_in, nsa_qk_norm, cmp_pos, cmp_w1, cmp_w2, gdn_conv_w, gdn_a_log,
                  gdn_dt_bias, gdn_norm_w, sc_conv_w, w_branch_nsa, w_branch_gdn, w_branch_sc, w_out):
    b, t, _ = x.shape
    hn = _rms_norm(x, norm_w)
    proj = jnp.einsum('btd,de->bte', hn, w_in)
    (q_a, kv_c, kv_s, kv_w, gate_a, z_a,
     q_b, k_b, v_b, a_b, beta_b, z_b,
     bg_c, cg_c, x_c, z_c, merge) = jnp.split(proj, _split_points(IN_SPLITS), axis=-1)

    o_a = _nsa_mixer(q_a, kv_c, kv_s, kv_w, gate_a, nsa_qk_norm, cmp_pos, cmp_w1, cmp_w2)
    o_a = o_a * jax.nn.silu(z_a)

    o_b = _gated_deltanet(q_b, k_b, v_b, a_b, beta_b, gdn_conv_w, gdn_a_log, gdn_dt_bias)
    o_b = _rms_norm(o_b, gdn_norm_w) * jax.nn.silu(z_b.reshape(b, t, GDN_HEADS, GDN_DV).astype(jnp.float32))
    o_b = o_b.reshape(b, t, GDN_VAL_WIDTH).astype(x.dtype)

    o_c = bg_c * _causal_conv(cg_c * x_c, sc_conv_w) * jax.nn.silu(z_c)

    gates = jax.nn.sigmoid(merge.reshape(b, t, N_BRANCH, D_MODEL))
    merged = (gates[:, :, 0] * jnp.einsum('btc,cd->btd', o_a, w_branch_nsa)
              + gates[:, :, 1] * jnp.einsum('btc,cd->btd', o_b, w_branch_gdn)
              + gates[:, :, 2] * jnp.einsum('btc,cd->btd', o_c, w_branch_sc))
    return x + jnp.einsum('btd,de->bte', merged, w_out)


def setup_inputs(seed: int = 0) -> dict:
    key = jax.random.key(seed)
    ks = jax.random.split(key, 16)
    f32 = jnp.float32

    def normal(k, shape, scale):
        return jax.random.normal(k, shape, f32) * scale

    x = normal(ks[0], (BATCH, SEQ, D_MODEL), 1.0)
    norm_w = 1.0 + normal(ks[1], (DEPTH, D_MODEL), 0.02)
    w_in = normal(ks[2], (DEPTH, D_MODEL, IN_WIDTH), D_MODEL ** -0.5)
    nsa_qk_norm = 1.0 + normal(ks[3], (DEPTH, 4, HEAD_DIM), 0.02)
    cmp_pos = normal(ks[4], (DEPTH, 2, CMP_BLOCK, HEAD_DIM), 0.02)
    cmp_w1 = normal(ks[5], (DEPTH, 2, CMP_BLOCK, HEAD_DIM, CMP_HIDDEN), (CMP_BLOCK * HEAD_DIM) ** -0.5)
    cmp_w2 = normal(ks[6], (DEPTH, 2, CMP_HIDDEN, HEAD_DIM), CMP_HIDDEN ** -0.5)
    gdn_conv_w = normal(ks[7], (DEPTH, GDN_CONV, GDN_QKV_WIDTH), GDN_CONV ** -0.5)
    gdn_a_log = jnp.log(jax.random.uniform(ks[8], (DEPTH, GDN_HEADS), f32, 1.0, 16.0))
    dt = jnp.exp(jax.random.uniform(ks[9], (DEPTH, GDN_HEADS), f32, math.log(1e-3), math.log(1e-1)))
    gdn_dt_bias = dt + jnp.log(-jnp.expm1(-dt))
    gdn_norm_w = 1.0 + normal(ks[10], (DEPTH, GDN_DV), 0.02)
    sc_conv_w = normal(ks[11], (DEPTH, SC_CONV, SC_WIDTH), SC_CONV ** -0.5)
    w_branch_nsa = normal(ks[12], (DEPTH, NSA_WIDTH, D_MODEL), NSA_WIDTH ** -0.5)
    w_branch_gdn = normal(ks[13], (DEPTH, GDN_VAL_WIDTH, D_MODEL), GDN_VAL_WIDTH ** -0.5)
    w_branch_sc = normal(ks[14], (DEPTH, SC_WIDTH, D_MODEL), SC_WIDTH ** -0.5)
    w_out = normal(ks[15], (DEPTH, D_MODEL, D_MODEL), D_MODEL ** -0.5)
    return {'x': x, 'norm_w': norm_w, 'w_in': w_in, 'nsa_qk_norm': nsa_qk_norm, 'cmp_pos': cmp_pos,
            'cmp_w1': cmp_w1, 'cmp_w2': cmp_w2, 'gdn_conv_w': gdn_conv_w, 'gdn_a_log': gdn_a_log,
            'gdn_dt_bias': gdn_dt_bias, 'gdn_norm_w': gdn_norm_w, 'sc_conv_w': sc_conv_w,
            'w_branch_nsa': w_branch_nsa, 'w_branch_gdn': w_branch_gdn, 'w_branch_sc': w_branch_sc,
            'w_out': w_out}


def reference(x, norm_w, w_in, nsa_qk_norm, cmp_pos, cmp_w1, cmp_w2, gdn_conv_w, gdn_a_log,
              gdn_dt_bias, gdn_norm_w, sc_conv_w, w_branch_nsa, w_branch_gdn, w_branch_sc, w_out):
    for l in range(DEPTH):
        x = _hybrid_layer(x, norm_w[l], w_in[l], nsa_qk_norm[l], cmp_pos[l], cmp_w1[l], cmp_w2[l],
                          gdn_conv_w[l], gdn_a_log[l], gdn_dt_bias[l], gdn_norm_w[l], sc_conv_w[l],
                          w_branch_nsa[l], w_branch_gdn[l], w_branch_sc[l], w_out[l])
    return x
```

```python
import functools
import math

import jax
import jax.numpy as jnp
from jax import lax
from jax.experimental import pallas as pl
from jax.experimental.pallas import tpu as pltpu

F32 = jnp.float32
BF16 = jnp.bfloat16

HEAD_DIM = 128
NSA_HEADS = 16
NSA_KV_HEADS = 4
NSA_GROUP = NSA_HEADS // NSA_KV_HEADS
NSA_WIDTH = NSA_HEADS * HEAD_DIM
NSA_KV_WIDTH = NSA_KV_HEADS * HEAD_DIM
CMP_BLOCK = 32
CMP_STRIDE = 16
CMP_HIDDEN = 256
SEL_BLOCK = 64
SEL_TOPK = 16
WINDOW = 512
GDN_HEADS = 16
GDN_DK = 128
GDN_DV = 128
GDN_KEY_WIDTH = GDN_HEADS * GDN_DK
GDN_VAL_WIDTH = GDN_HEADS * GDN_DV
GDN_CONV = 4
GDN_CHUNK = 64
SC_WIDTH = 2048
SC_CONV = 3
N_BRANCH = 3
NORM_EPS = 1e-6
NEG_INF = -1e30
FORCE_SCORE = 1e6

LANES = 128
SUBLANES = 8
N_BLK_LANES = 128
MIB = 1024 * 1024

_OFF_QA = 0
_OFF_KVC = _OFF_QA + NSA_WIDTH
_OFF_KVS = _OFF_KVC + 2 * NSA_KV_WIDTH
_OFF_KVW = _OFF_KVS + 2 * NSA_KV_WIDTH
_OFF_GATE = _OFF_KVW + 2 * NSA_KV_WIDTH
_OFF_ZA = _OFF_GATE + 3 * NSA_HEADS
_OFF_QB = _OFF_ZA + NSA_WIDTH
_OFF_AB = _OFF_QB + 2 * GDN_KEY_WIDTH + GDN_VAL_WIDTH
_OFF_BETA = _OFF_AB + GDN_HEADS
_OFF_ZB = _OFF_BETA + GDN_HEADS
_OFF_SC = _OFF_ZB + GDN_VAL_WIDTH
_OFF_MERGE = _OFF_SC + 4 * SC_WIDTH

_SM_A = 0
_SM_BETA = GDN_HEADS
_SM_GATE = 2 * GDN_HEADS


def _cparams(sem, vmem_mib):
    return pltpu.CompilerParams(dimension_semantics=sem, vmem_limit_bytes=vmem_mib * MIB)


def _sigmoid(x):
    return jax.nn.sigmoid(x)


def _silu(x):
    return x * jax.nn.sigmoid(x)


def _dot(a, b):
    return jnp.dot(a, b, preferred_element_type=F32)


def _dot_nt(a, b):
    return lax.dot_general(a, b, (((1,), (1,)), ((), ())), preferred_element_type=F32)


def _dot_tn(a, b):
    return lax.dot_general(a, b, (((0,), (0,)), ((), ())), preferred_element_type=F32)


def _head_rms(x, w):
    return x * lax.rsqrt(jnp.mean(x * x, axis=-1, keepdims=True) + NORM_EPS) * w


def _rmsnorm_kernel(x_ref, w_ref, o_ref):
    x = x_ref[...]
    y = x * lax.rsqrt(jnp.mean(x * x, axis=-1, keepdims=True) + NORM_EPS)
    o_ref[...] = (y * w_ref[...]).astype(o_ref.dtype)


def _rmsnorm(x, w, tm=256):
    t, d = x.shape
    return pl.pallas_call(
        _rmsnorm_kernel,
        grid=(t // tm,),
        in_specs=[pl.BlockSpec((tm, d), lambda i: (i, 0)), pl.BlockSpec((1, d), lambda i: (0, 0))],
        out_specs=pl.BlockSpec((tm, d), lambda i: (i, 0)),
        out_shape=jax.ShapeDtypeStruct((t, d), BF16),
        compiler_params=_cparams(("parallel",), 32),
        name="rmsnorm",
    )(x, w.reshape(1, d))


def _mm_kernel(a_ref, b_ref, o_ref):
    o_ref[...] = _dot(a_ref[...], b_ref[...]).astype(o_ref.dtype)


def _matmul(a, b, out_dtype, tm, tn, name):
    m, k = a.shape
    n = b.shape[1]
    tm = min(tm, m)
    tn = min(tn, n)
    return pl.pallas_call(
        _mm_kernel,
        grid=(m // tm, n // tn),
        in_specs=[pl.BlockSpec((tm, k), lambda i, j: (i, 0)), pl.BlockSpec((k, tn), lambda i, j: (0, j))],
        out_specs=pl.BlockSpec((tm, tn), lambda i, j: (i, j)),
        out_shape=jax.ShapeDtypeStruct((m, n), out_dtype),
        compiler_params=_cparams(("parallel", "arbitrary"), 48),
        name=name,
    )(a, b)


def _mm_res_kernel(a_ref, b_ref, r_ref, o_ref):
    o_ref[...] = r_ref[...] + _dot(a_ref[...], b_ref[...])


def _matmul_residual(a, b, r, tm, tn):
    m, k = a.shape
    n = b.shape[1]
    tm = min(tm, m)
    return pl.pallas_call(
        _mm_res_kernel,
        grid=(m // tm, n // tn),
        in_specs=[pl.BlockSpec((tm, k), lambda i, j: (i, 0)), pl.BlockSpec((k, tn), lambda i, j: (0, j)),
                  pl.BlockSpec((tm, tn), lambda i, j: (i, j))],
        out_specs=pl.BlockSpec((tm, tn), lambda i, j: (i, j)),
        out_shape=jax.ShapeDtypeStruct((m, n), F32),
        compiler_params=_cparams(("parallel", "arbitrary"), 48),
        name="out_proj_residual",
    )(a, b, r)


def _merge_kernel(oa_ref, ob_ref, oc_ref, wa_ref, wb_ref, wc_ref, ga_ref, gb_ref, gc_ref, o_ref):
    acc = _sigmoid(ga_ref[...]) * _dot(oa_ref[...], wa_ref[...])
    acc = acc + _sigmoid(gb_ref[...]) * _dot(ob_ref[...], wb_ref[...])
    acc = acc + _sigmoid(gc_ref[...]) * _dot(oc_ref[...], wc_ref[...])
    o_ref[...] = acc.astype(o_ref.dtype)


def _merge(o_a, o_b, o_c, wa, wb, wc, gates, tm=512, tn=512):
    m, k = o_a.shape
    n = wa.shape[1]
    tm = min(tm, m)
    nb = n // tn
    a_spec = pl.BlockSpec((tm, k), lambda i, j: (i, 0))
    w_spec = pl.BlockSpec((k, tn), lambda i, j: (0, j))
    g_specs = [pl.BlockSpec((tm, tn), functools.partial(lambda i, j, br: (i, br * nb + j), br=br))
               for br in range(N_BRANCH)]
    return pl.pallas_call(
        _merge_kernel,
        grid=(m // tm, nb),
        in_specs=[a_spec, a_spec, a_spec, w_spec, w_spec, w_spec] + g_specs,
        out_specs=pl.BlockSpec((tm, tn), lambda i, j: (i, j)),
        out_shape=jax.ShapeDtypeStruct((m, n), BF16),
        compiler_params=_cparams(("parallel", "arbitrary"), 48),
        name="branch_merge",
    )(o_a, o_b, o_c, wa, wb, wc, gates, gates, gates)


def _nsa_prep_kernel(p_ref, nw_ref, q_ref, ks_ref, vs_ref, kw_ref, vw_ref):
    nw = nw_ref[...]
    for h in range(NSA_HEADS):
        sl = slice(h * HEAD_DIM, (h + 1) * HEAD_DIM)
        q_ref[:, sl] = _head_rms(p_ref[:, sl], nw[0:1]).astype(BF16)
    for g in range(NSA_KV_HEADS):
        sl = slice(g * HEAD_DIM, (g + 1) * HEAD_DIM)
        ks = p_ref[:, _OFF_KVS + g * HEAD_DIM:_OFF_KVS + (g + 1) * HEAD_DIM]
        ks_ref[:, sl] = _head_rms(ks, nw[2:3]).astype(BF16)
        vs_ref[:, sl] = p_ref[:, _OFF_KVS + NSA_KV_WIDTH + g * HEAD_DIM:
                              _OFF_KVS + NSA_KV_WIDTH + (g + 1) * HEAD_DIM].astype(BF16)
        kw = p_ref[:, _OFF_KVW + g * HEAD_DIM:_OFF_KVW + (g + 1) * HEAD_DIM]
        kw_ref[:, sl] = _head_rms(kw, nw[3:4]).astype(BF16)
        vw_ref[:, sl] = p_ref[:, _OFF_KVW + NSA_KV_WIDTH + g * HEAD_DIM:
                              _OFF_KVW + NSA_KV_WIDTH + (g + 1) * HEAD_DIM].astype(BF16)


def _nsa_prep(p_nsa, qk_norm, tm=256):
    t, width = p_nsa.shape
    row = lambda i: (i, 0)
    return pl.pallas_call(
        _nsa_prep_kernel,
        grid=(t // tm,),
        in_specs=[pl.BlockSpec((tm, width), row), pl.BlockSpec((4, HEAD_DIM), lambda i: (0, 0))],
        out_specs=[pl.BlockSpec((tm, NSA_WIDTH), row)] + [pl.BlockSpec((tm, NSA_KV_WIDTH), row)] * 4,
        out_shape=[jax.ShapeDtypeStruct((t, NSA_WIDTH), BF16)]
        + [jax.ShapeDtypeStruct((t, NSA_KV_WIDTH), BF16)] * 4,
        compiler_params=_cparams(("parallel",), 40),
        name="nsa_prep",
    )(p_nsa, qk_norm)


def _compress_kernel(x_ref, pos_ref, w1_ref, w2_ref, nw_ref, o_ref, *, n_chunk):
    kv = pl.program_id(0)
    x = x_ref[0, 0]
    lo = _dot((x + pos_ref[0, 0:1, :]).astype(BF16), w1_ref[0, 0])
    hi = _dot((x + pos_ref[0, 1:2, :]).astype(BF16), w1_ref[0, 1])
    hidden = _silu(lo + pltpu.roll(hi, shift=n_chunk - 1, axis=0))
    out = _dot(hidden.astype(BF16), w2_ref[0])
    normed = _head_rms(out, nw_ref[...])
    o_ref[0, 0] = jnp.where(kv == 0, normed, out).astype(BF16)


def _compress(kv_c, cmp_pos, cmp_w1, cmp_w2, k_norm_w):
    t = kv_c.shape[0]
    n_chunk = t // CMP_STRIDE
    half = CMP_STRIDE * HEAD_DIM
    x = kv_c.reshape(n_chunk, CMP_STRIDE, 2, NSA_KV_HEADS, HEAD_DIM).transpose(2, 3, 0, 1, 4)
    x = x.reshape(2, NSA_KV_HEADS, n_chunk, half)
    pos = cmp_pos.reshape(2, 2, half)
    w1 = cmp_w1.reshape(2, 2, half, CMP_HIDDEN).astype(BF16)
    w2 = cmp_w2.astype(BF16)
    return pl.pallas_call(
        functools.partial(_compress_kernel, n_chunk=n_chunk),
        grid=(2, NSA_KV_HEADS),
        in_specs=[pl.BlockSpec((1, 1, n_chunk, half), lambda a, g: (a, g, 0, 0)),
                  pl.BlockSpec((1, 2, half), lambda a, g: (a, 0, 0)),
                  pl.BlockSpec((1, 2, half, CMP_HIDDEN), lambda a, g: (a, 0, 0, 0)),
                  pl.BlockSpec((1, CMP_HIDDEN, HEAD_DIM), lambda a, g: (a, 0, 0)),
                  pl.BlockSpec((1, HEAD_DIM), lambda a, g: (0, 0))],
        out_specs=pl.BlockSpec((1, 1, n_chunk, HEAD_DIM), lambda a, g: (a, g, 0, 0)),
        out_shape=jax.ShapeDtypeStruct((2, NSA_KV_HEADS, n_chunk, HEAD_DIM), BF16),
        compiler_params=_cparams(("arbitrary", "arbitrary"), 40),
        name="nsa_compress",
    )(x, pos, w1, w2, k_norm_w.reshape(1, HEAD_DIM))


def _cmp_attn_kernel(slopes_ref, q_ref, k_ref, v_ref, cov_ref, o_ref, sel_ref, *, tq, n_chunk, scale):
    g = pl.program_id(0)
    qi = pl.program_id(1)
    k = k_ref[0, 0]
    v = v_ref[0, 0]
    t_pos = qi * tq + lax.broadcasted_iota(jnp.int32, (tq, n_chunk), 0)
    n_idx = lax.broadcasted_iota(jnp.int32, (tq, n_chunk), 1)
    dist = t_pos - (n_idx * CMP_STRIDE + (CMP_BLOCK - 1))
    mask = dist >= 0
    maskf = mask.astype(F32)
    distf = dist.astype(F32)
    psum = jnp.zeros((tq, n_chunk), F32)
    for z in range(NSA_GROUP):
        slope = slopes_ref[g * NSA_GROUP + z]
        sl = slice(z * HEAD_DIM, (z + 1) * HEAD_DIM)
        s = _dot_nt(q_ref[:, sl], k) * scale - slope * distf
        s = jnp.where(mask, s, NEG_INF)
        m = jnp.max(s, axis=-1, keepdims=True)
        e = jnp.exp(s - m) * maskf
        p = e / jnp.maximum(jnp.sum(e, axis=-1, keepdims=True), 1e-30)
        o_ref[:, sl] = _dot(p.astype(BF16), v)
        psum = psum + p

    p_hi = psum.astype(BF16)
    p_lo = (psum - p_hi.astype(F32)).astype(BF16)
    cov = cov_ref[...]
    imp = _dot(p_hi, cov) + _dot(p_lo, cov)

    j = lax.broadcasted_iota(jnp.int32, (tq, N_BLK_LANES), 1)
    cur = (qi * tq + lax.broadcasted_iota(jnp.int32, (tq, N_BLK_LANES), 0)) // SEL_BLOCK
    valid = j <= cur
    forced = (j == 0) | (j == cur) | (j == cur - 1)
    val = jnp.where(valid, jnp.where(forced, FORCE_SCORE, imp), -1.0)
    sel = jnp.zeros((tq, N_BLK_LANES), F32)
    jf = j.astype(F32)
    for _ in range(SEL_TOPK):
        m = jnp.max(val, axis=-1, keepdims=True)
        idx = jnp.min(jnp.where(val == m, jf, float(N_BLK_LANES)), axis=-1, keepdims=True)
        hit = jf == idx
        sel = jnp.where(hit & (m > -0.5), 1.0, sel)
        val = jnp.where(hit, -2.0, val)
    sel_ref[0] = sel.astype(BF16)


def _cmp_attention(q, kv_cmp, slopes, tq=256):
    t = q.shape[0]
    n_chunk = kv_cmp.shape[2]
    n_cmp = n_chunk - CMP_BLOCK // CMP_STRIDE + 1
    n_blk = t // SEL_BLOCK
    assert n_blk <= N_BLK_LANES
    tq = min(tq, t)
    cs = jnp.arange(n_chunk)[:, None] * CMP_STRIDE
    bs = jnp.arange(N_BLK_LANES)[None, :] * SEL_BLOCK
    cover = ((cs <= bs + SEL_BLOCK - 1) & (cs + CMP_BLOCK - 1 >= bs)
             & (jnp.arange(n_chunk)[:, None] < n_cmp) & (jnp.arange(N_BLK_LANES)[None, :] < n_blk))
    cover = cover.astype(BF16)
    return pl.pallas_call(
        functools.partial(_cmp_attn_kernel, tq=tq, n_chunk=n_chunk, scale=HEAD_DIM ** -0.5),
        grid=(NSA_KV_HEADS, t // tq),
        in_specs=[pl.BlockSpec(memory_space=pltpu.SMEM),
                  pl.BlockSpec((tq, NSA_GROUP * HEAD_DIM), lambda g, i: (i, g)),
                  pl.BlockSpec((1, 1, n_chunk, HEAD_DIM), lambda g, i: (0, g, 0, 0)),
                  pl.BlockSpec((1, 1, n_chunk, HEAD_DIM), lambda g, i: (1, g, 0, 0)),
                  pl.BlockSpec((n_chunk, N_BLK_LANES), lambda g, i: (0, 0))],
        out_specs=[pl.BlockSpec((tq, NSA_GROUP * HEAD_DIM), lambda g, i: (i, g)),
                   pl.BlockSpec((1, tq, N_BLK_LANES), lambda g, i: (g, i, 0))],
        out_shape=[jax.ShapeDtypeStruct((t, NSA_WIDTH), F32),
                   jax.ShapeDtypeStruct((NSA_KV_HEADS, t, N_BLK_LANES), BF16)],
        compiler_params=_cparams(("parallel", "parallel"), 40),
        name="nsa_cmp_attn",
    )(slopes, q, kv_cmp, kv_cmp, cover)


def _sel_attn_kernel(slopes_ref, q_ref, k_ref, v_ref, sel_ref, o_ref, m_sc, l_sc, acc_sc, *, tq, tk, scale):
    g = pl.program_id(0)
    qi = pl.program_id(1)
    kj = pl.program_id(2)

    @pl.when(kj == 0)
    def _():
        m_sc[...] = jnp.full(m_sc.shape, NEG_INF, F32)
        l_sc[...] = jnp.zeros(l_sc.shape, F32)
        acc_sc[...] = jnp.zeros(acc_sc.shape, F32)

    last = (qi * tq + tq - 1) // tk

    @pl.when(kj <= last)
    def _():
        t_pos = qi * tq + lax.broadcasted_iota(jnp.int32, (tq, tk), 0)
        k_pos = kj * tk + lax.broadcasted_iota(jnp.int32, (tq, tk), 1)
        dist = t_pos - k_pos
        blk_of_key = kj * (tk // SEL_BLOCK) + lax.broadcasted_iota(jnp.int32, (N_BLK_LANES, tk), 1) // SEL_BLOCK
        expand = (lax.broadcasted_iota(jnp.int32, (N_BLK_LANES, tk), 0) == blk_of_key).astype(BF16)
        chosen = _dot(sel_ref[0], expand)
        mask = (chosen > 0.5) & (dist >= 0)
        distf = dist.astype(F32)
        k = k_ref[...]
        v = v_ref[...]
        for z in range(NSA_GROUP):
            slope = slopes_ref[g * NSA_GROUP + z]
            s = _dot_nt(q_ref[:, z * HEAD_DIM:(z + 1) * HEAD_DIM], k) * scale - slope * distf
            s = jnp.where(mask, s, NEG_INF)
            m_prev = m_sc[z]
            m_new = jnp.maximum(m_prev, jnp.max(s, axis=-1, keepdims=True))
            alpha = jnp.exp(m_prev - m_new)
            e = jnp.where(mask, jnp.exp(s - m_new), 0.0)
            l_sc[z] = alpha * l_sc[z] + jnp.sum(e, axis=-1, keepdims=True)
            acc_sc[z] = alpha * acc_sc[z] + _dot(e.astype(BF16), v)
            m_sc[z] = m_new

    @pl.when(kj == pl.num_programs(2) - 1)
    def _():
        for z in range(NSA_GROUP):
            o_ref[:, z * HEAD_DIM:(z + 1) * HEAD_DIM] = acc_sc[z] / jnp.maximum(l_sc[z], 1e-30)


def _sel_attention(q, k_s, v_s, sel, slopes, tq=128, tk=512):
    t = q.shape[0]
    tq = min(tq, t)
    tk = min(tk, t)

    def kv_map(g, i, j):
        return (jnp.minimum(j, (i * tq + tq - 1) // tk), g)

    return pl.pallas_call(
        functools.partial(_sel_attn_kernel, tq=tq, tk=tk, scale=HEAD_DIM ** -0.5),
        grid=(NSA_KV_HEADS, t // tq, t // tk),
        in_specs=[pl.BlockSpec(memory_space=pltpu.SMEM),
                  pl.BlockSpec((tq, NSA_GROUP * HEAD_DIM), lambda g, i, j: (i, g)),
                  pl.BlockSpec((tk, HEAD_DIM), kv_map),
                  pl.BlockSpec((tk, HEAD_DIM), kv_map),
                  pl.BlockSpec((1, tq, N_BLK_LANES), lambda g, i, j: (g, i, 0))],
        out_specs=pl.BlockSpec((tq, NSA_GROUP * HEAD_DIM), lambda g, i, j: (i, g)),
        out_shape=jax.ShapeDtypeStruct((t, NSA_WIDTH), F32),
        scratch_shapes=[pltpu.VMEM((NSA_GROUP, tq, 1), F32), pltpu.VMEM((NSA_GROUP, tq, 1), F32),
                        pltpu.VMEM((NSA_GROUP, tq, HEAD_DIM), F32)],
        compiler_params=_cparams(("parallel", "parallel", "arbitrary"), 40),
        name="nsa_sel_attn",
    )(slopes, q, k_s, v_s, sel)


def _win_attn_kernel(slopes_ref, q_ref, *refs, tq, n_tile, scale):
    k_refs = refs[:n_tile]
    v_refs = refs[n_tile:2 * n_tile]
    o_ref = refs[2 * n_tile]
    g = pl.program_id(0)
    qi = pl.program_id(1)
    span = n_tile * tq
    k = jnp.concatenate([r[...] for r in k_refs], axis=0)
    v = jnp.concatenate([r[...] for r in v_refs], axis=0)
    t_pos = qi * tq + lax.broadcasted_iota(jnp.int32, (tq, span), 0)
    k_pos = (qi - (n_tile - 1)) * tq + lax.broadcasted_iota(jnp.int32, (tq, span), 1)
    dist = t_pos - k_pos
    mask = (dist >= 0) & (dist < WINDOW) & (k_pos >= 0)
    maskf = mask.astype(F32)
    distf = dist.astype(F32)
    for z in range(NSA_GROUP):
        slope = slopes_ref[g * NSA_GROUP + z]
        sl = slice(z * HEAD_DIM, (z + 1) * HEAD_DIM)
        s = _dot_nt(q_ref[:, sl], k) * scale - slope * distf
        s = jnp.where(mask, s, NEG_INF)
        m = jnp.max(s, axis=-1, keepdims=True)
        e = jnp.exp(s - m) * maskf
        p = e / jnp.maximum(jnp.sum(e, axis=-1, keepdims=True), 1e-30)
        o_ref[:, sl] = _dot(p.astype(BF16), v)


def _win_attention(q, k_w, v_w, slopes, tq=256):
    t = q.shape[0]
    tq = min(tq, t)
    n_tile = -(-WINDOW // tq) + 1

    def kv_spec(c):
        return pl.BlockSpec((tq, HEAD_DIM), lambda g, i: (jnp.maximum(i - (n_tile - 1) + c, 0), g))

    return pl.pallas_call(
        functools.partial(_win_attn_kernel, tq=tq, n_tile=n_tile, scale=HEAD_DIM ** -0.5),
        grid=(NSA_KV_HEADS, t // tq),
        in_specs=[pl.BlockSpec(memory_space=pltpu.SMEM),
                  pl.BlockSpec((tq, NSA_GROUP * HEAD_DIM), lambda g, i: (i, g))]
        + [kv_spec(c) for c in range(n_tile)] * 2,
        out_specs=pl.BlockSpec((tq, NSA_GROUP * HEAD_DIM), lambda g, i: (i, g)),
        out_shape=jax.ShapeDtypeStruct((t, NSA_WIDTH), F32),
        compiler_params=_cparams(("parallel", "parallel"), 40),
        name="nsa_win_attn",
    )(slopes, q, *([k_w] * n_tile), *([v_w] * n_tile))


def _nsa_combine_kernel(sm_ref, ex_ref, oc_ref, os_ref, ow_ref, z_ref, o_ref):
    logits = sm_ref[...]
    hi = logits.astype(BF16)
    lo = (logits - hi.astype(F32)).astype(BF16)
    acc = None
    for br, branch_ref in enumerate((oc_ref, os_ref, ow_ref)):
        ex = ex_ref[br]
        gate = _sigmoid(_dot(hi, ex) + _dot(lo, ex))
        term = gate * branch_ref[...]
        acc = term if acc is None else acc + term
    o_ref[...] = (acc * _silu(z_ref[...])).astype(o_ref.dtype)


def _nsa_combine(small, o_cmp, o_slc, o_win, z_a, tm=256):
    t = small.shape[0]
    tm = min(tm, t)
    lane = jnp.arange(LANES)[None, :, None]
    br = jnp.arange(N_BRANCH)[:, None, None]
    head = (jnp.arange(NSA_WIDTH) // HEAD_DIM)[None, None, :]
    expand = (lane == _SM_GATE + head * N_BRANCH + br).astype(BF16)
    row = lambda i: (i, 0)
    wide = pl.BlockSpec((tm, NSA_WIDTH), row)
    return pl.pallas_call(
        _nsa_combine_kernel,
        grid=(t // tm,),
        in_specs=[pl.BlockSpec((tm, LANES), row),
                  pl.BlockSpec((N_BRANCH, LANES, NSA_WIDTH), lambda i: (0, 0, 0)),
                  wide, wide, wide, wide],
        out_specs=wide,
        out_shape=jax.ShapeDtypeStruct((t, NSA_WIDTH), BF16),
        compiler_params=_cparams(("parallel",), 40),
        name="nsa_combine",
    )(small, expand, o_cmp, o_slc, o_win, z_a)


def _shift_rows(cur, halo, s):
    rolled = pltpu.roll(cur, shift=s, axis=0)
    halo_rolled = pltpu.roll(halo, shift=s, axis=0)
    row = lax.broadcasted_iota(jnp.int32, halo.shape, 0)
    head = jnp.where(row < s, halo_rolled, rolled[0:SUBLANES])
    return jnp.concatenate([head, rolled[SUBLANES:]], axis=0)


def _causal_conv(cur, halo, w_ref, k):
    acc = None
    for j in range(k):
        s = k - 1 - j
        term = (cur if s == 0 else _shift_rows(cur, halo, s)) * w_ref[j:j + 1, :]
        acc = term if acc is None else acc + term
    return acc


def _gdn_prep_kernel(p_ref, halo_ref, cw_ref, sm_ref, alog_ref, dtb_ref, q_ref, k_ref, v_ref, gb_ref):
    i = pl.program_id(0)
    cur = p_ref[...]
    halo = jnp.where(i > 0, halo_ref[...], 0.0)
    y = _silu(_causal_conv(cur, halo, cw_ref, GDN_CONV))
    for h in range(GDN_HEADS):
        sl = slice(h * GDN_DK, (h + 1) * GDN_DK)
        qh = y[:, sl]
        q_ref[:, sl] = qh * lax.rsqrt(jnp.sum(qh * qh, axis=-1, keepdims=True) + NORM_EPS) * (GDN_DK ** -0.5)
        kh = y[:, GDN_KEY_WIDTH + h * GDN_DK:GDN_KEY_WIDTH + (h + 1) * GDN_DK]
        k_ref[:, sl] = kh * lax.rsqrt(jnp.sum(kh * kh, axis=-1, keepdims=True) + NORM_EPS)
    v_ref[...] = y[:, 2 * GDN_KEY_WIDTH:]
    sm = sm_ref[...]
    lane = lax.broadcasted_iota(jnp.int32, sm.shape, 1)
    gdecay = -jnp.exp(alog_ref[...]) * jax.nn.softplus(sm + dtb_ref[...])
    beta = _sigmoid(sm)
    gb_ref[...] = jnp.where(lane < _SM_BETA, gdecay, jnp.where(lane < _SM_GATE, beta, 0.0))


def _gdn_prep(p_gdn, small, conv_w, a_log, dt_bias, tm=256):
    t, width = p_gdn.shape
    tm = min(tm, t)
    hb = tm // SUBLANES
    row = lambda i: (i, 0)
    alog = jnp.zeros((1, LANES), F32).at[0, _SM_A:_SM_A + GDN_HEADS].set(a_log)
    dtb = jnp.zeros((1, LANES), F32).at[0, _SM_A:_SM_A + GDN_HEADS].set(dt_bias)
    const = lambda i: (0, 0)
    return pl.pallas_call(
        _gdn_prep_kernel,
        grid=(t // tm,),
        in_specs=[pl.BlockSpec((tm, width), row),
                  pl.BlockSpec((SUBLANES, width), lambda i: (jnp.maximum(i * hb - 1, 0), 0)),
                  pl.BlockSpec((GDN_CONV, width), const),
                  pl.BlockSpec((tm, LANES), row),
                  pl.BlockSpec((1, LANES), const), pl.BlockSpec((1, LANES), const)],
        out_specs=[pl.BlockSpec((tm, GDN_KEY_WIDTH), row), pl.BlockSpec((tm, GDN_KEY_WIDTH), row),
                   pl.BlockSpec((tm, GDN_VAL_WIDTH), row), pl.BlockSpec((tm, LANES), row)],
        out_shape=[jax.ShapeDtypeStruct((t, GDN_KEY_WIDTH), F32), jax.ShapeDtypeStruct((t, GDN_KEY_WIDTH), F32),
                   jax.ShapeDtypeStruct((t, GDN_VAL_WIDTH), F32), jax.ShapeDtypeStruct((t, LANES), F32)],
        compiler_params=_cparams(("parallel",), 48),
        name="gdn_prep",
    )(p_gdn, p_gdn, conv_w, small, alog, dtb)


def _gdn_chunk_kernel(q_ref, k_ref, v_ref, gb_ref, z_ref, nw_ref, o_ref, state, *, heads_per_step):
    n = pl.program_id(0)
    hb = pl.program_id(1)
    c = GDN_CHUNK

    @pl.when(n == 0)
    def _():
        for i in range(heads_per_step):
            state[hb * heads_per_step + i] = jnp.zeros((GDN_DK, GDN_DV), F32)

    gb = gb_ref[...]
    row = lax.broadcasted_iota(jnp.int32, gb.shape, 0)
    gcum = gb
    shift = 1
    while shift < c:
        gcum = gcum + jnp.where(row >= shift, pltpu.roll(gcum, shift=shift, axis=0), 0.0)
        shift *= 2
    lane = lax.broadcasted_iota(jnp.int32, gb.shape, 1)
    ri = lax.broadcasted_iota(jnp.int32, (c, c), 0)
    ci = lax.broadcasted_iota(jnp.int32, (c, c), 1)
    tri = ri >= ci
    strict = ri > ci
    eye = ri == ci
    eyef = eye.astype(F32)
    nw = nw_ref[...]

    for i in range(heads_per_step):
        head = hb * heads_per_step + i
        sl = slice(i * GDN_DK, (i + 1) * GDN_DK)
        gcol = jnp.sum(jnp.where(lane == _SM_A + head, gcum, 0.0), axis=-1, keepdims=True)
        beta = jnp.sum(jnp.where(lane == _SM_BETA + head, gb, 0.0), axis=-1, keepdims=True)
        gmat = jnp.broadcast_to(gcol, (c, c))
        grow = jnp.sum(jnp.where(eye, gmat, 0.0), axis=0, keepdims=True)
        decay = jnp.where(tri, jnp.exp(jnp.where(tri, gmat - grow, 0.0)), 0.0)
        glast = gcol[c - 1:c, :]
        egc = jnp.exp(gcol)

        q = q_ref[:, sl]
        k = k_ref[:, sl]
        v = v_ref[:, sl]
        k16 = k.astype(BF16)
        lmat = jnp.where(strict, beta * _dot_nt(k16, k16) * decay, 0.0)
        attn = _dot_nt(q.astype(BF16), k16) * decay

        pw = -lmat
        inv = eyef + pw
        span = 2
        while span < c:
            pw16 = pw.astype(BF16)
            pw = _dot(pw16, pw16)
            inv = inv + _dot(inv.astype(BF16), pw.astype(BF16))
            span *= 2
        inv16 = inv.astype(BF16)
        u = _dot(inv16, (v * beta).astype(BF16))
        w = _dot(inv16, (k * (beta * egc)).astype(BF16))

        s_prev = state[head]
        s16 = s_prev.astype(BF16)
        v_new = u - _dot(w.astype(BF16), s16)
        v_new16 = v_new.astype(BF16)
        o = _dot((q * egc).astype(BF16), s16) + _dot(attn.astype(BF16), v_new16)
        k_end = k * jnp.exp(glast - gcol)
        state[head] = s_prev * jnp.exp(glast) + _dot_tn(k_end.astype(BF16), v_new16)

        o_ref[:, sl] = (_head_rms(o, nw) * _silu(z_ref[:, sl])).astype(o_ref.dtype)


def _gdn_chunks(q, k, v, gb, z_b, norm_w, heads_per_step=4):
    t = q.shape[0]
    c = GDN_CHUNK
    wblk = heads_per_step * GDN_DK
    col = lambda n, h: (n, h)
    wide = pl.BlockSpec((c, wblk), col)
    return pl.pallas_call(
        functools.partial(_gdn_chunk_kernel, heads_per_step=heads_per_step),
        grid=(t // c, GDN_HEADS // heads_per_step),
        in_specs=[wide, wide, wide, pl.BlockSpec((c, LANES), lambda n, h: (n, 0)), wide,
                  pl.BlockSpec((1, GDN_DV), lambda n, h: (0, 0))],
        out_specs=wide,
        out_shape=jax.ShapeDtypeStruct((t, GDN_VAL_WIDTH), BF16),
        scratch_shapes=[pltpu.VMEM((GDN_HEADS, GDN_DK, GDN_DV), F32)],
        compiler_params=_cparams(("arbitrary", "arbitrary"), 32),
        name="gdn_chunks",
    )(q, k, v, gb, z_b, norm_w.reshape(1, GDN_DV))


def _short_conv_kernel(bg_ref, cg_ref, x_ref, z_ref, cgh_ref, xh_ref, cw_ref, o_ref):
    i = pl.program_id(0)
    cur = cg_ref[...] * x_ref[...]
    halo = jnp.where(i > 0, cgh_ref[...] * xh_ref[...], 0.0)
    y = _causal_conv(cur, halo, cw_ref, SC_CONV)
    o_ref[...] = (bg_ref[...] * y * _silu(z_ref[...])).astype(o_ref.dtype)


def _short_conv(p_sc, conv_w, tm=256):
    t = p_sc.shape[0]
    tm = min(tm, t)
    hb = tm // SUBLANES
    main = lambda c: pl.BlockSpec((tm, SC_WIDTH), functools.partial(lambda i, c: (i, c), c=c))
    halo = lambda c: pl.BlockSpec((SUBLANES, SC_WIDTH),
                                  functools.partial(lambda i, c: (jnp.maximum(i * hb - 1, 0), c), c=c))
    return pl.pallas_call(
        _short_conv_kernel,
        grid=(t // tm,),
        in_specs=[main(0), main(1), main(2), main(3), halo(1), halo(2),
                  pl.BlockSpec((SC_CONV, SC_WIDTH), lambda i: (0, 0))],
        out_specs=pl.BlockSpec((tm, SC_WIDTH), lambda i: (i, 0)),
        out_shape=jax.ShapeDtypeStruct((t, SC_WIDTH), BF16),
        compiler_params=_cparams(("parallel",), 40),
        name="short_conv",
    )(p_sc, p_sc, p_sc, p_sc, p_sc, p_sc, conv_w)


def _layer(x, norm_w, w_in, nsa_qk_norm, cmp_pos, cmp_w1, cmp_w2, gdn_conv_w, gdn_a_log, gdn_dt_bias,
           gdn_norm_w, sc_conv_w, w_branch_nsa, w_branch_gdn, w_branch_sc, w_out, slopes):
    d = x.shape[1]
    cast = lambda w: w.astype(BF16)
    w_nsa = cast(w_in[:, _OFF_QA:_OFF_GATE])
    w_za = cast(w_in[:, _OFF_ZA:_OFF_QB])
    w_gdn = cast(w_in[:, _OFF_QB:_OFF_AB])
    w_zb = cast(w_in[:, _OFF_ZB:_OFF_SC])
    w_sc = cast(w_in[:, _OFF_SC:_OFF_MERGE])
    w_mg = cast(w_in[:, _OFF_MERGE:])
    w_small = jnp.concatenate(
        [w_in[:, _OFF_AB:_OFF_ZB], w_in[:, _OFF_GATE:_OFF_ZA],
         jnp.zeros((d, LANES - 2 * GDN_HEADS - 3 * NSA_HEADS), F32)], axis=1).astype(BF16)

    hn = _rmsnorm(x, norm_w)
    p_nsa = _matmul(hn, w_nsa, F32, 1024, 512, "in_proj_nsa")
    z_a = _matmul(hn, w_za, F32, 1024, 512, "in_proj_za")
    p_gdn = _matmul(hn, w_gdn, F32, 1024, 512, "in_proj_gdn")
    z_b = _matmul(hn, w_zb, F32, 1024, 512, "in_proj_zb")
    p_sc = _matmul(hn, w_sc, F32, 1024, 512, "in_proj_sc")
    p_mg = _matmul(hn, w_mg, F32, 1024, 512, "in_proj_merge")
    small = _matmul(hn, w_small, F32, 1024, LANES, "in_proj_small")

    q, k_s, v_s, k_w, v_w = _nsa_prep(p_nsa, nsa_qk_norm)
    kv_cmp = _compress(p_nsa[:, _OFF_KVC:_OFF_KVS], cmp_pos, cmp_w1, cmp_w2, nsa_qk_norm[1])
    o_cmp, sel = _cmp_attention(q, kv_cmp, slopes)
    o_slc = _sel_attention(q, k_s, v_s, sel, slopes)
    o_win = _win_attention(q, k_w, v_w, slopes)
    o_a = _nsa_combine(small, o_cmp, o_slc, o_win, z_a)

    q_b, k_b, v_b, gb = _gdn_prep(p_gdn, small, gdn_conv_w, gdn_a_log, gdn_dt_bias)
    o_b = _gdn_chunks(q_b, k_b, v_b, gb, z_b, gdn_norm_w)

    o_c = _short_conv(p_sc, sc_conv_w)

    merged = _merge(o_a, o_b, o_c, cast(w_branch_nsa), cast(w_branch_gdn), cast(w_branch_sc), p_mg)
    return _matmul_residual(merged, cast(w_out), x, 1024, 512)


def kernel(x, norm_w, w_in, nsa_qk_norm, cmp_pos, cmp_w1, cmp_w2, gdn_conv_w, gdn_a_log, gdn_dt_bias,
           gdn_norm_w, sc_conv_w, w_branch_nsa, w_branch_gdn, w_branch_sc, w_out):
    b, t, d = x.shape
    depth = norm_w.shape[0]
    heads = jnp.arange(1, NSA_HEADS + 1, dtype=F32)
    slopes = jnp.exp2(-8.0 * heads / NSA_HEADS)
    outs = []
    for bi in range(b):
        xb = x[bi]
        for l in range(depth):
            xb = _layer(xb, norm_w[l], w_in[l], nsa_qk_norm[l], cmp_pos[l], cmp_w1[l], cmp_w2[l],
                        gdn_conv_w[l], gdn_a_log[l], gdn_dt_bias[l], gdn_norm_w[l], sc_conv_w[l],
                        w_branch_nsa[l], w_branch_gdn[l], w_branch_sc[l], w_out[l], slopes)
        outs.append(xb)
    return jnp.stack(outs, axis=0)
```

```python
import functools
import math

import jax
import jax.numpy as jnp
from jax import lax
from jax.experimental import pallas as pl
from jax.experimental.pallas import tpu as pltpu

F32 = jnp.float32
BF16 = jnp.bfloat16

HEAD_DIM = 128
NSA_HEADS = 16
NSA_KV_HEADS = 4
NSA_GROUP = NSA_HEADS // NSA_KV_HEADS
NSA_WIDTH = NSA_HEADS * HEAD_DIM
NSA_KV_WIDTH = NSA_KV_HEADS * HEAD_DIM
CMP_BLOCK = 32
CMP_STRIDE = 16
CMP_HIDDEN = 256
SEL_BLOCK = 64
SEL_TOPK = 16
WINDOW = 512
GDN_HEADS = 16
GDN_DK = 128
GDN_DV = 128
GDN_KEY_WIDTH = GDN_HEADS * GDN_DK
GDN_VAL_WIDTH = GDN_HEADS * GDN_DV
GDN_CONV = 4
GDN_CHUNK = 64
SC_WIDTH = 2048
SC_CONV = 3
N_BRANCH = 3
NORM_EPS = 1e-6
NEG_INF = -1e30
MASK_BIAS = -2e30
FORCE_SCORE = 1e6

LANES = 128
SUBLANES = 8
N_BLK_LANES = 128
MIB = 1024 * 1024

_OFF_QA = 0
_OFF_KVC = _OFF_QA + NSA_WIDTH
_OFF_KVS = _OFF_KVC + 2 * NSA_KV_WIDTH
_OFF_KVW = _OFF_KVS + 2 * NSA_KV_WIDTH
_OFF_GATE = _OFF_KVW + 2 * NSA_KV_WIDTH
_OFF_ZA = _OFF_GATE + 3 * NSA_HEADS
_OFF_QB = _OFF_ZA + NSA_WIDTH
_OFF_AB = _OFF_QB + 2 * GDN_KEY_WIDTH + GDN_VAL_WIDTH
_OFF_BETA = _OFF_AB + GDN_HEADS
_OFF_ZB = _OFF_BETA + GDN_HEADS
_OFF_SC = _OFF_ZB + GDN_VAL_WIDTH
_OFF_MERGE = _OFF_SC + 4 * SC_WIDTH

_SM_A = 0
_SM_BETA = GDN_HEADS
_SM_GATE = 2 * GDN_HEADS


def _cparams(sem, vmem_mib):
    return pltpu.CompilerParams(dimension_semantics=sem, vmem_limit_bytes=vmem_mib * MIB)


def _sigmoid(x):
    return jax.nn.sigmoid(x)


def _silu(x):
    return x * jax.nn.sigmoid(x)


def _dot(a, b):
    return jnp.dot(a, b, preferred_element_type=F32)


def _dot_nt(a, b):
    return lax.dot_general(a, b, (((1,), (1,)), ((), ())), preferred_element_type=F32)


def _dot_tn(a, b):
    return lax.dot_general(a, b, (((0,), (0,)), ((), ())), preferred_element_type=F32)


def _head_rms(x, w):
    return x * lax.rsqrt(jnp.mean(x * x, axis=-1, keepdims=True) + NORM_EPS) * w


def _rmsnorm_kernel(x_ref, w_ref, o_ref):
    x = x_ref[...]
    y = x * lax.rsqrt(jnp.mean(x * x, axis=-1, keepdims=True) + NORM_EPS)
    o_ref[...] = (y * w_ref[...]).astype(o_ref.dtype)


def _rmsnorm(x, w, tm=256):
    t, d = x.shape
    return pl.pallas_call(
        _rmsnorm_kernel,
        grid=(t // tm,),
        in_specs=[pl.BlockSpec((tm, d), lambda i: (i, 0)), pl.BlockSpec((1, d), lambda i: (0, 0))],
        out_specs=pl.BlockSpec((tm, d), lambda i: (i, 0)),
        out_shape=jax.ShapeDtypeStruct((t, d), BF16),
        compiler_params=_cparams(("parallel",), 32),
        name="rmsnorm",
    )(x, w.reshape(1, d))


def _mm_kernel(a_ref, b_ref, o_ref):
    o_ref[...] = _dot(a_ref[...], b_ref[...]).astype(o_ref.dtype)


def _matmul(a, b, col_off, n, out_dtype, tm, tn, name):
    m, k = a.shape
    tm = min(tm, m)
    tn = min(tn, n)
    assert col_off % tn == 0 and n % tn == 0
    off = col_off // tn
    return pl.pallas_call(
        _mm_kernel,
        grid=(m // tm, n // tn),
        in_specs=[pl.BlockSpec((tm, k), lambda i, j: (i, 0)), pl.BlockSpec((k, tn), lambda i, j: (0, off + j))],
        out_specs=pl.BlockSpec((tm, tn), lambda i, j: (i, j)),
        out_shape=jax.ShapeDtypeStruct((m, n), out_dtype),
        compiler_params=_cparams(("parallel", "arbitrary"), 48),
        name=name,
    )(a, b)


def _mm_res_kernel(a_ref, b_ref, r_ref, o_ref):
    o_ref[...] = r_ref[...] + _dot(a_ref[...], b_ref[...])


def _matmul_residual(a, b, r, tm, tn):
    m, k = a.shape
    n = b.shape[1]
    tm = min(tm, m)
    return pl.pallas_call(
        _mm_res_kernel,
        grid=(m // tm, n // tn),
        in_specs=[pl.BlockSpec((tm, k), lambda i, j: (i, 0)), pl.BlockSpec((k, tn), lambda i, j: (0, j)),
                  pl.BlockSpec((tm, tn), lambda i, j: (i, j))],
        out_specs=pl.BlockSpec((tm, tn), lambda i, j: (i, j)),
        out_shape=jax.ShapeDtypeStruct((m, n), F32),
        compiler_params=_cparams(("parallel", "arbitrary"), 48),
        name="out_proj_residual",
    )(a, b, r)


def _merge_kernel(oa_ref, ob_ref, oc_ref, wa_ref, wb_ref, wc_ref, ga_ref, gb_ref, gc_ref, o_ref):
    acc = _sigmoid(ga_ref[...]) * _dot(oa_ref[...], wa_ref[...])
    acc = acc + _sigmoid(gb_ref[...]) * _dot(ob_ref[...], wb_ref[...])
    acc = acc + _sigmoid(gc_ref[...]) * _dot(oc_ref[...], wc_ref[...])
    o_ref[...] = acc.astype(o_ref.dtype)


def _merge(o_a, o_b, o_c, wa, wb, wc, gates, tm=512, tn=512):
    m, k = o_a.shape
    n = wa.shape[1]
    tm = min(tm, m)
    nb = n // tn
    a_spec = pl.BlockSpec((tm, k), lambda i, j: (i, 0))
    w_spec = pl.BlockSpec((k, tn), lambda i, j: (0, j))
    g_specs = [pl.BlockSpec((tm, tn), functools.partial(lambda i, j, br: (i, br * nb + j), br=br))
               for br in range(N_BRANCH)]
    return pl.pallas_call(
        _merge_kernel,
        grid=(m // tm, nb),
        in_specs=[a_spec, a_spec, a_spec, w_spec, w_spec, w_spec] + g_specs,
        out_specs=pl.BlockSpec((tm, tn), lambda i, j: (i, j)),
        out_shape=jax.ShapeDtypeStruct((m, n), BF16),
        compiler_params=_cparams(("parallel", "arbitrary"), 48),
        name="branch_merge",
    )(o_a, o_b, o_c, wa, wb, wc, gates, gates, gates)


def _nsa_prep_kernel(p_ref, nw_ref, q_ref, kvc_ref, ks_ref, vs_ref, kw_ref, vw_ref):
    nw = nw_ref[...]
    for h in range(NSA_HEADS):
        sl = slice(h * HEAD_DIM, (h + 1) * HEAD_DIM)
        q_ref[:, sl] = (_head_rms(p_ref[:, sl], nw[0:1]) * (HEAD_DIM ** -0.5)).astype(BF16)
    kvc_ref[...] = p_ref[:, _OFF_KVC:_OFF_KVS]
    for g in range(NSA_KV_HEADS):
        sl = slice(g * HEAD_DIM, (g + 1) * HEAD_DIM)
        ks = p_ref[:, _OFF_KVS + g * HEAD_DIM:_OFF_KVS + (g + 1) * HEAD_DIM]
        ks_ref[:, sl] = _head_rms(ks, nw[2:3]).astype(BF16)
        vs_ref[:, sl] = p_ref[:, _OFF_KVS + NSA_KV_WIDTH + g * HEAD_DIM:
                              _OFF_KVS + NSA_KV_WIDTH + (g + 1) * HEAD_DIM].astype(BF16)
        kw = p_ref[:, _OFF_KVW + g * HEAD_DIM:_OFF_KVW + (g + 1) * HEAD_DIM]
        kw_ref[:, sl] = _head_rms(kw, nw[3:4]).astype(BF16)
        vw_ref[:, sl] = p_ref[:, _OFF_KVW + NSA_KV_WIDTH + g * HEAD_DIM:
                              _OFF_KVW + NSA_KV_WIDTH + (g + 1) * HEAD_DIM].astype(BF16)


def _nsa_prep(p_nsa, qk_norm, tm=256):
    t, width = p_nsa.shape
    row = lambda i: (i, 0)
    return pl.pallas_call(
        _nsa_prep_kernel,
        grid=(t // tm,),
        in_specs=[pl.BlockSpec((tm, width), row), pl.BlockSpec((4, HEAD_DIM), lambda i: (0, 0))],
        out_specs=[pl.BlockSpec((tm, NSA_WIDTH), row)] + [pl.BlockSpec((tm, 2 * NSA_KV_WIDTH), row)]
        + [pl.BlockSpec((tm, NSA_KV_WIDTH), row)] * 4,
        out_shape=[jax.ShapeDtypeStruct((t, NSA_WIDTH), BF16), jax.ShapeDtypeStruct((t, 2 * NSA_KV_WIDTH), F32)]
        + [jax.ShapeDtypeStruct((t, NSA_KV_WIDTH), BF16)] * 4,
        compiler_params=_cparams(("parallel",), 40),
        name="nsa_prep",
    )(p_nsa, qk_norm)


def _compress_kernel(x_ref, pos_ref, w1lo_ref, w1hi_ref, w2_ref, nw_ref, o_ref, lo_acc, hi_acc, *, n_chunk):
    kv = pl.program_id(0)
    l = pl.program_id(1)

    @pl.when(l == 0)
    def _():
        lo_acc[...] = jnp.zeros(lo_acc.shape, F32)
        hi_acc[...] = jnp.zeros(hi_acc.shape, F32)

    pos_lo = pos_ref[0, pl.ds(l, 1), :]
    pos_hi = pos_ref[0, pl.ds(l + CMP_STRIDE, 1), :]
    w_lo = w1lo_ref[0, 0]
    w_hi = w1hi_ref[0, 0]
    for g in range(NSA_KV_HEADS):
        x = x_ref[:, g * HEAD_DIM:(g + 1) * HEAD_DIM]
        lo_acc[g] += _dot((x + pos_lo).astype(BF16), w_lo)
        hi_acc[g] += _dot((x + pos_hi).astype(BF16), w_hi)

    @pl.when(l == CMP_STRIDE - 1)
    def _():
        for g in range(NSA_KV_HEADS):
            hidden = _silu(lo_acc[g] + pltpu.roll(hi_acc[g], shift=n_chunk - 1, axis=0))
            out = _dot(hidden.astype(BF16), w2_ref[0])
            normed = _head_rms(out, nw_ref[...])
            o_ref[0, g] = jnp.where(kv == 0, normed, out).astype(BF16)


def _compress(kv_c, cmp_pos, cmp_w1, cmp_w2, k_norm_w):
    t = kv_c.shape[0]
    n_chunk = t // CMP_STRIDE
    x = kv_c.reshape(n_chunk, CMP_STRIDE * 2 * NSA_KV_WIDTH)
    w1 = cmp_w1.astype(BF16)
    w2 = cmp_w2.astype(BF16)
    return pl.pallas_call(
        functools.partial(_compress_kernel, n_chunk=n_chunk),
        grid=(2, CMP_STRIDE),
        in_specs=[pl.BlockSpec((n_chunk, NSA_KV_WIDTH), lambda a, l: (0, 2 * l + a)),
                  pl.BlockSpec((1, CMP_BLOCK, HEAD_DIM), lambda a, l: (a, 0, 0)),
                  pl.BlockSpec((1, 1, HEAD_DIM, CMP_HIDDEN), lambda a, l: (a, l, 0, 0)),
                  pl.BlockSpec((1, 1, HEAD_DIM, CMP_HIDDEN), lambda a, l: (a, l + CMP_STRIDE, 0, 0)),
                  pl.BlockSpec((1, CMP_HIDDEN, HEAD_DIM), lambda a, l: (a, 0, 0)),
                  pl.BlockSpec((1, HEAD_DIM), lambda a, l: (0, 0))],
        out_specs=pl.BlockSpec((1, NSA_KV_HEADS, n_chunk, HEAD_DIM), lambda a, l: (a, 0, 0, 0)),
        out_shape=jax.ShapeDtypeStruct((2, NSA_KV_HEADS, n_chunk, HEAD_DIM), BF16),
        scratch_shapes=[pltpu.VMEM((NSA_KV_HEADS, n_chunk, CMP_HIDDEN), F32),
                        pltpu.VMEM((NSA_KV_HEADS, n_chunk, CMP_HIDDEN), F32)],
        compiler_params=_cparams(("arbitrary", "arbitrary"), 40),
        name="nsa_compress",
    )(x, cmp_pos, w1, w1, w2, k_norm_w.reshape(1, HEAD_DIM))


def _cmp_attn_kernel(slopes_ref, q_ref, k_ref, v_ref, cov_ref, o_ref, sel_ref, any_ref, *, tq, n_chunk):
    g = pl.program_id(0)
    qi = pl.program_id(1)
    k = k_ref[0, 0]
    v = v_ref[0, 0]
    t0 = qi * tq
    t_pos = t0 + lax.broadcasted_iota(jnp.int32, (tq, n_chunk), 0)
    k_end = lax.broadcasted_iota(jnp.int32, (tq, n_chunk), 1) * CMP_STRIDE + (CMP_BLOCK - 1)
    mask_bias = jnp.where(t_pos >= k_end, 0.0, MASK_BIAS)
    k_rel = (lax.broadcasted_iota(jnp.int32, (1, n_chunk), 1) * CMP_STRIDE + (CMP_BLOCK - 1) - t0).astype(F32)
    psum = jnp.zeros((tq, n_chunk), F32)
    for z in range(NSA_GROUP):
        slope = slopes_ref[g * NSA_GROUP + z]
        sl = slice(z * HEAD_DIM, (z + 1) * HEAD_DIM)
        s = _dot_nt(q_ref[:, sl], k) + (mask_bias + slope * k_rel)
        m = jnp.maximum(jnp.max(s, axis=-1, keepdims=True), NEG_INF)
        e = jnp.exp(s - m)
        p = e * (1.0 / jnp.maximum(jnp.sum(e, axis=-1, keepdims=True), 1e-30))
        o_ref[:, sl] = _dot(p.astype(BF16), v)
        psum = psum + p

    p_hi = psum.astype(BF16)
    p_lo = (psum - p_hi.astype(F32)).astype(BF16)
    cov = cov_ref[...]
    imp = _dot(p_hi, cov) + _dot(p_lo, cov)

    j = lax.broadcasted_iota(jnp.int32, (tq, N_BLK_LANES), 1)
    cur = (qi * tq + lax.broadcasted_iota(jnp.int32, (tq, N_BLK_LANES), 0)) // SEL_BLOCK
    valid = j <= cur
    forced = (j == 0) | (j == cur) | (j == cur - 1)
    val = jnp.where(valid, jnp.where(forced, FORCE_SCORE, imp), -1.0)
    sel = jnp.zeros((tq, N_BLK_LANES), F32)
    jf = j.astype(F32)
    for _ in range(SEL_TOPK):
        m = jnp.max(val, axis=-1, keepdims=True)
        idx = jnp.min(jnp.where(val == m, jf, float(N_BLK_LANES)), axis=-1, keepdims=True)
        hit = jf == idx
        sel = jnp.where(hit & (m > -0.5), 1.0, sel)
        val = jnp.where(hit, -2.0, val)
    sel_ref[0] = sel.astype(BF16)
    any_ref[0, 0] = jnp.broadcast_to(jnp.max(sel, axis=0, keepdims=True), (SUBLANES, N_BLK_LANES))


def _cmp_attention(q, kv_cmp, slopes, tq=256):
    t = q.shape[0]
    n_chunk = kv_cmp.shape[2]
    n_cmp = n_chunk - CMP_BLOCK // CMP_STRIDE + 1
    n_blk = t // SEL_BLOCK
    assert n_blk <= N_BLK_LANES
    tq = min(tq, t)
    cs = jnp.arange(n_chunk)[:, None] * CMP_STRIDE
    bs = jnp.arange(N_BLK_LANES)[None, :] * SEL_BLOCK
    cover = ((cs <= bs + SEL_BLOCK - 1) & (cs + CMP_BLOCK - 1 >= bs)
             & (jnp.arange(n_chunk)[:, None] < n_cmp) & (jnp.arange(N_BLK_LANES)[None, :] < n_blk))
    cover = cover.astype(BF16)
    return pl.pallas_call(
        functools.partial(_cmp_attn_kernel, tq=tq, n_chunk=n_chunk),
        grid=(NSA_KV_HEADS, t // tq),
        in_specs=[pl.BlockSpec(memory_space=pltpu.SMEM),
                  pl.BlockSpec((tq, NSA_GROUP * HEAD_DIM), lambda g, i: (i, g)),
                  pl.BlockSpec((1, 1, n_chunk, HEAD_DIM), lambda g, i: (0, g, 0, 0)),
                  pl.BlockSpec((1, 1, n_chunk, HEAD_DIM), lambda g, i: (1, g, 0, 0)),
                  pl.BlockSpec((n_chunk, N_BLK_LANES), lambda g, i: (0, 0))],
        out_specs=[pl.BlockSpec((tq, NSA_GROUP * HEAD_DIM), lambda g, i: (i, g)),
                   pl.BlockSpec((1, tq, N_BLK_LANES), lambda g, i: (g, i, 0)),
                   pl.BlockSpec((1, 1, SUBLANES, N_BLK_LANES), lambda g, i: (g, i, 0, 0))],
        out_shape=[jax.ShapeDtypeStruct((t, NSA_WIDTH), F32),
                   jax.ShapeDtypeStruct((NSA_KV_HEADS, t, N_BLK_LANES), BF16),
                   jax.ShapeDtypeStruct((NSA_KV_HEADS, t // tq, SUBLANES, N_BLK_LANES), F32)],
        compiler_params=_cparams(("parallel", "parallel"), 40),
        name="nsa_cmp_attn",
    )(slopes, q, kv_cmp, kv_cmp, cover)


_WL_FIRST, _WL_LAST, _WL_ACTIVE = 1, 2, 4


def _sel_attn_kernel(qi_ref, kj_ref, fl_ref, slopes_ref, q_ref, k_ref, v_ref, sel_ref, o_ref,
                     m_sc, l_sc, acc_sc, *, tq, tk, n_work):
    g = pl.program_id(0)
    idx = g * n_work + pl.program_id(1)
    qi = qi_ref[idx]
    kj = kj_ref[idx]
    flags = fl_ref[idx]

    @pl.when((flags & _WL_FIRST) != 0)
    def _():
        m_sc[...] = jnp.full(m_sc.shape, NEG_INF, F32)
        l_sc[...] = jnp.zeros(l_sc.shape, F32)
        acc_sc[...] = jnp.zeros(acc_sc.shape, F32)

    @pl.when((flags & _WL_ACTIVE) != 0)
    def _():
        t0 = qi * tq
        k0 = kj * tk
        t_pos = t0 + lax.broadcasted_iota(jnp.int32, (tq, tk), 0)
        k_pos = k0 + lax.broadcasted_iota(jnp.int32, (tq, tk), 1)
        blk_of_key = kj * (tk // SEL_BLOCK) + lax.broadcasted_iota(jnp.int32, (N_BLK_LANES, tk), 1) // SEL_BLOCK
        expand = (lax.broadcasted_iota(jnp.int32, (N_BLK_LANES, tk), 0) == blk_of_key).astype(BF16)
        chosen = _dot(sel_ref[0], expand)
        mask_bias = jnp.where((chosen > 0.5) & (t_pos >= k_pos), 0.0, MASK_BIAS)
        k_rel = (k0 - t0 + lax.broadcasted_iota(jnp.int32, (1, tk), 1)).astype(F32)
        k = k_ref[...]
        v = v_ref[...]
        qk_next = _dot_nt(q_ref[:, 0:HEAD_DIM], k)
        for z in range(NSA_GROUP):
            qk = qk_next
            if z + 1 < NSA_GROUP:
                qk_next = _dot_nt(q_ref[:, (z + 1) * HEAD_DIM:(z + 2) * HEAD_DIM], k)
            s = qk + (mask_bias + slopes_ref[g * NSA_GROUP + z] * k_rel)
            m_prev = m_sc[z]
            m_new = jnp.maximum(m_prev, jnp.max(s, axis=-1, keepdims=True))
            alpha = jnp.exp(m_prev - m_new)
            e = jnp.exp(s - m_new)
            l_sc[z] = alpha * l_sc[z] + jnp.sum(e, axis=-1, keepdims=True)
            acc_sc[z] = alpha * acc_sc[z] + _dot(e.astype(BF16), v)
            m_sc[z] = m_new

    @pl.when((flags & _WL_LAST) != 0)
    def _():
        for z in range(NSA_GROUP):
            o_ref[:, z * HEAD_DIM:(z + 1) * HEAD_DIM] = acc_sc[z] * (1.0 / jnp.maximum(l_sc[z], 1e-30))


def _sel_work_list(blk_any, t, tq, tk):
    nq, nk, bpt = t // tq, t // tk, tk // SEL_BLOCK
    n_blk = t // SEL_BLOCK
    picked = blk_any[:, :, 0, :n_blk].reshape(NSA_KV_HEADS, nq, nk, bpt).max(axis=-1) > 0.5
    qi = jnp.arange(nq)[:, None]
    kj = jnp.arange(nk)[None, :]
    causal = kj * tk <= qi * tq + tq - 1
    n_work = sum((i * tq + tq - 1) // tk + 1 for i in range(nq))
    act = (picked & causal[None]).reshape(NSA_KV_HEADS, nq * nk)
    order = jnp.argsort(jnp.logical_not(act), axis=1, stable=True)[:, :n_work].astype(jnp.int32)
    n_act = act.sum(axis=1).astype(jnp.int32)[:, None]
    w = jnp.arange(n_work, dtype=jnp.int32)[None, :]
    valid = w < n_act
    pos = jnp.take_along_axis(order, jnp.minimum(w, n_act - 1), axis=1)
    qi_l = pos // nk
    kj_l = pos % nk
    prev_q = jnp.concatenate([jnp.full((NSA_KV_HEADS, 1), -1, jnp.int32), qi_l[:, :-1]], axis=1)
    next_q = jnp.concatenate([qi_l[:, 1:], jnp.full((NSA_KV_HEADS, 1), -1, jnp.int32)], axis=1)
    first = valid & (qi_l != prev_q)
    last = valid & ((qi_l != next_q) | (w == n_act - 1))
    flags = (first * _WL_FIRST + last * _WL_LAST + valid * _WL_ACTIVE).astype(jnp.int32)
    return qi_l.reshape(-1), kj_l.reshape(-1), flags.reshape(-1), n_work


def _sel_attention(q, k_s, v_s, sel, blk_any, slopes, tq=256, tk=512):
    t = q.shape[0]
    tq = min(tq, t)
    tk = min(tk, t)
    qi_l, kj_l, flags, n_work = _sel_work_list(blk_any, t, tq, tk)

    def q_map(g, w, qi_ref, kj_ref, fl_ref):
        return (qi_ref[g * n_work + w], g)

    def kv_map(g, w, qi_ref, kj_ref, fl_ref):
        return (kj_ref[g * n_work + w], g)

    def sel_map(g, w, qi_ref, kj_ref, fl_ref):
        return (g, qi_ref[g * n_work + w], 0)

    grid_spec = pltpu.PrefetchScalarGridSpec(
        num_scalar_prefetch=3,
        grid=(NSA_KV_HEADS, n_work),
        in_specs=[pl.BlockSpec(memory_space=pltpu.SMEM),
                  pl.BlockSpec((tq, NSA_GROUP * HEAD_DIM), q_map),
                  pl.BlockSpec((tk, HEAD_DIM), kv_map),
                  pl.BlockSpec((tk, HEAD_DIM), kv_map),
                  pl.BlockSpec((1, tq, N_BLK_LANES), sel_map)],
        out_specs=pl.BlockSpec((tq, NSA_GROUP * HEAD_DIM), q_map),
        scratch_shapes=[pltpu.VMEM((NSA_GROUP, tq, 1), F32), pltpu.VMEM((NSA_GROUP, tq, 1), F32),
                        pltpu.VMEM((NSA_GROUP, tq, HEAD_DIM), F32)])
    return pl.pallas_call(
        functools.partial(_sel_attn_kernel, tq=tq, tk=tk, n_work=n_work),
        grid_spec=grid_spec,
        out_shape=jax.ShapeDtypeStruct((t, NSA_WIDTH), F32),
        compiler_params=_cparams(("arbitrary", "arbitrary"), 40),
        name="nsa_sel_attn",
    )(qi_l, kj_l, flags, slopes, q, k_s, v_s, sel)


def _win_attn_kernel(slopes_ref, q_ref, *refs, tq, n_tile):
    k_refs = refs[:n_tile]
    v_refs = refs[n_tile:2 * n_tile]
    o_ref = refs[2 * n_tile]
    g = pl.program_id(0)
    qi = pl.program_id(1)
    span = n_tile * tq
    k = jnp.concatenate([r[...] for r in k_refs], axis=0)
    v = jnp.concatenate([r[...] for r in v_refs], axis=0)
    t_pos = qi * tq + lax.broadcasted_iota(jnp.int32, (tq, span), 0)
    k_pos = (qi - (n_tile - 1)) * tq + lax.broadcasted_iota(jnp.int32, (tq, span), 1)
    dist = t_pos - k_pos
    mask_bias = jnp.where((dist >= 0) & (dist < WINDOW) & (k_pos >= 0), 0.0, MASK_BIAS)
    k_rel = (lax.broadcasted_iota(jnp.int32, (1, span), 1) - (n_tile - 1) * tq).astype(F32)
    for z in range(NSA_GROUP):
        slope = slopes_ref[g * NSA_GROUP + z]
        sl = slice(z * HEAD_DIM, (z + 1) * HEAD_DIM)
        s = _dot_nt(q_ref[:, sl], k) + (mask_bias + slope * k_rel)
        m = jnp.maximum(jnp.max(s, axis=-1, keepdims=True), NEG_INF)
        e = jnp.exp(s - m)
        p = e * (1.0 / jnp.maximum(jnp.sum(e, axis=-1, keepdims=True), 1e-30))
        o_ref[:, sl] = _dot(p.astype(BF16), v)


def _win_attention(q, k_w, v_w, slopes, tq=256):
    t = q.shape[0]
    tq = min(tq, t)
    n_tile = -(-WINDOW // tq) + 1

    def kv_spec(c):
        return pl.BlockSpec((tq, HEAD_DIM), lambda g, i: (jnp.maximum(i - (n_tile - 1) + c, 0), g))

    return pl.pallas_call(
        functools.partial(_win_attn_kernel, tq=tq, n_tile=n_tile),
        grid=(NSA_KV_HEADS, t // tq),
        in_specs=[pl.BlockSpec(memory_space=pltpu.SMEM),
                  pl.BlockSpec((tq, NSA_GROUP * HEAD_DIM), lambda g, i: (i, g))]
        + [kv_spec(c) for c in range(n_tile)] * 2,
        out_specs=pl.BlockSpec((tq, NSA_GROUP * HEAD_DIM), lambda g, i: (i, g)),
        out_shape=jax.ShapeDtypeStruct((t, NSA_WIDTH), F32),
        compiler_params=_cparams(("parallel", "parallel"), 40),
        name="nsa_win_attn",
    )(slopes, q, *([k_w] * n_tile), *([v_w] * n_tile))


def _nsa_combine_kernel(sm_ref, ex_ref, oc_ref, os_ref, ow_ref, z_ref, o_ref):
    logits = sm_ref[...]
    hi = logits.astype(BF16)
    lo = (logits - hi.astype(F32)).astype(BF16)
    acc = None
    for br, branch_ref in enumerate((oc_ref, os_ref, ow_ref)):
        ex = ex_ref[br]
        gate = _sigmoid(_dot(hi, ex) + _dot(lo, ex))
        term = gate * branch_ref[...]
        acc = term if acc is None else acc + term
    o_ref[...] = (acc * _silu(z_ref[...])).astype(o_ref.dtype)


def _nsa_combine(small, o_cmp, o_slc, o_win, z_a, tm=256):
    t = small.shape[0]
    tm = min(tm, t)
    lane = jnp.arange(LANES)[None, :, None]
    br = jnp.arange(N_BRANCH)[:, None, None]
    head = (jnp.arange(NSA_WIDTH) // HEAD_DIM)[None, None, :]
    expand = (lane == _SM_GATE + head * N_BRANCH + br).astype(BF16)
    row = lambda i: (i, 0)
    wide = pl.BlockSpec((tm, NSA_WIDTH), row)
    return pl.pallas_call(
        _nsa_combine_kernel,
        grid=(t // tm,),
        in_specs=[pl.BlockSpec((tm, LANES), row),
                  pl.BlockSpec((N_BRANCH, LANES, NSA_WIDTH), lambda i: (0, 0, 0)),
                  wide, wide, wide, wide],
        out_specs=wide,
        out_shape=jax.ShapeDtypeStruct((t, NSA_WIDTH), BF16),
        compiler_params=_cparams(("parallel",), 40),
        name="nsa_combine",
    )(small, expand, o_cmp, o_slc, o_win, z_a)


def _shift_rows(cur, halo, s):
    rolled = pltpu.roll(cur, shift=s, axis=0)
    halo_rolled = pltpu.roll(halo, shift=s, axis=0)
    row = lax.broadcasted_iota(jnp.int32, halo.shape, 0)
    head = jnp.where(row < s, halo_rolled, rolled[0:SUBLANES])
    return jnp.concatenate([head, rolled[SUBLANES:]], axis=0)


def _causal_conv(cur, halo, w_ref, k):
    acc = None
    for j in range(k):
        s = k - 1 - j
        term = (cur if s == 0 else _shift_rows(cur, halo, s)) * w_ref[j:j + 1, :]
        acc = term if acc is None else acc + term
    return acc


def _gdn_prep_kernel(p_ref, halo_ref, cw_ref, sm_ref, alog_ref, dtb_ref, q_ref, k_ref, v_ref, gb_ref):
    i = pl.program_id(0)
    cur = p_ref[...]
    halo = jnp.where(i > 0, halo_ref[...], 0.0)
    y = _silu(_causal_conv(cur, halo, cw_ref, GDN_CONV))
    for h in range(GDN_HEADS):
        sl = slice(h * GDN_DK, (h + 1) * GDN_DK)
        qh = y[:, sl]
        q_ref[:, sl] = qh * lax.rsqrt(jnp.sum(qh * qh, axis=-1, keepdims=True) + NORM_EPS) * (GDN_DK ** -0.5)
        kh = y[:, GDN_KEY_WIDTH + h * GDN_DK:GDN_KEY_WIDTH + (h + 1) * GDN_DK]
        k_ref[:, sl] = kh * lax.rsqrt(jnp.sum(kh * kh, axis=-1, keepdims=True) + NORM_EPS)
    v_ref[...] = y[:, 2 * GDN_KEY_WIDTH:]
    sm = sm_ref[...]
    lane = lax.broadcasted_iota(jnp.int32, sm.shape, 1)
    gdecay = -jnp.exp(alog_ref[...]) * jax.nn.softplus(sm + dtb_ref[...])
    beta = _sigmoid(sm)
    gb_ref[...] = jnp.where(lane < _SM_BETA, gdecay, jnp.where(lane < _SM_GATE, beta, 0.0))


def _gdn_prep(p_gdn, small, conv_w, a_log, dt_bias, tm=256):
    t, width = p_gdn.shape
    tm = min(tm, t)
    hb = tm // SUBLANES
    row = lambda i: (i, 0)
    alog = jnp.zeros((1, LANES), F32).at[0, _SM_A:_SM_A + GDN_HEADS].set(a_log)
    dtb = jnp.zeros((1, LANES), F32).at[0, _SM_A:_SM_A + GDN_HEADS].set(dt_bias)
    const = lambda i: (0, 0)
    return pl.pallas_call(
        _gdn_prep_kernel,
        grid=(t // tm,),
        in_specs=[pl.BlockSpec((tm, width), row),
                  pl.BlockSpec((SUBLANES, width), lambda i: (jnp.maximum(i * hb - 1, 0), 0)),
                  pl.BlockSpec((GDN_CONV, width), const),
                  pl.BlockSpec((tm, LANES), row),
                  pl.BlockSpec((1, LANES), const), pl.BlockSpec((1, LANES), const)],
        out_specs=[pl.BlockSpec((tm, GDN_KEY_WIDTH), row), pl.BlockSpec((tm, GDN_KEY_WIDTH), row),
                   pl.BlockSpec((tm, GDN_VAL_WIDTH), row), pl.BlockSpec((tm, LANES), row)],
        out_shape=[jax.ShapeDtypeStruct((t, GDN_KEY_WIDTH), F32), jax.ShapeDtypeStruct((t, GDN_KEY_WIDTH), F32),
                   jax.ShapeDtypeStruct((t, GDN_VAL_WIDTH), F32), jax.ShapeDtypeStruct((t, LANES), F32)],
        compiler_params=_cparams(("parallel",), 48),
        name="gdn_prep",
    )(p_gdn, p_gdn, conv_w, small, alog, dtb)


def _row_pad(x):
    return jnp.concatenate([x, jnp.zeros_like(x)], axis=0)


def _gdn_intra_kernel(q_ref, k_ref, v_ref, gb_ref, u_ref, w_ref, qe_ref, ke_ref, at_ref, eg_ref, *, heads):
    c = GDN_CHUNK
    gb = gb_ref[...]
    row = lax.broadcasted_iota(jnp.int32, gb.shape, 0)
    gcum = gb
    shift = 1
    while shift < c:
        gcum = gcum + jnp.where(row >= shift, pltpu.roll(gcum, shift=shift, axis=0), 0.0)
        shift *= 2
    lane = lax.broadcasted_iota(jnp.int32, gb.shape, 1)
    ri = lax.broadcasted_iota(jnp.int32, (c, LANES), 0)
    ci = lax.broadcasted_iota(jnp.int32, (c, LANES), 1)
    tri = ri >= ci
    strict = ri > ci
    eye = ri == ci
    eyef = eye.astype(F32)
    lane8 = lax.broadcasted_iota(jnp.int32, (SUBLANES, LANES), 1)
    eg_all = jnp.zeros((SUBLANES, LANES), F32)
    hs = range(heads)
    sls = [slice(h * GDN_DK, (h + 1) * GDN_DK) for h in hs]

    gcol = [jnp.sum(jnp.where(lane == _SM_A + h, gcum, 0.0), axis=-1, keepdims=True) for h in hs]
    beta = [jnp.sum(jnp.where(lane == _SM_BETA + h, gb, 0.0), axis=-1, keepdims=True) for h in hs]
    decay = []
    for h in hs:
        gmat = jnp.broadcast_to(gcol[h], (c, LANES))
        grow = jnp.sum(jnp.where(eye, gmat, 0.0), axis=0, keepdims=True)
        decay.append(jnp.where(tri, jnp.exp(jnp.where(tri, gmat - grow, 0.0)), 0.0))
    glast = [gcol[h][c - 1:c, :] for h in hs]
    egc = [jnp.exp(gcol[h]) for h in hs]
    q = [q_ref[:, sls[h]] for h in hs]
    k = [k_ref[:, sls[h]] for h in hs]
    k16 = [k[h].astype(BF16) for h in hs]
    qk = [_dot_nt(jnp.concatenate([q[h].astype(BF16), k16[h]], axis=0), _row_pad(k16[h])) for h in hs]
    for h in hs:
        at_ref[:, sls[h]] = (qk[h][:c] * decay[h]).astype(BF16)

    pw = [-jnp.where(strict, beta[h] * qk[h][c:] * decay[h], 0.0) for h in hs]
    inv = [eyef + pw[h] for h in hs]
    pw16 = [pw[h].astype(BF16) for h in hs]
    pw = [_dot(pw16[h], _row_pad(pw16[h])) for h in hs]
    span = 2
    while span < c:
        pw16 = [pw[h].astype(BF16) for h in hs]
        if 2 * span < c:
            both = [_dot(jnp.concatenate([pw16[h], inv[h].astype(BF16)], axis=0), _row_pad(pw16[h])) for h in hs]
            pw = [both[h][:c] for h in hs]
            inv = [inv[h] + both[h][c:] for h in hs]
        else:
            inv = [inv[h] + _dot(inv[h].astype(BF16), _row_pad(pw16[h])) for h in hs]
        span *= 2

    uw = []
    for h in hs:
        rhs = jnp.concatenate([(v_ref[:, sls[h]] * beta[h]).astype(BF16),
                               (k[h] * (beta[h] * egc[h])).astype(BF16)], axis=1)
        uw.append(_dot(inv[h].astype(BF16), _row_pad(rhs)))
    for h in hs:
        u_ref[:, sls[h]] = uw[h][:, :GDN_DV]
        w_ref[:, sls[h]] = uw[h][:, GDN_DV:].astype(BF16)
        qe_ref[:, sls[h]] = (q[h] * egc[h]).astype(BF16)
        ke_ref[:, sls[h]] = (k[h] * jnp.exp(glast[h] - gcol[h])).astype(BF16)
        eg_all = jnp.where(lane8 == h, jnp.exp(glast[h]), eg_all)
    eg_ref[0] = eg_all


def _gdn_scan_kernel(eg_ref, u_ref, w_ref, qe_ref, ke_ref, at_ref, z_ref, nw_ref, o_ref, state, *, heads):
    n = pl.program_id(0)
    c = GDN_CHUNK

    @pl.when(n == 0)
    def _():
        state[...] = jnp.zeros(state.shape, F32)

    nw = nw_ref[...]
    hs = range(heads)
    sls = [slice(h * GDN_DK, (h + 1) * GDN_DK) for h in hs]
    s_prev = [state[h] for h in hs]
    ws_qs = [_dot(jnp.concatenate([w_ref[:, sls[h]], qe_ref[:, sls[h]]], axis=0), s_prev[h].astype(BF16))
             for h in hs]
    v_new16 = [(u_ref[:, sls[h]] - ws_qs[h][:c]).astype(BF16) for h in hs]
    o = [ws_qs[h][c:] + _dot(at_ref[:, h * GDN_DK:h * GDN_DK + c], v_new16[h]) for h in hs]
    for h in hs:
        state[h] = s_prev[h] * eg_ref[n * heads + h] + _dot_tn(ke_ref[:, sls[h]], v_new16[h])
    for h in hs:
        o_ref[:, sls[h]] = (_head_rms(o[h], nw) * _silu(z_ref[:, sls[h]])).astype(o_ref.dtype)


def _gdn_chunks(q, k, v, gb, z_b, norm_w):
    t = q.shape[0]
    c = GDN_CHUNK
    n_chunk = t // c
    heads = GDN_HEADS
    wide = pl.BlockSpec((c, GDN_VAL_WIDTH), lambda n: (n, 0))
    u, w, qe, ke, attn, eg = pl.pallas_call(
        functools.partial(_gdn_intra_kernel, heads=heads),
        grid=(n_chunk,),
        in_specs=[wide, wide, wide, pl.BlockSpec((c, LANES), lambda n: (n, 0))],
        out_specs=[wide, wide, wide, wide, wide, pl.BlockSpec((1, SUBLANES, LANES), lambda n: (n, 0, 0))],
        out_shape=[jax.ShapeDtypeStruct((t, GDN_VAL_WIDTH), F32)]
        + [jax.ShapeDtypeStruct((t, GDN_VAL_WIDTH), BF16)] * 4
        + [jax.ShapeDtypeStruct((n_chunk, SUBLANES, LANES), F32)],
        compiler_params=_cparams(("parallel",), 40),
        name="gdn_intra",
    )(q, k, v, gb)
    eg_flat = eg[:, 0, :heads].reshape(n_chunk * heads)
    return pl.pallas_call(
        functools.partial(_gdn_scan_kernel, heads=heads),
        grid=(n_chunk,),
        in_specs=[pl.BlockSpec(memory_space=pltpu.SMEM), wide, wide, wide, wide, wide, wide,
                  pl.BlockSpec((1, GDN_DV), lambda n: (0, 0))],
        out_specs=wide,
        out_shape=jax.ShapeDtypeStruct((t, GDN_VAL_WIDTH), BF16),
        scratch_shapes=[pltpu.VMEM((heads, GDN_DK, GDN_DV), F32)],
        compiler_params=_cparams(("arbitrary",), 40),
        name="gdn_scan",
    )(eg_flat, u, w, qe, ke, attn, z_b, norm_w.reshape(1, GDN_DV))


def _short_conv_kernel(bg_ref, cg_ref, x_ref, z_ref, cgh_ref, xh_ref, cw_ref, o_ref):
    i = pl.program_id(0)
    cur = cg_ref[...] * x_ref[...]
    halo = jnp.where(i > 0, cgh_ref[...] * xh_ref[...], 0.0)
    y = _causal_conv(cur, halo, cw_ref, SC_CONV)
    o_ref[...] = (bg_ref[...] * y * _silu(z_ref[...])).astype(o_ref.dtype)


def _short_conv(p_sc, conv_w, tm=256):
    t = p_sc.shape[0]
    tm = min(tm, t)
    hb = tm // SUBLANES
    main = lambda c: pl.BlockSpec((tm, SC_WIDTH), functools.partial(lambda i, c: (i, c), c=c))
    halo = lambda c: pl.BlockSpec((SUBLANES, SC_WIDTH),
                                  functools.partial(lambda i, c: (jnp.maximum(i * hb - 1, 0), c), c=c))
    return pl.pallas_call(
        _short_conv_kernel,
        grid=(t // tm,),
        in_specs=[main(0), main(1), main(2), main(3), halo(1), halo(2),
                  pl.BlockSpec((SC_CONV, SC_WIDTH), lambda i: (0, 0))],
        out_specs=pl.BlockSpec((tm, SC_WIDTH), lambda i: (i, 0)),
        out_shape=jax.ShapeDtypeStruct((t, SC_WIDTH), BF16),
        compiler_params=_cparams(("parallel",), 40),
        name="short_conv",
    )(p_sc, p_sc, p_sc, p_sc, p_sc, p_sc, conv_w)


def _layer(x, norm_w, w_in, nsa_qk_norm, cmp_pos, cmp_w1, cmp_w2, gdn_conv_w, gdn_a_log, gdn_dt_bias,
           gdn_norm_w, sc_conv_w, w_branch_nsa, w_branch_gdn, w_branch_sc, w_out, slopes):
    d = x.shape[1]
    cast = lambda w: w.astype(BF16)
    groups = [("nsa", _OFF_QA, _OFF_GATE), ("za", _OFF_ZA, _OFF_QB), ("gdn", _OFF_QB, _OFF_AB),
              ("zb", _OFF_ZB, _OFF_SC), ("sc", _OFF_SC, _OFF_MERGE), ("merge", _OFF_MERGE, w_in.shape[1])]
    small_pad = LANES - 2 * GDN_HEADS - 3 * NSA_HEADS
    w_st = jnp.concatenate(
        [cast(w_in[:, a:b]) for _, a, b in groups]
        + [cast(w_in[:, _OFF_AB:_OFF_ZB]), cast(w_in[:, _OFF_GATE:_OFF_ZA]), jnp.zeros((d, small_pad), BF16)],
        axis=1)

    hn = _rmsnorm(x, norm_w)
    proj = {}
    off = 0
    for name, a, b in groups:
        proj[name] = _matmul(hn, w_st, off, b - a, F32, 1024, 512, "in_proj_" + name)
        off += b - a
    small = _matmul(hn, w_st, off, LANES, F32, 1024, LANES, "in_proj_small")
    p_nsa, z_a, p_gdn, z_b, p_sc, p_mg = (proj[n] for n, _, _ in groups)

    q, kv_c, k_s, v_s, k_w, v_w = _nsa_prep(p_nsa, nsa_qk_norm)
    kv_cmp = _compress(kv_c, cmp_pos, cmp_w1, cmp_w2, nsa_qk_norm[1])
    o_cmp, sel, blk_any = _cmp_attention(q, kv_cmp, slopes)
    o_slc = _sel_attention(q, k_s, v_s, sel, blk_any, slopes)
    o_win = _win_attention(q, k_w, v_w, slopes)
    o_a = _nsa_combine(small, o_cmp, o_slc, o_win, z_a)

    q_b, k_b, v_b, gb = _gdn_prep(p_gdn, small, gdn_conv_w, gdn_a_log, gdn_dt_bias)
    o_b = _gdn_chunks(q_b, k_b, v_b, gb, z_b, gdn_norm_w)

    o_c = _short_conv(p_sc, sc_conv_w)

    merged = _merge(o_a, o_b, o_c, cast(w_branch_nsa), cast(w_branch_gdn), cast(w_branch_sc), p_mg)
    return _matmul_residual(merged, cast(w_out), x, 1024, 512)


def kernel(x, norm_w, w_in, nsa_qk_norm, cmp_pos, cmp_w1, cmp_w2, gdn_conv_w, gdn_a_log, gdn_dt_bias,
           gdn_norm_w, sc_conv_w, w_branch_nsa, w_branch_gdn, w_branch_sc, w_out):
    b, t, d = x.shape
    depth = norm_w.shape[0]
    heads = jnp.arange(1, NSA_HEADS + 1, dtype=F32)
    slopes = jnp.exp2(-8.0 * heads / NSA_HEADS)
    outs = []
    for bi in range(b):
        xb = x[bi]
        for l in range(depth):
            xb = _layer(xb, norm_w[l], w_in[l], nsa_qk_norm[l], cmp_pos[l], cmp_w1[l], cmp_w2[l],
                        gdn_conv_w[l], gdn_a_log[l], gdn_dt_bias[l], gdn_norm_w[l], sc_conv_w[l],
                        w_branch_nsa[l], w_branch_gdn[l], w_branch_sc[l], w_out[l], slopes)
        outs.append(xb)
    return jnp.stack(outs, axis=0)
```

```python
import functools
import math

import jax
import jax.numpy as jnp
from jax import lax
from jax.experimental import pallas as pl
from jax.experimental.pallas import tpu as pltpu

F32 = jnp.float32
BF16 = jnp.bfloat16

HEAD_DIM = 128
NSA_HEADS = 16
NSA_KV_HEADS = 4
NSA_GROUP = NSA_HEADS // NSA_KV_HEADS
NSA_WIDTH = NSA_HEADS * HEAD_DIM
NSA_KV_WIDTH = NSA_KV_HEADS * HEAD_DIM
CMP_BLOCK = 32
CMP_STRIDE = 16
CMP_HIDDEN = 256
SEL_BLOCK = 64
SEL_TOPK = 16
WINDOW = 512
GDN_HEADS = 16
GDN_DK = 128
GDN_DV = 128
GDN_KEY_WIDTH = GDN_HEADS * GDN_DK
GDN_VAL_WIDTH = GDN_HEADS * GDN_DV
GDN_CONV = 4
GDN_CHUNK = 64
SC_WIDTH = 2048
SC_CONV = 3
N_BRANCH = 3
NORM_EPS = 1e-6
NEG_INF = -1e30
MASK_BIAS = -2e30
FORCE_SCORE = 1e6

LANES = 128
SUBLANES = 8
N_BLK_LANES = 128
MIB = 1024 * 1024

_OFF_QA = 0
_OFF_KVC = _OFF_QA + NSA_WIDTH
_OFF_KVS = _OFF_KVC + 2 * NSA_KV_WIDTH
_OFF_KVW = _OFF_KVS + 2 * NSA_KV_WIDTH
_OFF_GATE = _OFF_KVW + 2 * NSA_KV_WIDTH
_OFF_ZA = _OFF_GATE + 3 * NSA_HEADS
_OFF_QB = _OFF_ZA + NSA_WIDTH
_OFF_AB = _OFF_QB + 2 * GDN_KEY_WIDTH + GDN_VAL_WIDTH
_OFF_BETA = _OFF_AB + GDN_HEADS
_OFF_ZB = _OFF_BETA + GDN_HEADS
_OFF_SC = _OFF_ZB + GDN_VAL_WIDTH
_OFF_MERGE = _OFF_SC + 4 * SC_WIDTH

_SM_A = 0
_SM_BETA = GDN_HEADS
_SM_GATE = 2 * GDN_HEADS


def _cparams(sem, vmem_mib):
    return pltpu.CompilerParams(dimension_semantics=sem, vmem_limit_bytes=vmem_mib * MIB)


def _sigmoid(x):
    return jax.nn.sigmoid(x)


def _silu(x):
    return x * jax.nn.sigmoid(x)


def _dot(a, b):
    return jnp.dot(a, b, preferred_element_type=F32)


def _dot_nt(a, b):
    return lax.dot_general(a, b, (((1,), (1,)), ((), ())), preferred_element_type=F32)


def _dot_tn(a, b):
    return lax.dot_general(a, b, (((0,), (0,)), ((), ())), preferred_element_type=F32)


def _head_rms(x, w):
    return x * lax.rsqrt(jnp.mean(x * x, axis=-1, keepdims=True) + NORM_EPS) * w


def _rmsnorm_kernel(x_ref, w_ref, o_ref):
    x = x_ref[...]
    y = x * lax.rsqrt(jnp.mean(x * x, axis=-1, keepdims=True) + NORM_EPS)
    o_ref[...] = (y * w_ref[...]).astype(o_ref.dtype)


def _rmsnorm(x, w, tm=256):
    t, d = x.shape
    return pl.pallas_call(
        _rmsnorm_kernel,
        grid=(t // tm,),
        in_specs=[pl.BlockSpec((tm, d), lambda i: (i, 0)), pl.BlockSpec((1, d), lambda i: (0, 0))],
        out_specs=pl.BlockSpec((tm, d), lambda i: (i, 0)),
        out_shape=jax.ShapeDtypeStruct((t, d), BF16),
        compiler_params=_cparams(("parallel",), 32),
        name="rmsnorm",
    )(x, w.reshape(1, d))


def _mm_kernel(a_ref, b_ref, o_ref):
    o_ref[...] = _dot(a_ref[...], b_ref[...]).astype(o_ref.dtype)


def _matmul(a, b, layer, col_off, n, out_dtype, tm, tn, name):
    m, k = a.shape
    tm = min(tm, m)
    tn = min(tn, n)
    assert col_off % tn == 0 and n % tn == 0
    off = col_off // tn
    return pl.pallas_call(
        _mm_kernel,
        grid=(m // tm, n // tn),
        in_specs=[pl.BlockSpec((tm, k), lambda i, j: (i, 0)),
                  pl.BlockSpec((None, k, tn), lambda i, j: (layer, 0, off + j))],
        out_specs=pl.BlockSpec((tm, tn), lambda i, j: (i, j)),
        out_shape=jax.ShapeDtypeStruct((m, n), out_dtype),
        compiler_params=_cparams(("parallel", "arbitrary"), 48),
        name=name,
    )(a, b)


def _mm_res_kernel(a_ref, b_ref, r_ref, o_ref):
    o_ref[...] = r_ref[...] + _dot(a_ref[...], b_ref[...])


def _matmul_residual(a, b, layer, r, tm, tn):
    m, k = a.shape
    n = b.shape[2]
    tm = min(tm, m)
    return pl.pallas_call(
        _mm_res_kernel,
        grid=(m // tm, n // tn),
        in_specs=[pl.BlockSpec((tm, k), lambda i, j: (i, 0)),
                  pl.BlockSpec((None, k, tn), lambda i, j: (layer, 0, j)),
                  pl.BlockSpec((tm, tn), lambda i, j: (i, j))],
        out_specs=pl.BlockSpec((tm, tn), lambda i, j: (i, j)),
        out_shape=jax.ShapeDtypeStruct((m, n), F32),
        compiler_params=_cparams(("parallel", "arbitrary"), 48),
        name="out_proj_residual",
    )(a, b, r)


def _merge_kernel(oa_ref, ob_ref, oc_ref, wa_ref, wb_ref, wc_ref, ga_ref, gb_ref, gc_ref, o_ref):
    acc = _sigmoid(ga_ref[...]) * _dot(oa_ref[...], wa_ref[...])
    acc = acc + _sigmoid(gb_ref[...]) * _dot(ob_ref[...], wb_ref[...])
    acc = acc + _sigmoid(gc_ref[...]) * _dot(oc_ref[...], wc_ref[...])
    o_ref[...] = acc.astype(o_ref.dtype)


def _merge(o_a, o_b, o_c, wa, wb, wc, layer, gates, tm=512, tn=512):
    m, k = o_a.shape
    n = wa.shape[2]
    tm = min(tm, m)
    nb = n // tn
    a_spec = pl.BlockSpec((tm, k), lambda i, j: (i, 0))
    w_spec = pl.BlockSpec((None, k, tn), lambda i, j: (layer, 0, j))
    g_specs = [pl.BlockSpec((tm, tn), functools.partial(lambda i, j, br: (i, br * nb + j), br=br))
               for br in range(N_BRANCH)]
    return pl.pallas_call(
        _merge_kernel,
        grid=(m // tm, nb),
        in_specs=[a_spec, a_spec, a_spec, w_spec, w_spec, w_spec] + g_specs,
        out_specs=pl.BlockSpec((tm, tn), lambda i, j: (i, j)),
        out_shape=jax.ShapeDtypeStruct((m, n), BF16),
        compiler_params=_cparams(("parallel", "arbitrary"), 48),
        name="branch_merge",
    )(o_a, o_b, o_c, wa, wb, wc, gates, gates, gates)


def _nsa_prep_kernel(p_ref, nw_ref, q_ref, kvc_ref, ks_ref, vs_ref, kw_ref, vw_ref):
    nw = nw_ref[...]
    for h in range(NSA_HEADS):
        sl = slice(h * HEAD_DIM, (h + 1) * HEAD_DIM)
        q_ref[:, sl] = (_head_rms(p_ref[:, sl], nw[0:1]) * (HEAD_DIM ** -0.5)).astype(BF16)
    kvc_ref[...] = p_ref[:, _OFF_KVC:_OFF_KVS]
    for g in range(NSA_KV_HEADS):
        sl = slice(g * HEAD_DIM, (g + 1) * HEAD_DIM)
        ks = p_ref[:, _OFF_KVS + g * HEAD_DIM:_OFF_KVS + (g + 1) * HEAD_DIM]
        ks_ref[:, sl] = _head_rms(ks, nw[2:3]).astype(BF16)
        vs_ref[:, sl] = p_ref[:, _OFF_KVS + NSA_KV_WIDTH + g * HEAD_DIM:
                              _OFF_KVS + NSA_KV_WIDTH + (g + 1) * HEAD_DIM].astype(BF16)
        kw = p_ref[:, _OFF_KVW + g * HEAD_DIM:_OFF_KVW + (g + 1) * HEAD_DIM]
        kw_ref[:, sl] = _head_rms(kw, nw[3:4]).astype(BF16)
        vw_ref[:, sl] = p_ref[:, _OFF_KVW + NSA_KV_WIDTH + g * HEAD_DIM:
                              _OFF_KVW + NSA_KV_WIDTH + (g + 1) * HEAD_DIM].astype(BF16)


def _nsa_prep(p_nsa, qk_norm, tm=256):
    t, width = p_nsa.shape
    row = lambda i: (i, 0)
    return pl.pallas_call(
        _nsa_prep_kernel,
        grid=(t // tm,),
        in_specs=[pl.BlockSpec((tm, width), row), pl.BlockSpec((4, HEAD_DIM), lambda i: (0, 0))],
        out_specs=[pl.BlockSpec((tm, NSA_WIDTH), row)] + [pl.BlockSpec((tm, 2 * NSA_KV_WIDTH), row)]
        + [pl.BlockSpec((tm, NSA_KV_WIDTH), row)] * 4,
        out_shape=[jax.ShapeDtypeStruct((t, NSA_WIDTH), BF16), jax.ShapeDtypeStruct((t, 2 * NSA_KV_WIDTH), F32)]
        + [jax.ShapeDtypeStruct((t, NSA_KV_WIDTH), BF16)] * 4,
        compiler_params=_cparams(("parallel",), 40),
        name="nsa_prep",
    )(p_nsa, qk_norm)


def _compress_kernel(x_ref, pos_ref, w1lo_ref, w1hi_ref, w2_ref, nw_ref, o_ref, lo_acc, hi_acc, *, n_chunk):
    kv = pl.program_id(0)
    l = pl.program_id(1)

    @pl.when(l == 0)
    def _():
        lo_acc[...] = jnp.zeros(lo_acc.shape, F32)
        hi_acc[...] = jnp.zeros(hi_acc.shape, F32)

    pos_lo = pos_ref[0, pl.ds(l, 1), :]
    pos_hi = pos_ref[0, pl.ds(l + CMP_STRIDE, 1), :]
    w_lo = w1lo_ref[0, 0]
    w_hi = w1hi_ref[0, 0]
    for g in range(NSA_KV_HEADS):
        x = x_ref[:, g * HEAD_DIM:(g + 1) * HEAD_DIM]
        lo_acc[g] += _dot((x + pos_lo).astype(BF16), w_lo)
        hi_acc[g] += _dot((x + pos_hi).astype(BF16), w_hi)

    @pl.when(l == CMP_STRIDE - 1)
    def _():
        for g in range(NSA_KV_HEADS):
            hidden = _silu(lo_acc[g] + pltpu.roll(hi_acc[g], shift=n_chunk - 1, axis=0))
            out = _dot(hidden.astype(BF16), w2_ref[0])
            normed = _head_rms(out, nw_ref[...])
            o_ref[0, g] = jnp.where(kv == 0, normed, out).astype(BF16)


def _compress(kv_c, cmp_pos, cmp_w1, cmp_w2, k_norm_w):
    t = kv_c.shape[0]
    n_chunk = t // CMP_STRIDE
    x = kv_c.reshape(n_chunk, CMP_STRIDE * 2 * NSA_KV_WIDTH)
    w1 = cmp_w1.astype(BF16)
    w2 = cmp_w2.astype(BF16)
    return pl.pallas_call(
        functools.partial(_compress_kernel, n_chunk=n_chunk),
        grid=(2, CMP_STRIDE),
        in_specs=[pl.BlockSpec((n_chunk, NSA_KV_WIDTH), lambda a, l: (0, 2 * l + a)),
                  pl.BlockSpec((1, CMP_BLOCK, HEAD_DIM), lambda a, l: (a, 0, 0)),
                  pl.BlockSpec((1, 1, HEAD_DIM, CMP_HIDDEN), lambda a, l: (a, l, 0, 0)),
                  pl.BlockSpec((1, 1, HEAD_DIM, CMP_HIDDEN), lambda a, l: (a, l + CMP_STRIDE, 0, 0)),
                  pl.BlockSpec((1, CMP_HIDDEN, HEAD_DIM), lambda a, l: (a, 0, 0)),
                  pl.BlockSpec((1, HEAD_DIM), lambda a, l: (0, 0))],
        out_specs=pl.BlockSpec((1, NSA_KV_HEADS, n_chunk, HEAD_DIM), lambda a, l: (a, 0, 0, 0)),
        out_shape=jax.ShapeDtypeStruct((2, NSA_KV_HEADS, n_chunk, HEAD_DIM), BF16),
        scratch_shapes=[pltpu.VMEM((NSA_KV_HEADS, n_chunk, CMP_HIDDEN), F32),
                        pltpu.VMEM((NSA_KV_HEADS, n_chunk, CMP_HIDDEN), F32)],
        compiler_params=_cparams(("arbitrary", "arbitrary"), 40),
        name="nsa_compress",
    )(x, cmp_pos, w1, w1, w2, k_norm_w.reshape(1, HEAD_DIM))


def _cmp_attn_kernel(slopes_ref, q_ref, k_ref, v_ref, cov_ref, o_ref, sel_ref, any_ref, *, tq, n_chunk):
    g = pl.program_id(0)
    qi = pl.program_id(1)
    k = k_ref[0, 0]
    v_t = v_ref[0, 0].T
    t0 = qi * tq
    t_pos = t0 + lax.broadcasted_iota(jnp.int32, (n_chunk, tq), 1)
    k_end = lax.broadcasted_iota(jnp.int32, (n_chunk, tq), 0) * CMP_STRIDE + (CMP_BLOCK - 1)
    mask_bias = jnp.where(t_pos >= k_end, 0.0, MASK_BIAS)
    k_rel = (k_end - t0).astype(F32)
    psum = jnp.zeros((n_chunk, tq), F32)
    for z in range(NSA_GROUP):
        slope = slopes_ref[g * NSA_GROUP + z]
        sl = slice(z * HEAD_DIM, (z + 1) * HEAD_DIM)
        s = _dot_nt(k, q_ref[:, sl]) + (mask_bias + slope * k_rel)
        m = jnp.maximum(jnp.max(s, axis=0, keepdims=True), NEG_INF)
        e = jnp.exp(s - m)
        p = e * (1.0 / jnp.maximum(jnp.sum(e, axis=0, keepdims=True), 1e-30))
        o_ref[:, sl] = _dot(v_t, p.astype(BF16)).T
        psum = psum + p

    p_hi = psum.astype(BF16)
    p_lo = (psum - p_hi.astype(F32)).astype(BF16)
    cov_t = cov_ref[...]
    imp = _dot(cov_t, p_hi) + _dot(cov_t, p_lo)

    j = lax.broadcasted_iota(jnp.int32, (N_BLK_LANES, tq), 0)
    cur = (t0 + lax.broadcasted_iota(jnp.int32, (N_BLK_LANES, tq), 1)) // SEL_BLOCK
    valid = j <= cur
    forced = (j == 0) | (j == cur) | (j == cur - 1)
    val = jnp.where(valid, jnp.where(forced, FORCE_SCORE, imp), -1.0)
    sel = jnp.zeros((N_BLK_LANES, tq), F32)
    jf = j.astype(F32)
    for _ in range(SEL_TOPK):
        m = jnp.max(val, axis=0, keepdims=True)
        idx = jnp.min(jnp.where(val == m, jf, float(N_BLK_LANES)), axis=0, keepdims=True)
        hit = jf == idx
        sel = jnp.where(hit & (m > -0.5), 1.0, sel)
        val = jnp.where(hit, -2.0, val)
    sel_q = sel.T
    sel_ref[0] = sel_q.astype(BF16)
    any_ref[0, 0] = jnp.broadcast_to(jnp.max(sel_q, axis=0, keepdims=True), (SUBLANES, N_BLK_LANES))


def _cmp_attention(q, kv_cmp, slopes, tq=256):
    t = q.shape[0]
    n_chunk = kv_cmp.shape[2]
    n_cmp = n_chunk - CMP_BLOCK // CMP_STRIDE + 1
    n_blk = t // SEL_BLOCK
    assert n_blk <= N_BLK_LANES
    tq = min(tq, t)
    cs = jnp.arange(n_chunk)[:, None] * CMP_STRIDE
    bs = jnp.arange(N_BLK_LANES)[None, :] * SEL_BLOCK
    cover = ((cs <= bs + SEL_BLOCK - 1) & (cs + CMP_BLOCK - 1 >= bs)
             & (jnp.arange(n_chunk)[:, None] < n_cmp) & (jnp.arange(N_BLK_LANES)[None, :] < n_blk))
    cover = cover.astype(BF16).T
    return pl.pallas_call(
        functools.partial(_cmp_attn_kernel, tq=tq, n_chunk=n_chunk),
        grid=(NSA_KV_HEADS, t // tq),
        in_specs=[pl.BlockSpec(memory_space=pltpu.SMEM),
                  pl.BlockSpec((tq, NSA_GROUP * HEAD_DIM), lambda g, i: (i, g)),
                  pl.BlockSpec((1, 1, n_chunk, HEAD_DIM), lambda g, i: (0, g, 0, 0)),
                  pl.BlockSpec((1, 1, n_chunk, HEAD_DIM), lambda g, i: (1, g, 0, 0)),
                  pl.BlockSpec((N_BLK_LANES, n_chunk), lambda g, i: (0, 0))],
        out_specs=[pl.BlockSpec((tq, NSA_GROUP * HEAD_DIM), lambda g, i: (i, g)),
                   pl.BlockSpec((1, tq, N_BLK_LANES), lambda g, i: (g, i, 0)),
                   pl.BlockSpec((1, 1, SUBLANES, N_BLK_LANES), lambda g, i: (g, i, 0, 0))],
        out_shape=[jax.ShapeDtypeStruct((t, NSA_WIDTH), F32),
                   jax.ShapeDtypeStruct((NSA_KV_HEADS, t, N_BLK_LANES), BF16),
                   jax.ShapeDtypeStruct((NSA_KV_HEADS, t // tq, SUBLANES, N_BLK_LANES), F32)],
        compiler_params=_cparams(("parallel", "parallel"), 40),
        name="nsa_cmp_attn",
    )(slopes, q, kv_cmp, kv_cmp, cover)


_WL_FIRST, _WL_LAST, _WL_ACTIVE = 1, 2, 4


def _sel_attn_kernel(qi_ref, kj_ref, fl_ref, slopes_ref, q_ref, k_ref, v_ref, eb_ref, sel_ref, o_ref,
                     qa_sc, m_sc, l_sc, acc_sc, *, tq, tk, n_work):
    g = pl.program_id(0)
    idx = g * n_work + pl.program_id(1)
    qi = qi_ref[idx]
    kj = kj_ref[idx]
    flags = fl_ref[idx]

    @pl.when((flags & _WL_FIRST) != 0)
    def _():
        m_sc[...] = jnp.full(m_sc.shape, NEG_INF, F32)
        l_sc[...] = jnp.zeros(l_sc.shape, F32)
        acc_sc[...] = jnp.zeros(acc_sc.shape, F32)
        unselected = sel_ref[0] - 1.0
        for z in range(NSA_GROUP):
            qa_sc[z] = jnp.concatenate([q_ref[:, z * HEAD_DIM:(z + 1) * HEAD_DIM], unselected], axis=1)

    t0 = qi * tq
    k0 = kj * tk
    active = (flags & _WL_ACTIVE) != 0
    on_diagonal = k0 + tk > t0

    def step(causal):
        k_aug = jnp.concatenate([k_ref[...], eb_ref[0]], axis=1)
        v_t = v_ref[...].T
        row = lax.broadcasted_iota(jnp.int32, (tk, tq), 0)
        k_rel = (k0 - t0 + row).astype(F32)
        if causal:
            col = lax.broadcasted_iota(jnp.int32, (tk, tq), 1)
            causal_bias = jnp.where(t0 + col >= k0 + row, 0.0, MASK_BIAS)
        for z in range(NSA_GROUP):
            s = _dot_nt(k_aug, qa_sc[z]) + slopes_ref[g * NSA_GROUP + z] * k_rel
            if causal:
                s = s + causal_bias
            m_prev = m_sc[z]
            m_new = jnp.maximum(m_prev, jnp.max(s, axis=0, keepdims=True))
            alpha = jnp.exp(m_prev - m_new)
            e = jnp.exp(s - m_new)
            l_sc[z] = alpha * l_sc[z] + jnp.sum(e, axis=0, keepdims=True)
            acc_sc[z] = alpha * acc_sc[z] + _dot(v_t, e.astype(BF16))
            m_sc[z] = m_new

    @pl.when(active & on_diagonal)
    def _():
        step(True)

    @pl.when(active & jnp.logical_not(on_diagonal))
    def _():
        step(False)

    @pl.when((flags & _WL_LAST) != 0)
    def _():
        for z in range(NSA_GROUP):
            out_t = acc_sc[z] * (1.0 / jnp.maximum(l_sc[z], 1e-30))
            o_ref[:, z * HEAD_DIM:(z + 1) * HEAD_DIM] = out_t.T


def _sel_work_list(blk_any, t, tq, tk):
    nq, nk, bpt = t // tq, t // tk, tk // SEL_BLOCK
    n_blk = t // SEL_BLOCK
    picked = blk_any[:, :, 0, :n_blk].reshape(NSA_KV_HEADS, nq, nk, bpt).max(axis=-1) > 0.5
    qi = jnp.arange(nq)[:, None]
    kj = jnp.arange(nk)[None, :]
    causal = kj * tk <= qi * tq + tq - 1
    n_work = sum((i * tq + tq - 1) // tk + 1 for i in range(nq))
    act = (picked & causal[None]).reshape(NSA_KV_HEADS, nq * nk)
    order = jnp.argsort(jnp.logical_not(act), axis=1, stable=True)[:, :n_work].astype(jnp.int32)
    n_act = act.sum(axis=1).astype(jnp.int32)[:, None]
    w = jnp.arange(n_work, dtype=jnp.int32)[None, :]
    valid = w < n_act
    pos = jnp.take_along_axis(order, jnp.minimum(w, n_act - 1), axis=1)
    qi_l = pos // nk
    kj_l = pos % nk
    prev_q = jnp.concatenate([jnp.full((NSA_KV_HEADS, 1), -1, jnp.int32), qi_l[:, :-1]], axis=1)
    next_q = jnp.concatenate([qi_l[:, 1:], jnp.full((NSA_KV_HEADS, 1), -1, jnp.int32)], axis=1)
    first = valid & (qi_l != prev_q)
    last = valid & ((qi_l != next_q) | (w == n_act - 1))
    flags = (first * _WL_FIRST + last * _WL_LAST + valid * _WL_ACTIVE).astype(jnp.int32)
    return qi_l.reshape(-1), kj_l.reshape(-1), flags.reshape(-1), n_work


def _sel_attention(q, k_s, v_s, sel, blk_any, slopes, tq=256, tk=512):
    t = q.shape[0]
    tq = min(tq, t)
    tk = min(tk, t)
    qi_l, kj_l, flags, n_work = _sel_work_list(blk_any, t, tq, tk)

    def q_map(g, w, qi_ref, kj_ref, fl_ref):
        return (qi_ref[g * n_work + w], g)

    def kv_map(g, w, qi_ref, kj_ref, fl_ref):
        return (kj_ref[g * n_work + w], g)

    def sel_map(g, w, qi_ref, kj_ref, fl_ref):
        return (g, qi_ref[g * n_work + w], 0)

    def eb_map(g, w, qi_ref, kj_ref, fl_ref):
        return (kj_ref[g * n_work + w], 0, 0)

    blk_of_key = jnp.arange(t)[:, None] // SEL_BLOCK
    block_onehot = jnp.where(blk_of_key == jnp.arange(N_BLK_LANES)[None, :], -MASK_BIAS, 0.0)
    block_onehot = block_onehot.astype(BF16).reshape(t // tk, tk, N_BLK_LANES)

    grid_spec = pltpu.PrefetchScalarGridSpec(
        num_scalar_prefetch=3,
        grid=(NSA_KV_HEADS, n_work),
        in_specs=[pl.BlockSpec(memory_space=pltpu.SMEM),
                  pl.BlockSpec((tq, NSA_GROUP * HEAD_DIM), q_map),
                  pl.BlockSpec((tk, HEAD_DIM), kv_map),
                  pl.BlockSpec((tk, HEAD_DIM), kv_map),
                  pl.BlockSpec((1, tk, N_BLK_LANES), eb_map),
                  pl.BlockSpec((1, tq, N_BLK_LANES), sel_map)],
        out_specs=pl.BlockSpec((tq, NSA_GROUP * HEAD_DIM), q_map),
        scratch_shapes=[pltpu.VMEM((NSA_GROUP, tq, 2 * HEAD_DIM), BF16),
                        pltpu.VMEM((NSA_GROUP, 1, tq), F32), pltpu.VMEM((NSA_GROUP, 1, tq), F32),
                        pltpu.VMEM((NSA_GROUP, HEAD_DIM, tq), F32)])
    return pl.pallas_call(
        functools.partial(_sel_attn_kernel, tq=tq, tk=tk, n_work=n_work),
        grid_spec=grid_spec,
        out_shape=jax.ShapeDtypeStruct((t, NSA_WIDTH), F32),
        compiler_params=_cparams(("arbitrary", "arbitrary"), 40),
        name="nsa_sel_attn",
    )(qi_l, kj_l, flags, slopes, q, k_s, v_s, block_onehot, sel)


def _win_attn_kernel(slopes_ref, q_ref, *refs, tq, n_tile):
    k_refs = refs[:n_tile]
    v_refs = refs[n_tile:2 * n_tile]
    o_ref = refs[2 * n_tile]
    g = pl.program_id(0)
    qi = pl.program_id(1)
    span = n_tile * tq
    k = jnp.concatenate([r[...] for r in k_refs], axis=0)
    v = jnp.concatenate([r[...] for r in v_refs], axis=0)
    t_pos = qi * tq + lax.broadcasted_iota(jnp.int32, (tq, span), 0)
    k_pos = (qi - (n_tile - 1)) * tq + lax.broadcasted_iota(jnp.int32, (tq, span), 1)
    dist = t_pos - k_pos
    mask_bias = jnp.where((dist >= 0) & (dist < WINDOW) & (k_pos >= 0), 0.0, MASK_BIAS)
    k_rel = (lax.broadcasted_iota(jnp.int32, (1, span), 1) - (n_tile - 1) * tq).astype(F32)
    for z in range(NSA_GROUP):
        slope = slopes_ref[g * NSA_GROUP + z]
        sl = slice(z * HEAD_DIM, (z + 1) * HEAD_DIM)
        s = _dot_nt(q_ref[:, sl], k) + (mask_bias + slope * k_rel)
        m = jnp.maximum(jnp.max(s, axis=-1, keepdims=True), NEG_INF)
        e = jnp.exp(s - m)
        p = e * (1.0 / jnp.maximum(jnp.sum(e, axis=-1, keepdims=True), 1e-30))
        o_ref[:, sl] = _dot(p.astype(BF16), v)


def _win_attention(q, k_w, v_w, slopes, tq=256):
    t = q.shape[0]
    tq = min(tq, t)
    n_tile = -(-WINDOW // tq) + 1

    def kv_spec(c):
        return pl.BlockSpec((tq, HEAD_DIM), lambda g, i: (jnp.maximum(i - (n_tile - 1) + c, 0), g))

    return pl.pallas_call(
        functools.partial(_win_attn_kernel, tq=tq, n_tile=n_tile),
        grid=(NSA_KV_HEADS, t // tq),
        in_specs=[pl.BlockSpec(memory_space=pltpu.SMEM),
                  pl.BlockSpec((tq, NSA_GROUP * HEAD_DIM), lambda g, i: (i, g))]
        + [kv_spec(c) for c in range(n_tile)] * 2,
        out_specs=pl.BlockSpec((tq, NSA_GROUP * HEAD_DIM), lambda g, i: (i, g)),
        out_shape=jax.ShapeDtypeStruct((t, NSA_WIDTH), F32),
        compiler_params=_cparams(("parallel", "parallel"), 40),
        name="nsa_win_attn",
    )(slopes, q, *([k_w] * n_tile), *([v_w] * n_tile))


def _nsa_combine_kernel(sm_ref, ex_ref, oc_ref, os_ref, ow_ref, z_ref, o_ref):
    logits = sm_ref[...]
    hi = logits.astype(BF16)
    lo = (logits - hi.astype(F32)).astype(BF16)
    acc = None
    for br, branch_ref in enumerate((oc_ref, os_ref, ow_ref)):
        ex = ex_ref[br]
        gate = _sigmoid(_dot(hi, ex) + _dot(lo, ex))
        term = gate * branch_ref[...]
        acc = term if acc is None else acc + term
    o_ref[...] = (acc * _silu(z_ref[...])).astype(o_ref.dtype)


def _nsa_combine(small, o_cmp, o_slc, o_win, z_a, tm=256):
    t = small.shape[0]
    tm = min(tm, t)
    lane = jnp.arange(LANES)[None, :, None]
    br = jnp.arange(N_BRANCH)[:, None, None]
    head = (jnp.arange(NSA_WIDTH) // HEAD_DIM)[None, None, :]
    expand = (lane == _SM_GATE + head * N_BRANCH + br).astype(BF16)
    row = lambda i: (i, 0)
    wide = pl.BlockSpec((tm, NSA_WIDTH), row)
    return pl.pallas_call(
        _nsa_combine_kernel,
        grid=(t // tm,),
        in_specs=[pl.BlockSpec((tm, LANES), row),
                  pl.BlockSpec((N_BRANCH, LANES, NSA_WIDTH), lambda i: (0, 0, 0)),
                  wide, wide, wide, wide],
        out_specs=wide,
        out_shape=jax.ShapeDtypeStruct((t, NSA_WIDTH), BF16),
        compiler_params=_cparams(("parallel",), 40),
        name="nsa_combine",
    )(small, expand, o_cmp, o_slc, o_win, z_a)


def _shift_rows(cur, halo, s):
    rolled = pltpu.roll(cur, shift=s, axis=0)
    halo_rolled = pltpu.roll(halo, shift=s, axis=0)
    row = lax.broadcasted_iota(jnp.int32, halo.shape, 0)
    head = jnp.where(row < s, halo_rolled, rolled[0:SUBLANES])
    return jnp.concatenate([head, rolled[SUBLANES:]], axis=0)


def _causal_conv(cur, halo, w_ref, k):
    acc = None
    for j in range(k):
        s = k - 1 - j
        term = (cur if s == 0 else _shift_rows(cur, halo, s)) * w_ref[j:j + 1, :]
        acc = term if acc is None else acc + term
    return acc


def _gdn_prep_kernel(p_ref, halo_ref, cw_ref, sm_ref, alog_ref, dtb_ref, q_ref, k_ref, v_ref, gb_ref):
    i = pl.program_id(0)
    cur = p_ref[...]
    halo = jnp.where(i > 0, halo_ref[...], 0.0)
    y = _silu(_causal_conv(cur, halo, cw_ref, GDN_CONV))
    for h in range(GDN_HEADS):
        sl = slice(h * GDN_DK, (h + 1) * GDN_DK)
        qh = y[:, sl]
        q_ref[:, sl] = qh * lax.rsqrt(jnp.sum(qh * qh, axis=-1, keepdims=True) + NORM_EPS) * (GDN_DK ** -0.5)
        kh = y[:, GDN_KEY_WIDTH + h * GDN_DK:GDN_KEY_WIDTH + (h + 1) * GDN_DK]
        k_ref[:, sl] = kh * lax.rsqrt(jnp.sum(kh * kh, axis=-1, keepdims=True) + NORM_EPS)
    v_ref[...] = y[:, 2 * GDN_KEY_WIDTH:]
    sm = sm_ref[...]
    lane = lax.broadcasted_iota(jnp.int32, sm.shape, 1)
    gdecay = -jnp.exp(alog_ref[...]) * jax.nn.softplus(sm + dtb_ref[...])
    beta = _sigmoid(sm)
    gb_ref[...] = jnp.where(lane < _SM_BETA, gdecay, jnp.where(lane < _SM_GATE, beta, 0.0))


def _gdn_prep(p_gdn, small, conv_w, a_log, dt_bias, tm=256):
    t, width = p_gdn.shape
    tm = min(tm, t)
    hb = tm // SUBLANES
    row = lambda i: (i, 0)
    alog = jnp.zeros((1, LANES), F32).at[0, _SM_A:_SM_A + GDN_HEADS].set(a_log)
    dtb = jnp.zeros((1, LANES), F32).at[0, _SM_A:_SM_A + GDN_HEADS].set(dt_bias)
    const = lambda i: (0, 0)
    return pl.pallas_call(
        _gdn_prep_kernel,
        grid=(t // tm,),
        in_specs=[pl.BlockSpec((tm, width), row),
                  pl.BlockSpec((SUBLANES, width), lambda i: (jnp.maximum(i * hb - 1, 0), 0)),
                  pl.BlockSpec((GDN_CONV, width), const),
                  pl.BlockSpec((tm, LANES), row),
                  pl.BlockSpec((1, LANES), const), pl.BlockSpec((1, LANES), const)],
        out_specs=[pl.BlockSpec((tm, GDN_KEY_WIDTH), row), pl.BlockSpec((tm, GDN_KEY_WIDTH), row),
                   pl.BlockSpec((tm, GDN_VAL_WIDTH), row), pl.BlockSpec((tm, LANES), row)],
        out_shape=[jax.ShapeDtypeStruct((t, GDN_KEY_WIDTH), F32), jax.ShapeDtypeStruct((t, GDN_KEY_WIDTH), F32),
                   jax.ShapeDtypeStruct((t, GDN_VAL_WIDTH), F32), jax.ShapeDtypeStruct((t, LANES), F32)],
        compiler_params=_cparams(("parallel",), 48),
        name="gdn_prep",
    )(p_gdn, p_gdn, conv_w, small, alog, dtb)


def _row_pad(x):
    return jnp.concatenate([x, jnp.zeros_like(x)], axis=0)


def _gdn_intra_kernel(q_ref, k_ref, v_ref, gb_ref, u_ref, w_ref, qe_ref, ke_ref, at_ref, eg_ref, *, heads):
    c = GDN_CHUNK
    gb = gb_ref[...]
    row = lax.broadcasted_iota(jnp.int32, gb.shape, 0)
    gcum = gb
    shift = 1
    while shift < c:
        gcum = gcum + jnp.where(row >= shift, pltpu.roll(gcum, shift=shift, axis=0), 0.0)
        shift *= 2
    lane = lax.broadcasted_iota(jnp.int32, gb.shape, 1)
    ri = lax.broadcasted_iota(jnp.int32, (c, LANES), 0)
    ci = lax.broadcasted_iota(jnp.int32, (c, LANES), 1)
    tri = ri >= ci
    strict = ri > ci
    eye = ri == ci
    eyef = eye.astype(F32)
    lane8 = lax.broadcasted_iota(jnp.int32, (SUBLANES, LANES), 1)
    eg_all = jnp.zeros((SUBLANES, LANES), F32)
    hs = range(heads)
    sls = [slice(h * GDN_DK, (h + 1) * GDN_DK) for h in hs]

    gcol = [jnp.sum(jnp.where(lane == _SM_A + h, gcum, 0.0), axis=-1, keepdims=True) for h in hs]
    beta = [jnp.sum(jnp.where(lane == _SM_BETA + h, gb, 0.0), axis=-1, keepdims=True) for h in hs]
    decay = []
    for h in hs:
        gmat = jnp.broadcast_to(gcol[h], (c, LANES))
        grow = jnp.sum(jnp.where(eye, gmat, 0.0), axis=0, keepdims=True)
        decay.append(jnp.where(tri, jnp.exp(jnp.where(tri, gmat - grow, 0.0)), 0.0))
    glast = [gcol[h][c - 1:c, :] for h in hs]
    egc = [jnp.exp(gcol[h]) for h in hs]
    q = [q_ref[:, sls[h]] for h in hs]
    k = [k_ref[:, sls[h]] for h in hs]
    k16 = [k[h].astype(BF16) for h in hs]
    qk = [_dot_nt(jnp.concatenate([q[h].astype(BF16), k16[h]], axis=0), _row_pad(k16[h])) for h in hs]
    for h in hs:
        at_ref[:, sls[h]] = (qk[h][:c] * decay[h]).astype(BF16)

    pw = [-jnp.where(strict, beta[h] * qk[h][c:] * decay[h], 0.0) for h in hs]
    inv = [eyef + pw[h] for h in hs]
    pw16 = [pw[h].astype(BF16) for h in hs]
    pw = [_dot(pw16[h], _row_pad(pw16[h])) for h in hs]
    span = 2
    while span < c:
        pw16 = [pw[h].astype(BF16) for h in hs]
        if 2 * span < c:
            both = [_dot(jnp.concatenate([pw16[h], inv[h].astype(BF16)], axis=0), _row_pad(pw16[h])) for h in hs]
            pw = [both[h][:c] for h in hs]
            inv = [inv[h] + both[h][c:] for h in hs]
        else:
            inv = [inv[h] + _dot(inv[h].astype(BF16), _row_pad(pw16[h])) for h in hs]
        span *= 2

    uw = []
    for h in hs:
        rhs = jnp.concatenate([(v_ref[:, sls[h]] * beta[h]).astype(BF16),
                               (k[h] * (beta[h] * egc[h])).astype(BF16)], axis=1)
        uw.append(_dot(inv[h].astype(BF16), _row_pad(rhs)))
    for h in hs:
        u_ref[:, sls[h]] = uw[h][:, :GDN_DV]
        w_ref[:, sls[h]] = uw[h][:, GDN_DV:].astype(BF16)
        qe_ref[:, sls[h]] = (q[h] * egc[h]).astype(BF16)
        ke_ref[:, sls[h]] = (k[h] * jnp.exp(glast[h] - gcol[h])).astype(BF16)
        eg_all = jnp.where(lane8 == h, jnp.exp(glast[h]), eg_all)
    eg_ref[0] = eg_all


def _gdn_scan_kernel(eg_ref, u_ref, w_ref, qe_ref, ke_ref, at_ref, z_ref, nw_ref, o_ref, state, *, heads):
    n = pl.program_id(0)
    c = GDN_CHUNK

    @pl.when(n == 0)
    def _():
        state[...] = jnp.zeros(state.shape, F32)

    nw = nw_ref[...]
    hs = range(heads)
    sls = [slice(h * GDN_DK, (h + 1) * GDN_DK) for h in hs]
    s_prev = [state[h] for h in hs]
    ws_qs = [_dot(jnp.concatenate([w_ref[:, sls[h]], qe_ref[:, sls[h]]], axis=0), s_prev[h].astype(BF16))
             for h in hs]
    v_new16 = [(u_ref[:, sls[h]] - ws_qs[h][:c]).astype(BF16) for h in hs]
    o = [ws_qs[h][c:] + _dot(at_ref[:, h * GDN_DK:h * GDN_DK + c], v_new16[h]) for h in hs]
    for h in hs:
        state[h] = s_prev[h] * eg_ref[n * heads + h] + _dot_tn(ke_ref[:, sls[h]], v_new16[h])
    for h in hs:
        o_ref[:, sls[h]] = (_head_rms(o[h], nw) * _silu(z_ref[:, sls[h]])).astype(o_ref.dtype)


def _gdn_chunks(q, k, v, gb, z_b, norm_w):
    t = q.shape[0]
    c = GDN_CHUNK
    n_chunk = t // c
    heads = GDN_HEADS
    wide = pl.BlockSpec((c, GDN_VAL_WIDTH), lambda n: (n, 0))
    u, w, qe, ke, attn, eg = pl.pallas_call(
        functools.partial(_gdn_intra_kernel, heads=heads),
        grid=(n_chunk,),
        in_specs=[wide, wide, wide, pl.BlockSpec((c, LANES), lambda n: (n, 0))],
        out_specs=[wide, wide, wide, wide, wide, pl.BlockSpec((1, SUBLANES, LANES), lambda n: (n, 0, 0))],
        out_shape=[jax.ShapeDtypeStruct((t, GDN_VAL_WIDTH), F32)]
        + [jax.ShapeDtypeStruct((t, GDN_VAL_WIDTH), BF16)] * 4
        + [jax.ShapeDtypeStruct((n_chunk, SUBLANES, LANES), F32)],
        compiler_params=_cparams(("parallel",), 40),
        name="gdn_intra",
    )(q, k, v, gb)
    eg_flat = eg[:, 0, :heads].reshape(n_chunk * heads)
    return pl.pallas_call(
        functools.partial(_gdn_scan_kernel, heads=heads),
        grid=(n_chunk,),
        in_specs=[pl.BlockSpec(memory_space=pltpu.SMEM), wide, wide, wide, wide, wide, wide,
                  pl.BlockSpec((1, GDN_DV), lambda n: (0, 0))],
        out_specs=wide,
        out_shape=jax.ShapeDtypeStruct((t, GDN_VAL_WIDTH), BF16),
        scratch_shapes=[pltpu.VMEM((heads, GDN_DK, GDN_DV), F32)],
        compiler_params=_cparams(("arbitrary",), 40),
        name="gdn_scan",
    )(eg_flat, u, w, qe, ke, attn, z_b, norm_w.reshape(1, GDN_DV))


def _short_conv_kernel(bg_ref, cg_ref, x_ref, z_ref, cgh_ref, xh_ref, cw_ref, o_ref):
    i = pl.program_id(0)
    cur = cg_ref[...] * x_ref[...]
    halo = jnp.where(i > 0, cgh_ref[...] * xh_ref[...], 0.0)
    y = _causal_conv(cur, halo, cw_ref, SC_CONV)
    o_ref[...] = (bg_ref[...] * y * _silu(z_ref[...])).astype(o_ref.dtype)


def _short_conv(p_sc, conv_w, tm=256):
    t = p_sc.shape[0]
    tm = min(tm, t)
    hb = tm // SUBLANES
    main = lambda c: pl.BlockSpec((tm, SC_WIDTH), functools.partial(lambda i, c: (i, c), c=c))
    halo = lambda c: pl.BlockSpec((SUBLANES, SC_WIDTH),
                                  functools.partial(lambda i, c: (jnp.maximum(i * hb - 1, 0), c), c=c))
    return pl.pallas_call(
        _short_conv_kernel,
        grid=(t // tm,),
        in_specs=[main(0), main(1), main(2), main(3), halo(1), halo(2),
                  pl.BlockSpec((SC_CONV, SC_WIDTH), lambda i: (0, 0))],
        out_specs=pl.BlockSpec((tm, SC_WIDTH), lambda i: (i, 0)),
        out_shape=jax.ShapeDtypeStruct((t, SC_WIDTH), BF16),
        compiler_params=_cparams(("parallel",), 40),
        name="short_conv",
    )(p_sc, p_sc, p_sc, p_sc, p_sc, p_sc, conv_w)


_IN_GROUPS = (("nsa", _OFF_QA, _OFF_GATE), ("za", _OFF_ZA, _OFF_QB), ("gdn", _OFF_QB, _OFF_AB),
              ("zb", _OFF_ZB, _OFF_SC), ("sc", _OFF_SC, _OFF_MERGE), ("merge", _OFF_MERGE, None))


def _stage_w_in(w_in):
    depth, d, width = w_in.shape
    small_pad = LANES - 2 * GDN_HEADS - 3 * NSA_HEADS
    parts = [w_in[:, :, a:(width if b is None else b)] for _, a, b in _IN_GROUPS]
    parts += [w_in[:, :, _OFF_AB:_OFF_ZB], w_in[:, :, _OFF_GATE:_OFF_ZA], jnp.zeros((depth, d, small_pad), F32)]
    return jnp.concatenate(parts, axis=2).astype(BF16)


def _layer(x, layer, norm_w, w_st, in_width, nsa_qk_norm, cmp_pos, cmp_w1, cmp_w2, gdn_conv_w, gdn_a_log,
           gdn_dt_bias, gdn_norm_w, sc_conv_w, wb_nsa, wb_gdn, wb_sc, w_out, slopes):
    hn = _rmsnorm(x, norm_w)
    proj = {}
    off = 0
    for name, a, b in _IN_GROUPS:
        n = (in_width if b is None else b) - a
        proj[name] = _matmul(hn, w_st, layer, off, n, F32, 1024, 512, "in_proj_" + name)
        off += n
    small = _matmul(hn, w_st, layer, off, LANES, F32, 1024, LANES, "in_proj_small")
    p_nsa, z_a, p_gdn, z_b, p_sc, p_mg = (proj[n] for n, _, _ in _IN_GROUPS)

    q, kv_c, k_s, v_s, k_w, v_w = _nsa_prep(p_nsa, nsa_qk_norm)
    kv_cmp = _compress(kv_c, cmp_pos, cmp_w1, cmp_w2, nsa_qk_norm[1])
    o_cmp, sel, blk_any = _cmp_attention(q, kv_cmp, slopes)
    o_slc = _sel_attention(q, k_s, v_s, sel, blk_any, slopes)
    o_win = _win_attention(q, k_w, v_w, slopes)
    o_a = _nsa_combine(small, o_cmp, o_slc, o_win, z_a)

    q_b, k_b, v_b, gb = _gdn_prep(p_gdn, small, gdn_conv_w, gdn_a_log, gdn_dt_bias)
    o_b = _gdn_chunks(q_b, k_b, v_b, gb, z_b, gdn_norm_w)

    o_c = _short_conv(p_sc, sc_conv_w)

    merged = _merge(o_a, o_b, o_c, wb_nsa, wb_gdn, wb_sc, layer, p_mg)
    return _matmul_residual(merged, w_out, layer, x, 1024, 512)


def kernel(x, norm_w, w_in, nsa_qk_norm, cmp_pos, cmp_w1, cmp_w2, gdn_conv_w, gdn_a_log, gdn_dt_bias,
           gdn_norm_w, sc_conv_w, w_branch_nsa, w_branch_gdn, w_branch_sc, w_out):
    b, t, d = x.shape
    depth = norm_w.shape[0]
    heads = jnp.arange(1, NSA_HEADS + 1, dtype=F32)
    slopes = jnp.exp2(-8.0 * heads / NSA_HEADS)
    w_st = _stage_w_in(w_in)
    wb_nsa, wb_gdn, wb_sc, w_out16 = (w.astype(BF16) for w in (w_branch_nsa, w_branch_gdn, w_branch_sc, w_out))
    outs = []
    for bi in range(b):
        xb = x[bi]
        for l in range(depth):
            xb = _layer(xb, l, norm_w[l], w_st, w_in.shape[2], nsa_qk_norm[l], cmp_pos[l], cmp_w1[l], cmp_w2[l],
                        gdn_conv_w[l], gdn_a_log[l], gdn_dt_bias[l], gdn_norm_w[l], sc_conv_w[l],
                        wb_nsa, wb_gdn, wb_sc, w_out16, slopes)
        outs.append(xb)
    return jnp.stack(outs, axis=0)
```

```python
import functools
import math

import jax
import jax.numpy as jnp
from jax import lax
from jax.experimental import pallas as pl
from jax.experimental.pallas import tpu as pltpu

F32 = jnp.float32
BF16 = jnp.bfloat16

HEAD_DIM = 128
NSA_HEADS = 16
NSA_KV_HEADS = 4
NSA_GROUP = NSA_HEADS // NSA_KV_HEADS
NSA_WIDTH = NSA_HEADS * HEAD_DIM
NSA_KV_WIDTH = NSA_KV_HEADS * HEAD_DIM
CMP_BLOCK = 32
CMP_STRIDE = 16
CMP_HIDDEN = 256
SEL_BLOCK = 64
SEL_TOPK = 16
WINDOW = 512
GDN_HEADS = 16
GDN_DK = 128
GDN_DV = 128
GDN_KEY_WIDTH = GDN_HEADS * GDN_DK
GDN_VAL_WIDTH = GDN_HEADS * GDN_DV
GDN_CONV = 4
GDN_CHUNK = 64
SC_WIDTH = 2048
SC_CONV = 3
N_BRANCH = 3
NORM_EPS = 1e-6
NEG_INF = -1e30
MASK_BIAS = -2e30
FORCE_SCORE = 1e6

LANES = 128
SUBLANES = 8
BF16_SUBLANES = 16
N_BLK_LANES = 128
MIB = 1024 * 1024

_OFF_QA = 0
_OFF_KVC = _OFF_QA + NSA_WIDTH
_OFF_KVS = _OFF_KVC + 2 * NSA_KV_WIDTH
_OFF_KVW = _OFF_KVS + 2 * NSA_KV_WIDTH
_OFF_GATE = _OFF_KVW + 2 * NSA_KV_WIDTH
_OFF_ZA = _OFF_GATE + 3 * NSA_HEADS
_OFF_QB = _OFF_ZA + NSA_WIDTH
_OFF_AB = _OFF_QB + 2 * GDN_KEY_WIDTH + GDN_VAL_WIDTH
_OFF_BETA = _OFF_AB + GDN_HEADS
_OFF_ZB = _OFF_BETA + GDN_HEADS
_OFF_SC = _OFF_ZB + GDN_VAL_WIDTH
_OFF_MERGE = _OFF_SC + 4 * SC_WIDTH

_SM_A = 0
_SM_BETA = GDN_HEADS
_SM_GATE = 2 * GDN_HEADS


def _cparams(sem, vmem_mib):
    return pltpu.CompilerParams(dimension_semantics=sem, vmem_limit_bytes=vmem_mib * MIB)


def _sigmoid(x):
    return jax.nn.sigmoid(x)


def _silu(x):
    return x * jax.nn.sigmoid(x)


def _dot(a, b):
    return jnp.dot(a, b, preferred_element_type=F32)


def _dot_nt(a, b):
    return lax.dot_general(a, b, (((1,), (1,)), ((), ())), preferred_element_type=F32)


def _dot_tn(a, b):
    return lax.dot_general(a, b, (((0,), (0,)), ((), ())), preferred_element_type=F32)


def _head_rms(x, w):
    return x * lax.rsqrt(jnp.mean(x * x, axis=-1, keepdims=True) + NORM_EPS) * w


def _rmsnorm_kernel(x_ref, w_ref, o_ref):
    x = x_ref[...]
    y = x * lax.rsqrt(jnp.mean(x * x, axis=-1, keepdims=True) + NORM_EPS)
    o_ref[...] = (y * w_ref[...]).astype(o_ref.dtype)


def _rmsnorm(x, w, tm=256):
    t, d = x.shape
    return pl.pallas_call(
        _rmsnorm_kernel,
        grid=(t // tm,),
        in_specs=[pl.BlockSpec((tm, d), lambda i: (i, 0)), pl.BlockSpec((1, d), lambda i: (0, 0))],
        out_specs=pl.BlockSpec((tm, d), lambda i: (i, 0)),
        out_shape=jax.ShapeDtypeStruct((t, d), BF16),
        compiler_params=_cparams(("parallel",), 32),
        name="rmsnorm",
    )(x, w.reshape(1, d))


def _mm_kernel(a_ref, b_ref, o_ref):
    o_ref[...] = _dot(a_ref[...], b_ref[...]).astype(o_ref.dtype)


def _mm_nt_kernel(a_ref, b_ref, o_ref):
    o_ref[...] = _dot_nt(a_ref[...], b_ref[...]).astype(o_ref.dtype)


def _matmul_nt(a, b_t, layer, row_off, n, out_dtype, tm, tn, name):
    m, k = a.shape
    tm = min(tm, m)
    tn = min(tn, n)
    depth, rows, _ = b_t.shape
    first_row = layer * rows + row_off
    assert first_row % BF16_SUBLANES == 0 and n % tn == 0
    b_t = b_t.reshape(depth * rows, k)
    return pl.pallas_call(
        _mm_nt_kernel,
        grid=(m // tm, n // tn),
        in_specs=[pl.BlockSpec((tm, k), lambda i, j: (i, 0)),
                  pl.BlockSpec((pl.Element(tn), pl.Element(k)),
                               lambda i, j: (pl.multiple_of(first_row + j * tn, BF16_SUBLANES), 0))],
        out_specs=pl.BlockSpec((tm, tn), lambda i, j: (i, j)),
        out_shape=jax.ShapeDtypeStruct((m, n), out_dtype),
        compiler_params=_cparams(("parallel", "arbitrary"), 48),
        name=name,
    )(a, b_t)


def _mm_res_kernel(a_ref, b_ref, r_ref, o_ref):
    o_ref[...] = r_ref[...] + _dot(a_ref[...], b_ref[...])


def _matmul_residual(a, b, layer, r, tm, tn):
    m, k = a.shape
    n = b.shape[2]
    tm = min(tm, m)
    return pl.pallas_call(
        _mm_res_kernel,
        grid=(m // tm, n // tn),
        in_specs=[pl.BlockSpec((tm, k), lambda i, j: (i, 0)),
                  pl.BlockSpec((None, k, tn), lambda i, j: (layer, 0, j)),
                  pl.BlockSpec((tm, tn), lambda i, j: (i, j))],
        out_specs=pl.BlockSpec((tm, tn), lambda i, j: (i, j)),
        out_shape=jax.ShapeDtypeStruct((m, n), F32),
        compiler_params=_cparams(("parallel", "arbitrary"), 48),
        name="out_proj_residual",
    )(a, b, r)


def _merge_kernel(oa_ref, ob_ref, oc_ref, wa_ref, wb_ref, wc_ref, ga_ref, gb_ref, gc_ref, o_ref):
    acc = _sigmoid(ga_ref[...]) * _dot(oa_ref[...], wa_ref[...])
    acc = acc + _sigmoid(gb_ref[...]) * _dot(ob_ref[...], wb_ref[...])
    acc = acc + _sigmoid(gc_ref[...]) * _dot(oc_ref[...], wc_ref[...])
    o_ref[...] = acc.astype(o_ref.dtype)


def _merge(o_a, o_b, o_c, wa, wb, wc, layer, gates, tm=512, tn=512):
    m, k = o_a.shape
    n = wa.shape[2]
    tm = min(tm, m)
    nb = n // tn
    a_spec = pl.BlockSpec((tm, k), lambda i, j: (i, 0))
    w_spec = pl.BlockSpec((None, k, tn), lambda i, j: (layer, 0, j))
    g_specs = [pl.BlockSpec((tm, tn), functools.partial(lambda i, j, br: (i, br * nb + j), br=br))
               for br in range(N_BRANCH)]
    return pl.pallas_call(
        _merge_kernel,
        grid=(m // tm, nb),
        in_specs=[a_spec, a_spec, a_spec, w_spec, w_spec, w_spec] + g_specs,
        out_specs=pl.BlockSpec((tm, tn), lambda i, j: (i, j)),
        out_shape=jax.ShapeDtypeStruct((m, n), BF16),
        compiler_params=_cparams(("parallel", "arbitrary"), 48),
        name="branch_merge",
    )(o_a, o_b, o_c, wa, wb, wc, gates, gates, gates)


def _nsa_prep_kernel(p_ref, nw_ref, q_ref, kvc_ref, ks_ref, vs_ref, kw_ref, vw_ref):
    nw = nw_ref[...]
    for h in range(NSA_HEADS):
        sl = slice(h * HEAD_DIM, (h + 1) * HEAD_DIM)
        q_ref[:, sl] = (_head_rms(p_ref[:, sl], nw[0:1]) * (HEAD_DIM ** -0.5)).astype(BF16)
    kvc_ref[...] = p_ref[:, _OFF_KVC:_OFF_KVS]
    for g in range(NSA_KV_HEADS):
        sl = slice(g * HEAD_DIM, (g + 1) * HEAD_DIM)
        ks = p_ref[:, _OFF_KVS + g * HEAD_DIM:_OFF_KVS + (g + 1) * HEAD_DIM]
        ks_ref[:, sl] = _head_rms(ks, nw[2:3]).astype(BF16)
        vs_ref[:, sl] = p_ref[:, _OFF_KVS + NSA_KV_WIDTH + g * HEAD_DIM:
                              _OFF_KVS + NSA_KV_WIDTH + (g + 1) * HEAD_DIM].astype(BF16)
        kw = p_ref[:, _OFF_KVW + g * HEAD_DIM:_OFF_KVW + (g + 1) * HEAD_DIM]
        kw_ref[:, sl] = _head_rms(kw, nw[3:4]).astype(BF16)
        vw_ref[:, sl] = p_ref[:, _OFF_KVW + NSA_KV_WIDTH + g * HEAD_DIM:
                              _OFF_KVW + NSA_KV_WIDTH + (g + 1) * HEAD_DIM].astype(BF16)


def _nsa_prep(p_nsa, qk_norm, tm=256):
    t, width = p_nsa.shape
    row = lambda i: (i, 0)
    return pl.pallas_call(
        _nsa_prep_kernel,
        grid=(t // tm,),
        in_specs=[pl.BlockSpec((tm, width), row), pl.BlockSpec((4, HEAD_DIM), lambda i: (0, 0))],
        out_specs=[pl.BlockSpec((tm, NSA_WIDTH), row)] + [pl.BlockSpec((tm, 2 * NSA_KV_WIDTH), row)]
        + [pl.BlockSpec((tm, NSA_KV_WIDTH), row)] * 4,
        out_shape=[jax.ShapeDtypeStruct((t, NSA_WIDTH), BF16), jax.ShapeDtypeStruct((t, 2 * NSA_KV_WIDTH), F32)]
        + [jax.ShapeDtypeStruct((t, NSA_KV_WIDTH), BF16)] * 4,
        compiler_params=_cparams(("parallel",), 40),
        name="nsa_prep",
    )(p_nsa, qk_norm)


def _compress_kernel(x_ref, pos_ref, w1lo_ref, w1hi_ref, w2_ref, nw_ref, o_ref, lo_acc, hi_acc, *, n_chunk):
    kv = pl.program_id(0)
    l = pl.program_id(1)

    @pl.when(l == 0)
    def _():
        lo_acc[...] = jnp.zeros(lo_acc.shape, F32)
        hi_acc[...] = jnp.zeros(hi_acc.shape, F32)

    pos_lo = pos_ref[0, pl.ds(l, 1), :]
    pos_hi = pos_ref[0, pl.ds(l + CMP_STRIDE, 1), :]
    w_lo = w1lo_ref[0, 0]
    w_hi = w1hi_ref[0, 0]
    for g in range(NSA_KV_HEADS):
        x = x_ref[:, g * HEAD_DIM:(g + 1) * HEAD_DIM]
        lo_acc[g] += _dot((x + pos_lo).astype(BF16), w_lo)
        hi_acc[g] += _dot((x + pos_hi).astype(BF16), w_hi)

    @pl.when(l == CMP_STRIDE - 1)
    def _():
        for g in range(NSA_KV_HEADS):
            hidden = _silu(lo_acc[g] + pltpu.roll(hi_acc[g], shift=n_chunk - 1, axis=0))
            out = _dot(hidden.astype(BF16), w2_ref[0])
            normed = _head_rms(out, nw_ref[...])
            o_ref[0, g] = jnp.where(kv == 0, normed, out).astype(BF16)


def _compress(kv_c, cmp_pos, cmp_w1, cmp_w2, k_norm_w):
    t = kv_c.shape[0]
    n_chunk = t // CMP_STRIDE
    x = kv_c.reshape(n_chunk, CMP_STRIDE * 2 * NSA_KV_WIDTH)
    w1 = cmp_w1.astype(BF16)
    w2 = cmp_w2.astype(BF16)
    return pl.pallas_call(
        functools.partial(_compress_kernel, n_chunk=n_chunk),
        grid=(2, CMP_STRIDE),
        in_specs=[pl.BlockSpec((n_chunk, NSA_KV_WIDTH), lambda a, l: (0, 2 * l + a)),
                  pl.BlockSpec((1, CMP_BLOCK, HEAD_DIM), lambda a, l: (a, 0, 0)),
                  pl.BlockSpec((1, 1, HEAD_DIM, CMP_HIDDEN), lambda a, l: (a, l, 0, 0)),
                  pl.BlockSpec((1, 1, HEAD_DIM, CMP_HIDDEN), lambda a, l: (a, l + CMP_STRIDE, 0, 0)),
                  pl.BlockSpec((1, CMP_HIDDEN, HEAD_DIM), lambda a, l: (a, 0, 0)),
                  pl.BlockSpec((1, HEAD_DIM), lambda a, l: (0, 0))],
        out_specs=pl.BlockSpec((1, NSA_KV_HEADS, n_chunk, HEAD_DIM), lambda a, l: (a, 0, 0, 0)),
        out_shape=jax.ShapeDtypeStruct((2, NSA_KV_HEADS, n_chunk, HEAD_DIM), BF16),
        scratch_shapes=[pltpu.VMEM((NSA_KV_HEADS, n_chunk, CMP_HIDDEN), F32),
                        pltpu.VMEM((NSA_KV_HEADS, n_chunk, CMP_HIDDEN), F32)],
        compiler_params=_cparams(("arbitrary", "arbitrary"), 40),
        name="nsa_compress",
    )(x, cmp_pos, w1, w1, w2, k_norm_w.reshape(1, HEAD_DIM))


def _cmp_attn_kernel(slopes_ref, q_ref, k_ref, v_ref, cov_ref, o_ref, sel_ref, any_ref, *, tq, n_chunk):
    g = pl.program_id(0)
    qi = pl.program_id(1)
    k = k_ref[0, 0]
    v_t = v_ref[0, 0].T
    t0 = qi * tq
    t_pos = t0 + lax.broadcasted_iota(jnp.int32, (n_chunk, tq), 1)
    k_end = lax.broadcasted_iota(jnp.int32, (n_chunk, tq), 0) * CMP_STRIDE + (CMP_BLOCK - 1)
    mask_bias = jnp.where(t_pos >= k_end, 0.0, MASK_BIAS)
    k_rel = (k_end - t0).astype(F32)
    psum = jnp.zeros((n_chunk, tq), F32)
    for z in range(NSA_GROUP):
        slope = slopes_ref[g * NSA_GROUP + z]
        sl = slice(z * HEAD_DIM, (z + 1) * HEAD_DIM)
        s = _dot_nt(k, q_ref[:, sl]) + (mask_bias + slope * k_rel)
        m = jnp.maximum(jnp.max(s, axis=0, keepdims=True), NEG_INF)
        e = jnp.exp(s - m)
        p = e * (1.0 / jnp.maximum(jnp.sum(e, axis=0, keepdims=True), 1e-30))
        o_ref[:, sl] = _dot(v_t, p.astype(BF16)).T
        psum = psum + p

    p_hi = psum.astype(BF16)
    p_lo = (psum - p_hi.astype(F32)).astype(BF16)
    cov_t = cov_ref[...]
    imp = _dot(cov_t, p_hi) + _dot(cov_t, p_lo)

    j = lax.broadcasted_iota(jnp.int32, (N_BLK_LANES, tq), 0)
    cur = (t0 + lax.broadcasted_iota(jnp.int32, (N_BLK_LANES, tq), 1)) // SEL_BLOCK
    valid = j <= cur
    forced = (j == 0) | (j == cur) | (j == cur - 1)
    val = jnp.where(valid, jnp.where(forced, FORCE_SCORE, imp), -1.0)
    sel = jnp.zeros((N_BLK_LANES, tq), F32)
    jf = j.astype(F32)
    for _ in range(SEL_TOPK):
        m = jnp.max(val, axis=0, keepdims=True)
        idx = jnp.min(jnp.where(val == m, jf, float(N_BLK_LANES)), axis=0, keepdims=True)
        hit = jf == idx
        sel = jnp.where(hit & (m > -0.5), 1.0, sel)
        val = jnp.where(hit, -2.0, val)
    sel_q = sel.T
    sel_ref[0] = sel_q.astype(BF16)
    any_ref[0, 0] = jnp.broadcast_to(jnp.max(sel_q, axis=0, keepdims=True), (SUBLANES, N_BLK_LANES))


def _cmp_attention(q, kv_cmp, slopes, tq=256):
    t = q.shape[0]
    n_chunk = kv_cmp.shape[2]
    n_cmp = n_chunk - CMP_BLOCK // CMP_STRIDE + 1
    n_blk = t // SEL_BLOCK
    assert n_blk <= N_BLK_LANES
    tq = min(tq, t)
    cs = jnp.arange(n_chunk)[:, None] * CMP_STRIDE
    bs = jnp.arange(N_BLK_LANES)[None, :] * SEL_BLOCK
    cover = ((cs <= bs + SEL_BLOCK - 1) & (cs + CMP_BLOCK - 1 >= bs)
             & (jnp.arange(n_chunk)[:, None] < n_cmp) & (jnp.arange(N_BLK_LANES)[None, :] < n_blk))
    cover = cover.astype(BF16).T
    return pl.pallas_call(
        functools.partial(_cmp_attn_kernel, tq=tq, n_chunk=n_chunk),
        grid=(NSA_KV_HEADS, t // tq),
        in_specs=[pl.BlockSpec(memory_space=pltpu.SMEM),
                  pl.BlockSpec((tq, NSA_GROUP * HEAD_DIM), lambda g, i: (i, g)),
                  pl.BlockSpec((1, 1, n_chunk, HEAD_DIM), lambda g, i: (0, g, 0, 0)),
                  pl.BlockSpec((1, 1, n_chunk, HEAD_DIM), lambda g, i: (1, g, 0, 0)),
                  pl.BlockSpec((N_BLK_LANES, n_chunk), lambda g, i: (0, 0))],
        out_specs=[pl.BlockSpec((tq, NSA_GROUP * HEAD_DIM), lambda g, i: (i, g)),
                   pl.BlockSpec((1, tq, N_BLK_LANES), lambda g, i: (g, i, 0)),
                   pl.BlockSpec((1, 1, SUBLANES, N_BLK_LANES), lambda g, i: (g, i, 0, 0))],
        out_shape=[jax.ShapeDtypeStruct((t, NSA_WIDTH), F32),
                   jax.ShapeDtypeStruct((NSA_KV_HEADS, t, N_BLK_LANES), BF16),
                   jax.ShapeDtypeStruct((NSA_KV_HEADS, t // tq, SUBLANES, N_BLK_LANES), F32)],
        compiler_params=_cparams(("parallel", "parallel"), 40),
        name="nsa_cmp_attn",
    )(slopes, q, kv_cmp, kv_cmp, cover)


def _sel_attn_kernel(cnt_ref, kjs_ref, slopes_ref, q_ref, k_ref, v_ref, eb_ref, sel_ref, o_ref,
                     qa_sc, m_sc, l_sc, acc_sc, *, tq, tk, max_tiles):
    g = pl.program_id(0)
    qi = pl.program_id(1)
    item = g * pl.num_programs(1) + qi
    t0 = qi * tq

    m_sc[...] = jnp.full(m_sc.shape, NEG_INF, F32)
    l_sc[...] = jnp.zeros(l_sc.shape, F32)
    acc_sc[...] = jnp.zeros(acc_sc.shape, F32)
    unselected = sel_ref[0] - 1.0
    for z in range(NSA_GROUP):
        qa_sc[z] = jnp.concatenate([q_ref[:, z * HEAD_DIM:(z + 1) * HEAD_DIM], unselected], axis=1)

    def tile(kj, causal):
        k0 = kj * tk
        rows = pl.ds(pl.multiple_of(k0, tk), tk)
        k_aug = jnp.concatenate([k_ref[rows, :], eb_ref[kj]], axis=1)
        v_t = v_ref[rows, :].T
        row = lax.broadcasted_iota(jnp.int32, (tk, tq), 0)
        k_rel = (k0 - t0 + row).astype(F32)
        if causal:
            col = lax.broadcasted_iota(jnp.int32, (tk, tq), 1)
            causal_bias = jnp.where(t0 + col >= k0 + row, 0.0, MASK_BIAS)
        for z in range(NSA_GROUP):
            s = _dot_nt(k_aug, qa_sc[z]) + slopes_ref[g * NSA_GROUP + z] * k_rel
            if causal:
                s = s + causal_bias
            m_prev = m_sc[z]
            m_new = jnp.maximum(m_prev, jnp.max(s, axis=0, keepdims=True))
            alpha = jnp.exp(m_prev - m_new)
            e = jnp.exp(s - m_new)
            l_sc[z] = alpha * l_sc[z] + jnp.sum(e, axis=0, keepdims=True)
            acc_sc[z] = alpha * acc_sc[z] + _dot(v_t, e.astype(BF16))
            m_sc[z] = m_new

    def visit(i, carry):
        tile(kjs_ref[item * max_tiles + i], False)
        return carry

    lax.fori_loop(0, cnt_ref[item], visit, 0)
    tile(t0 // tk, True)

    for z in range(NSA_GROUP):
        out_t = acc_sc[z] * (1.0 / jnp.maximum(l_sc[z], 1e-30))
        o_ref[:, z * HEAD_DIM:(z + 1) * HEAD_DIM] = out_t.T


def _sel_work_list(blk_any, t, tq, tk):
    nq, nk, bpt = t // tq, t // tk, tk // SEL_BLOCK
    n_blk = t // SEL_BLOCK
    picked = blk_any[:, :, 0, :n_blk].reshape(NSA_KV_HEADS, nq, nk, bpt).max(axis=-1) > 0.5
    diagonal = (jnp.arange(nq) * tq) // tk
    before = picked & (jnp.arange(nk)[None, :] < diagonal[:, None])[None]
    count = before.sum(axis=-1).astype(jnp.int32)
    tiles = jnp.argsort(jnp.logical_not(before), axis=-1, stable=True).astype(jnp.int32)
    return count.reshape(-1), tiles.reshape(-1)


def _sel_attention(q, k_s, v_s, sel, blk_any, slopes, tq=256, tk=512):
    t = q.shape[0]
    tq = min(tq, t)
    tk = min(tk, t)
    assert tk % tq == 0
    nk = t // tk
    count, tiles = _sel_work_list(blk_any, t, tq, tk)

    blk_of_key = jnp.arange(t)[:, None] // SEL_BLOCK
    block_onehot = jnp.where(blk_of_key == jnp.arange(N_BLK_LANES)[None, :], -MASK_BIAS, 0.0)
    block_onehot = block_onehot.astype(BF16).reshape(nk, tk, N_BLK_LANES)

    smem = pl.BlockSpec(memory_space=pltpu.SMEM)
    return pl.pallas_call(
        functools.partial(_sel_attn_kernel, tq=tq, tk=tk, max_tiles=nk),
        grid=(NSA_KV_HEADS, t // tq),
        in_specs=[smem, smem, smem,
                  pl.BlockSpec((tq, NSA_GROUP * HEAD_DIM), lambda g, i: (i, g)),
                  pl.BlockSpec((t, HEAD_DIM), lambda g, i: (0, g)),
                  pl.BlockSpec((t, HEAD_DIM), lambda g, i: (0, g)),
                  pl.BlockSpec((nk, tk, N_BLK_LANES), lambda g, i: (0, 0, 0)),
                  pl.BlockSpec((1, tq, N_BLK_LANES), lambda g, i: (g, i, 0))],
        out_specs=pl.BlockSpec((tq, NSA_GROUP * HEAD_DIM), lambda g, i: (i, g)),
        out_shape=jax.ShapeDtypeStruct((t, NSA_WIDTH), F32),
        scratch_shapes=[pltpu.VMEM((NSA_GROUP, tq, 2 * HEAD_DIM), BF16),
                        pltpu.VMEM((NSA_GROUP, 1, tq), F32), pltpu.VMEM((NSA_GROUP, 1, tq), F32),
                        pltpu.VMEM((NSA_GROUP, HEAD_DIM, tq), F32)],
        compiler_params=_cparams(("arbitrary", "arbitrary"), 40),
        name="nsa_sel_attn",
    )(count, tiles, slopes, q, k_s, v_s, block_onehot, sel)


def _win_attn_kernel(slopes_ref, q_ref, *refs, tq, n_tile):
    k_refs = refs[:n_tile]
    v_refs = refs[n_tile:2 * n_tile]
    o_ref = refs[2 * n_tile]
    g = pl.program_id(0)
    qi = pl.program_id(1)
    span = n_tile * tq
    k = jnp.concatenate([r[...] for r in k_refs], axis=0)
    v = jnp.concatenate([r[...] for r in v_refs], axis=0)
    t_pos = qi * tq + lax.broadcasted_iota(jnp.int32, (tq, span), 0)
    k_pos = (qi - (n_tile - 1)) * tq + lax.broadcasted_iota(jnp.int32, (tq, span), 1)
    dist = t_pos - k_pos
    mask_bias = jnp.where((dist >= 0) & (dist < WINDOW) & (k_pos >= 0), 0.0, MASK_BIAS)
    k_rel = (lax.broadcasted_iota(jnp.int32, (1, span), 1) - (n_tile - 1) * tq).astype(F32)
    for z in range(NSA_GROUP):
        slope = slopes_ref[g * NSA_GROUP + z]
        sl = slice(z * HEAD_DIM, (z + 1) * HEAD_DIM)
        s = _dot_nt(q_ref[:, sl], k) + (mask_bias + slope * k_rel)
        m = jnp.maximum(jnp.max(s, axis=-1, keepdims=True), NEG_INF)
        e = jnp.exp(s - m)
        p = e * (1.0 / jnp.maximum(jnp.sum(e, axis=-1, keepdims=True), 1e-30))
        o_ref[:, sl] = _dot(p.astype(BF16), v)


def _win_attention(q, k_w, v_w, slopes, tq=256):
    t = q.shape[0]
    tq = min(tq, t)
    n_tile = -(-WINDOW // tq) + 1

    def kv_spec(c):
        return pl.BlockSpec((tq, HEAD_DIM), lambda g, i: (jnp.maximum(i - (n_tile - 1) + c, 0), g))

    return pl.pallas_call(
        functools.partial(_win_attn_kernel, tq=tq, n_tile=n_tile),
        grid=(NSA_KV_HEADS, t // tq),
        in_specs=[pl.BlockSpec(memory_space=pltpu.SMEM),
                  pl.BlockSpec((tq, NSA_GROUP * HEAD_DIM), lambda g, i: (i, g))]
        + [kv_spec(c) for c in range(n_tile)] * 2,
        out_specs=pl.BlockSpec((tq, NSA_GROUP * HEAD_DIM), lambda g, i: (i, g)),
        out_shape=jax.ShapeDtypeStruct((t, NSA_WIDTH), F32),
        compiler_params=_cparams(("parallel", "parallel"), 40),
        name="nsa_win_attn",
    )(slopes, q, *([k_w] * n_tile), *([v_w] * n_tile))


def _nsa_combine_kernel(sm_ref, ex_ref, oc_ref, os_ref, ow_ref, z_ref, o_ref):
    logits = sm_ref[...]
    hi = logits.astype(BF16)
    lo = (logits - hi.astype(F32)).astype(BF16)
    acc = None
    for br, branch_ref in enumerate((oc_ref, os_ref, ow_ref)):
        ex = ex_ref[br]
        gate = _sigmoid(_dot(hi, ex) + _dot(lo, ex))
        term = gate * branch_ref[...]
        acc = term if acc is None else acc + term
    o_ref[...] = (acc * _silu(z_ref[...])).astype(o_ref.dtype)


def _nsa_combine(small, o_cmp, o_slc, o_win, z_a, tm=256):
    t = small.shape[0]
    tm = min(tm, t)
    lane = jnp.arange(LANES)[None, :, None]
    br = jnp.arange(N_BRANCH)[:, None, None]
    head = (jnp.arange(NSA_WIDTH) // HEAD_DIM)[None, None, :]
    expand = (lane == _SM_GATE + head * N_BRANCH + br).astype(BF16)
    row = lambda i: (i, 0)
    wide = pl.BlockSpec((tm, NSA_WIDTH), row)
    return pl.pallas_call(
        _nsa_combine_kernel,
        grid=(t // tm,),
        in_specs=[pl.BlockSpec((tm, LANES), row),
                  pl.BlockSpec((N_BRANCH, LANES, NSA_WIDTH), lambda i: (0, 0, 0)),
                  wide, wide, wide, wide],
        out_specs=wide,
        out_shape=jax.ShapeDtypeStruct((t, NSA_WIDTH), BF16),
        compiler_params=_cparams(("parallel",), 40),
        name="nsa_combine",
    )(small, expand, o_cmp, o_slc, o_win, z_a)


def _shift_rows(cur, halo, s):
    rolled = pltpu.roll(cur, shift=s, axis=0)
    halo_rolled = pltpu.roll(halo, shift=s, axis=0)
    row = lax.broadcasted_iota(jnp.int32, halo.shape, 0)
    head = jnp.where(row < s, halo_rolled, rolled[0:SUBLANES])
    return jnp.concatenate([head, rolled[SUBLANES:]], axis=0)


def _causal_conv(cur, halo, w_ref, k):
    acc = None
    for j in range(k):
        s = k - 1 - j
        term = (cur if s == 0 else _shift_rows(cur, halo, s)) * w_ref[j:j + 1, :]
        acc = term if acc is None else acc + term
    return acc


def _gdn_prep_kernel(p_ref, halo_ref, cw_ref, sm_ref, alog_ref, dtb_ref, q_ref, k_ref, v_ref, gb_ref):
    i = pl.program_id(0)
    cur = p_ref[...]
    halo = jnp.where(i > 0, halo_ref[...], 0.0)
    y = _silu(_causal_conv(cur, halo, cw_ref, GDN_CONV))
    for h in range(GDN_HEADS):
        sl = slice(h * GDN_DK, (h + 1) * GDN_DK)
        qh = y[:, sl]
        q_ref[:, sl] = qh * lax.rsqrt(jnp.sum(qh * qh, axis=-1, keepdims=True) + NORM_EPS) * (GDN_DK ** -0.5)
        kh = y[:, GDN_KEY_WIDTH + h * GDN_DK:GDN_KEY_WIDTH + (h + 1) * GDN_DK]
        k_ref[:, sl] = kh * lax.rsqrt(jnp.sum(kh * kh, axis=-1, keepdims=True) + NORM_EPS)
    v_ref[...] = y[:, 2 * GDN_KEY_WIDTH:]
    sm = sm_ref[...]
    lane = lax.broadcasted_iota(jnp.int32, sm.shape, 1)
    gdecay = -jnp.exp(alog_ref[...]) * jax.nn.softplus(sm + dtb_ref[...])
    beta = _sigmoid(sm)
    gb_ref[...] = jnp.where(lane < _SM_BETA, gdecay, jnp.where(lane < _SM_GATE, beta, 0.0))


def _gdn_prep(p_gdn, small, conv_w, a_log, dt_bias, tm=256):
    t, width = p_gdn.shape
    tm = min(tm, t)
    hb = tm // SUBLANES
    row = lambda i: (i, 0)
    alog = jnp.zeros((1, LANES), F32).at[0, _SM_A:_SM_A + GDN_HEADS].set(a_log)
    dtb = jnp.zeros((1, LANES), F32).at[0, _SM_A:_SM_A + GDN_HEADS].set(dt_bias)
    const = lambda i: (0, 0)
    return pl.pallas_call(
        _gdn_prep_kernel,
        grid=(t // tm,),
        in_specs=[pl.BlockSpec((tm, width), row),
                  pl.BlockSpec((SUBLANES, width), lambda i: (jnp.maximum(i * hb - 1, 0), 0)),
                  pl.BlockSpec((GDN_CONV, width), const),
                  pl.BlockSpec((tm, LANES), row),
                  pl.BlockSpec((1, LANES), const), pl.BlockSpec((1, LANES), const)],
        out_specs=[pl.BlockSpec((tm, GDN_KEY_WIDTH), row), pl.BlockSpec((tm, GDN_KEY_WIDTH), row),
                   pl.BlockSpec((tm, GDN_VAL_WIDTH), row), pl.BlockSpec((tm, LANES), row)],
        out_shape=[jax.ShapeDtypeStruct((t, GDN_KEY_WIDTH), F32), jax.ShapeDtypeStruct((t, GDN_KEY_WIDTH), F32),
                   jax.ShapeDtypeStruct((t, GDN_VAL_WIDTH), F32), jax.ShapeDtypeStruct((t, LANES), F32)],
        compiler_params=_cparams(("parallel",), 48),
        name="gdn_prep",
    )(p_gdn, p_gdn, conv_w, small, alog, dtb)


def _row_pad(x):
    return jnp.concatenate([x, jnp.zeros_like(x)], axis=0)


def _gdn_intra_kernel(q_ref, k_ref, v_ref, gb_ref, u_ref, w_ref, qe_ref, ke_ref, at_ref, eg_ref, *, heads):
    c = GDN_CHUNK
    gb = gb_ref[...]
    row = lax.broadcasted_iota(jnp.int32, gb.shape, 0)
    gcum = gb
    shift = 1
    while shift < c:
        gcum = gcum + jnp.where(row >= shift, pltpu.roll(gcum, shift=shift, axis=0), 0.0)
        shift *= 2
    lane = lax.broadcasted_iota(jnp.int32, gb.shape, 1)
    ri = lax.broadcasted_iota(jnp.int32, (c, LANES), 0)
    ci = lax.broadcasted_iota(jnp.int32, (c, LANES), 1)
    tri = ri >= ci
    strict = ri > ci
    eye = ri == ci
    eyef = eye.astype(F32)
    lane8 = lax.broadcasted_iota(jnp.int32, (SUBLANES, LANES), 1)
    eg_all = jnp.zeros((SUBLANES, LANES), F32)
    hs = range(heads)
    sls = [slice(h * GDN_DK, (h + 1) * GDN_DK) for h in hs]

    gcol = [jnp.sum(jnp.where(lane == _SM_A + h, gcum, 0.0), axis=-1, keepdims=True) for h in hs]
    beta = [jnp.sum(jnp.where(lane == _SM_BETA + h, gb, 0.0), axis=-1, keepdims=True) for h in hs]
    decay = []
    for h in hs:
        gmat = jnp.broadcast_to(gcol[h], (c, LANES))
        grow = jnp.sum(jnp.where(eye, gmat, 0.0), axis=0, keepdims=True)
        decay.append(jnp.where(tri, jnp.exp(jnp.where(tri, gmat - grow, 0.0)), 0.0))
    glast = [gcol[h][c - 1:c, :] for h in hs]
    egc = [jnp.exp(gcol[h]) for h in hs]
    q = [q_ref[:, sls[h]] for h in hs]
    k = [k_ref[:, sls[h]] for h in hs]
    k16 = [k[h].astype(BF16) for h in hs]
    qk = [_dot_nt(jnp.concatenate([q[h].astype(BF16), k16[h]], axis=0), _row_pad(k16[h])) for h in hs]
    for h in hs:
        at_ref[:, sls[h]] = (qk[h][:c] * decay[h]).astype(BF16)

    pw = [-jnp.where(strict, beta[h] * qk[h][c:] * decay[h], 0.0) for h in hs]
    inv = [eyef + pw[h] for h in hs]
    pw16 = [pw[h].astype(BF16) for h in hs]
    pw = [_dot(pw16[h], _row_pad(pw16[h])) for h in hs]
    span = 2
    while span < c:
        pw16 = [pw[h].astype(BF16) for h in hs]
        if 2 * span < c:
            both = [_dot(jnp.concatenate([pw16[h], inv[h].astype(BF16)], axis=0), _row_pad(pw16[h])) for h in hs]
            pw = [both[h][:c] for h in hs]
            inv = [inv[h] + both[h][c:] for h in hs]
        else:
            inv = [inv[h] + _dot(inv[h].astype(BF16), _row_pad(pw16[h])) for h in hs]
        span *= 2

    uw = []
    for h in hs:
        rhs = jnp.concatenate([(v_ref[:, sls[h]] * beta[h]).astype(BF16),
                               (k[h] * (beta[h] * egc[h])).astype(BF16)], axis=1)
        uw.append(_dot(inv[h].astype(BF16), _row_pad(rhs)))
    for h in hs:
        u_ref[:, sls[h]] = uw[h][:, :GDN_DV]
        w_ref[:, sls[h]] = uw[h][:, GDN_DV:].astype(BF16)
        qe_ref[:, sls[h]] = (q[h] * egc[h]).astype(BF16)
        ke_ref[:, sls[h]] = (k[h] * jnp.exp(glast[h] - gcol[h])).astype(BF16)
        eg_all = jnp.where(lane8 == h, jnp.exp(glast[h]), eg_all)
    eg_ref[0] = eg_all


def _gdn_scan_kernel(eg_ref, u_ref, w_ref, qe_ref, ke_ref, at_ref, z_ref, nw_ref, o_ref, state, *, heads):
    n = pl.program_id(0)
    c = GDN_CHUNK

    @pl.when(n == 0)
    def _():
        state[...] = jnp.zeros(state.shape, F32)

    nw = nw_ref[...]
    hs = range(heads)
    sls = [slice(h * GDN_DK, (h + 1) * GDN_DK) for h in hs]
    s_prev = [state[h] for h in hs]
    ws_qs = [_dot(jnp.concatenate([w_ref[:, sls[h]], qe_ref[:, sls[h]]], axis=0), s_prev[h].astype(BF16))
             for h in hs]
    v_new16 = [(u_ref[:, sls[h]] - ws_qs[h][:c]).astype(BF16) for h in hs]
    o = [ws_qs[h][c:] + _dot(at_ref[:, h * GDN_DK:h * GDN_DK + c], v_new16[h]) for h in hs]
    for h in hs:
        state[h] = s_prev[h] * eg_ref[n * heads + h] + _dot_tn(ke_ref[:, sls[h]], v_new16[h])
    for h in hs:
        o_ref[:, sls[h]] = (_head_rms(o[h], nw) * _silu(z_ref[:, sls[h]])).astype(o_ref.dtype)


def _gdn_chunks(q, k, v, gb, z_b, norm_w):
    t = q.shape[0]
    c = GDN_CHUNK
    n_chunk = t // c
    heads = GDN_HEADS
    wide = pl.BlockSpec((c, GDN_VAL_WIDTH), lambda n: (n, 0))
    u, w, qe, ke, attn, eg = pl.pallas_call(
        functools.partial(_gdn_intra_kernel, heads=heads),
        grid=(n_chunk,),
        in_specs=[wide, wide, wide, pl.BlockSpec((c, LANES), lambda n: (n, 0))],
        out_specs=[wide, wide, wide, wide, wide, pl.BlockSpec((1, SUBLANES, LANES), lambda n: (n, 0, 0))],
        out_shape=[jax.ShapeDtypeStruct((t, GDN_VAL_WIDTH), F32)]
        + [jax.ShapeDtypeStruct((t, GDN_VAL_WIDTH), BF16)] * 4
        + [jax.ShapeDtypeStruct((n_chunk, SUBLANES, LANES), F32)],
        compiler_params=_cparams(("parallel",), 40),
        name="gdn_intra",
    )(q, k, v, gb)
    eg_flat = eg[:, 0, :heads].reshape(n_chunk * heads)
    return pl.pallas_call(
        functools.partial(_gdn_scan_kernel, heads=heads),
        grid=(n_chunk,),
        in_specs=[pl.BlockSpec(memory_space=pltpu.SMEM), wide, wide, wide, wide, wide, wide,
                  pl.BlockSpec((1, GDN_DV), lambda n: (0, 0))],
        out_specs=wide,
        out_shape=jax.ShapeDtypeStruct((t, GDN_VAL_WIDTH), BF16),
        scratch_shapes=[pltpu.VMEM((heads, GDN_DK, GDN_DV), F32)],
        compiler_params=_cparams(("arbitrary",), 40),
        name="gdn_scan",
    )(eg_flat, u, w, qe, ke, attn, z_b, norm_w.reshape(1, GDN_DV))


def _short_conv_kernel(bg_ref, cg_ref, x_ref, z_ref, cgh_ref, xh_ref, cw_ref, o_ref):
    i = pl.program_id(0)
    cur = cg_ref[...] * x_ref[...]
    halo = jnp.where(i > 0, cgh_ref[...] * xh_ref[...], 0.0)
    y = _causal_conv(cur, halo, cw_ref, SC_CONV)
    o_ref[...] = (bg_ref[...] * y * _silu(z_ref[...])).astype(o_ref.dtype)


def _short_conv(p_sc, conv_w, tm=256):
    t = p_sc.shape[0]
    tm = min(tm, t)
    hb = tm // SUBLANES
    main = lambda c: pl.BlockSpec((tm, SC_WIDTH), functools.partial(lambda i, c: (i, c), c=c))
    halo = lambda c: pl.BlockSpec((SUBLANES, SC_WIDTH),
                                  functools.partial(lambda i, c: (jnp.maximum(i * hb - 1, 0), c), c=c))
    return pl.pallas_call(
        _short_conv_kernel,
        grid=(t // tm,),
        in_specs=[main(0), main(1), main(2), main(3), halo(1), halo(2),
                  pl.BlockSpec((SC_CONV, SC_WIDTH), lambda i: (0, 0))],
        out_specs=pl.BlockSpec((tm, SC_WIDTH), lambda i: (i, 0)),
        out_shape=jax.ShapeDtypeStruct((t, SC_WIDTH), BF16),
        compiler_params=_cparams(("parallel",), 40),
        name="short_conv",
    )(p_sc, p_sc, p_sc, p_sc, p_sc, p_sc, conv_w)


_IN_GROUPS = (("nsa", _OFF_QA, _OFF_GATE), ("za", _OFF_ZA, _OFF_QB), ("gdn", _OFF_QB, _OFF_AB),
              ("zb", _OFF_ZB, _OFF_SC), ("sc", _OFF_SC, _OFF_MERGE), ("merge", _OFF_MERGE, None))


def _stage_w_in(w_in):
    depth, d, _ = w_in.shape
    w_t = jnp.transpose(w_in, (0, 2, 1)).astype(BF16)
    small_pad = LANES - 2 * GDN_HEADS - 3 * NSA_HEADS
    w_small_t = jnp.concatenate([w_t[:, _OFF_AB:_OFF_ZB], w_t[:, _OFF_GATE:_OFF_ZA],
                                 jnp.zeros((depth, small_pad, d), BF16)], axis=1)
    return w_t, w_small_t


def _layer(x, layer, norm_w, w_t, w_small_t, nsa_qk_norm, cmp_pos, cmp_w1, cmp_w2, gdn_conv_w, gdn_a_log,
           gdn_dt_bias, gdn_norm_w, sc_conv_w, wb_nsa, wb_gdn, wb_sc, w_out, slopes):
    hn = _rmsnorm(x, norm_w)
    proj = {}
    for name, a, b in _IN_GROUPS:
        n = (w_t.shape[1] if b is None else b) - a
        proj[name] = _matmul_nt(hn, w_t, layer, a, n, F32, 1024, 512, "in_proj_" + name)
    small = _matmul_nt(hn, w_small_t, layer, 0, LANES, F32, 1024, LANES, "in_proj_small")
    p_nsa, z_a, p_gdn, z_b, p_sc, p_mg = (proj[n] for n, _, _ in _IN_GROUPS)

    q, kv_c, k_s, v_s, k_w, v_w = _nsa_prep(p_nsa, nsa_qk_norm)
    kv_cmp = _compress(kv_c, cmp_pos, cmp_w1, cmp_w2, nsa_qk_norm[1])
    o_cmp, sel, blk_any = _cmp_attention(q, kv_cmp, slopes)
    o_slc = _sel_attention(q, k_s, v_s, sel, blk_any, slopes)
    o_win = _win_attention(q, k_w, v_w, slopes)
    o_a = _nsa_combine(small, o_cmp, o_slc, o_win, z_a)

    q_b, k_b, v_b, gb = _gdn_prep(p_gdn, small, gdn_conv_w, gdn_a_log, gdn_dt_bias)
    o_b = _gdn_chunks(q_b, k_b, v_b, gb, z_b, gdn_norm_w)

    o_c = _short_conv(p_sc, sc_conv_w)

    merged = _merge(o_a, o_b, o_c, wb_nsa, wb_gdn, wb_sc, layer, p_mg)
    return _matmul_residual(merged, w_out, layer, x, 1024, 512)


def kernel(x, norm_w, w_in, nsa_qk_norm, cmp_pos, cmp_w1, cmp_w2, gdn_conv_w, gdn_a_log, gdn_dt_bias,
           gdn_norm_w, sc_conv_w, w_branch_nsa, w_branch_gdn, w_branch_sc, w_out):
    b, t, d = x.shape
    depth = norm_w.shape[0]
    heads = jnp.arange(1, NSA_HEADS + 1, dtype=F32)
    slopes = jnp.exp2(-8.0 * heads / NSA_HEADS)
    w_t, w_small_t = _stage_w_in(w_in)
    wb_nsa, wb_gdn, wb_sc, w_out16 = (w.astype(BF16) for w in (w_branch_nsa, w_branch_gdn, w_branch_sc, w_out))
    outs = []
    for bi in range(b):
        xb = x[bi]
        for l in range(depth):
            xb = _layer(xb, l, norm_w[l], w_t, w_small_t, nsa_qk_norm[l], cmp_pos[l], cmp_w1[l], cmp_w2[l],
                        gdn_conv_w[l], gdn_a_log[l], gdn_dt_bias[l], gdn_norm_w[l], sc_conv_w[l],
                        wb_nsa, wb_gdn, wb_sc, w_out16, slopes)
        outs.append(xb)
    return jnp.stack(outs, axis=0)
```

```python
import functools
import math

import jax
import jax.numpy as jnp
from jax import lax
from jax.experimental import pallas as pl
from jax.experimental.pallas import tpu as pltpu

F32 = jnp.float32
BF16 = jnp.bfloat16

HEAD_DIM = 128
NSA_HEADS = 16
NSA_KV_HEADS = 4
NSA_GROUP = NSA_HEADS // NSA_KV_HEADS
NSA_WIDTH = NSA_HEADS * HEAD_DIM
NSA_KV_WIDTH = NSA_KV_HEADS * HEAD_DIM
CMP_BLOCK = 32
CMP_STRIDE = 16
CMP_HIDDEN = 256
SEL_BLOCK = 64
SEL_TOPK = 16
WINDOW = 512
GDN_HEADS = 16
GDN_DK = 128
GDN_DV = 128
GDN_KEY_WIDTH = GDN_HEADS * GDN_DK
GDN_VAL_WIDTH = GDN_HEADS * GDN_DV
GDN_CONV = 4
GDN_CHUNK = 64
SC_WIDTH = 2048
SC_CONV = 3
N_BRANCH = 3
NORM_EPS = 1e-6
NEG_INF = -1e30
MASK_BIAS = -2e30
FORCE_SCORE = 1e6

LANES = 128
SUBLANES = 8
BF16_SUBLANES = 16
N_BLK_LANES = 128
MIB = 1024 * 1024

_OFF_QA = 0
_OFF_KVC = _OFF_QA + NSA_WIDTH
_OFF_KVS = _OFF_KVC + 2 * NSA_KV_WIDTH
_OFF_KVW = _OFF_KVS + 2 * NSA_KV_WIDTH
_OFF_GATE = _OFF_KVW + 2 * NSA_KV_WIDTH
_OFF_ZA = _OFF_GATE + 3 * NSA_HEADS
_OFF_QB = _OFF_ZA + NSA_WIDTH
_OFF_AB = _OFF_QB + 2 * GDN_KEY_WIDTH + GDN_VAL_WIDTH
_OFF_BETA = _OFF_AB + GDN_HEADS
_OFF_ZB = _OFF_BETA + GDN_HEADS
_OFF_SC = _OFF_ZB + GDN_VAL_WIDTH
_OFF_MERGE = _OFF_SC + 4 * SC_WIDTH

_SM_A = 0
_SM_BETA = GDN_HEADS
_SM_GATE = 2 * GDN_HEADS


def _cparams(sem, vmem_mib):
    return pltpu.CompilerParams(dimension_semantics=sem, vmem_limit_bytes=vmem_mib * MIB)


def _sigmoid(x):
    return jax.nn.sigmoid(x)


def _silu(x):
    return x * jax.nn.sigmoid(x)


def _dot(a, b):
    return jnp.dot(a, b, preferred_element_type=F32)


def _dot_nt(a, b):
    return lax.dot_general(a, b, (((1,), (1,)), ((), ())), preferred_element_type=F32)


def _dot_tn(a, b):
    return lax.dot_general(a, b, (((0,), (0,)), ((), ())), preferred_element_type=F32)


def _head_rms(x, w):
    return x * lax.rsqrt(jnp.mean(x * x, axis=-1, keepdims=True) + NORM_EPS) * w


def _rmsnorm_kernel(x_ref, w_ref, o_ref):
    x = x_ref[...]
    y = x * lax.rsqrt(jnp.mean(x * x, axis=-1, keepdims=True) + NORM_EPS)
    o_ref[...] = (y * w_ref[...]).astype(o_ref.dtype)


def _rmsnorm(x, w, tm=256):
    t, d = x.shape
    return pl.pallas_call(
        _rmsnorm_kernel,
        grid=(t // tm,),
        in_specs=[pl.BlockSpec((tm, d), lambda i: (i, 0)), pl.BlockSpec((1, d), lambda i: (0, 0))],
        out_specs=pl.BlockSpec((tm, d), lambda i: (i, 0)),
        out_shape=jax.ShapeDtypeStruct((t, d), BF16),
        compiler_params=_cparams(("parallel",), 32),
        name="rmsnorm",
    )(x, w.reshape(1, d))


def _mm_kernel(a_ref, b_ref, o_ref):
    o_ref[...] = _dot(a_ref[...], b_ref[...]).astype(o_ref.dtype)


def _mm_nt_kernel(a_ref, b_ref, o_ref):
    o_ref[...] = _dot_nt(a_ref[...], b_ref[...]).astype(o_ref.dtype)


def _matmul_nt(a, b_t, layer, row_off, n, out_dtype, tm, tn, name):
    m, k = a.shape
    tm = min(tm, m)
    tn = min(tn, n)
    depth, rows, _ = b_t.shape
    first_row = layer * rows + row_off
    assert first_row % BF16_SUBLANES == 0 and n % tn == 0
    b_t = b_t.reshape(depth * rows, k)
    return pl.pallas_call(
        _mm_nt_kernel,
        grid=(m // tm, n // tn),
        in_specs=[pl.BlockSpec((tm, k), lambda i, j: (i, 0)),
                  pl.BlockSpec((pl.Element(tn), pl.Element(k)),
                               lambda i, j: (pl.multiple_of(first_row + j * tn, BF16_SUBLANES), 0))],
        out_specs=pl.BlockSpec((tm, tn), lambda i, j: (i, j)),
        out_shape=jax.ShapeDtypeStruct((m, n), out_dtype),
        compiler_params=_cparams(("parallel", "arbitrary"), 48),
        name=name,
    )(a, b_t)


def _mm_res_kernel(a_ref, b_ref, r_ref, o_ref):
    o_ref[...] = r_ref[...] + _dot(a_ref[...], b_ref[...])


def _matmul_residual(a, b, layer, r, tm, tn):
    m, k = a.shape
    n = b.shape[2]
    tm = min(tm, m)
    return pl.pallas_call(
        _mm_res_kernel,
        grid=(m // tm, n // tn),
        in_specs=[pl.BlockSpec((tm, k), lambda i, j: (i, 0)),
                  pl.BlockSpec((None, k, tn), lambda i, j: (layer, 0, j)),
                  pl.BlockSpec((tm, tn), lambda i, j: (i, j))],
        out_specs=pl.BlockSpec((tm, tn), lambda i, j: (i, j)),
        out_shape=jax.ShapeDtypeStruct((m, n), F32),
        compiler_params=_cparams(("parallel", "arbitrary"), 48),
        name="out_proj_residual",
    )(a, b, r)


def _merge_kernel(oa_ref, ob_ref, oc_ref, wa_ref, wb_ref, wc_ref, ga_ref, gb_ref, gc_ref, o_ref):
    acc = _sigmoid(ga_ref[...]) * _dot(oa_ref[...], wa_ref[...])
    acc = acc + _sigmoid(gb_ref[...]) * _dot(ob_ref[...], wb_ref[...])
    acc = acc + _sigmoid(gc_ref[...]) * _dot(oc_ref[...], wc_ref[...])
    o_ref[...] = acc.astype(o_ref.dtype)


def _merge(o_a, o_b, o_c, wa, wb, wc, layer, gates, tm=1024, tn=256):
    m, k = o_a.shape
    n = wa.shape[2]
    tm = min(tm, m)
    nb = n // tn
    a_spec = pl.BlockSpec((tm, k), lambda i, j: (i, 0))
    w_spec = pl.BlockSpec((None, k, tn), lambda i, j: (layer, 0, j))
    g_specs = [pl.BlockSpec((tm, tn), functools.partial(lambda i, j, br: (i, br * nb + j), br=br))
               for br in range(N_BRANCH)]
    return pl.pallas_call(
        _merge_kernel,
        grid=(m // tm, nb),
        in_specs=[a_spec, a_spec, a_spec, w_spec, w_spec, w_spec] + g_specs,
        out_specs=pl.BlockSpec((tm, tn), lambda i, j: (i, j)),
        out_shape=jax.ShapeDtypeStruct((m, n), BF16),
        compiler_params=_cparams(("parallel", "arbitrary"), 48),
        name="branch_merge",
    )(o_a, o_b, o_c, wa, wb, wc, gates, gates, gates)


def _nsa_prep_kernel(p_ref, nw_ref, q_ref, kvc_ref, ks_ref, vs_ref, kw_ref, vw_ref):
    nw = nw_ref[...]
    for h in range(NSA_HEADS):
        sl = slice(h * HEAD_DIM, (h + 1) * HEAD_DIM)
        q_ref[:, sl] = (_head_rms(p_ref[:, sl], nw[0:1]) * (HEAD_DIM ** -0.5)).astype(BF16)
    kvc_ref[...] = p_ref[:, _OFF_KVC:_OFF_KVS]
    for g in range(NSA_KV_HEADS):
        sl = slice(g * HEAD_DIM, (g + 1) * HEAD_DIM)
        ks = p_ref[:, _OFF_KVS + g * HEAD_DIM:_OFF_KVS + (g + 1) * HEAD_DIM]
        ks_ref[:, sl] = _head_rms(ks, nw[2:3]).astype(BF16)
        vs_ref[:, sl] = p_ref[:, _OFF_KVS + NSA_KV_WIDTH + g * HEAD_DIM:
                              _OFF_KVS + NSA_KV_WIDTH + (g + 1) * HEAD_DIM].astype(BF16)
        kw = p_ref[:, _OFF_KVW + g * HEAD_DIM:_OFF_KVW + (g + 1) * HEAD_DIM]
        kw_ref[:, sl] = _head_rms(kw, nw[3:4]).astype(BF16)
        vw_ref[:, sl] = p_ref[:, _OFF_KVW + NSA_KV_WIDTH + g * HEAD_DIM:
                              _OFF_KVW + NSA_KV_WIDTH + (g + 1) * HEAD_DIM].astype(BF16)


def _nsa_prep(p_nsa, qk_norm, tm=256):
    t, width = p_nsa.shape
    row = lambda i: (i, 0)
    return pl.pallas_call(
        _nsa_prep_kernel,
        grid=(t // tm,),
        in_specs=[pl.BlockSpec((tm, width), row), pl.BlockSpec((4, HEAD_DIM), lambda i: (0, 0))],
        out_specs=[pl.BlockSpec((tm, NSA_WIDTH), row)] + [pl.BlockSpec((tm, 2 * NSA_KV_WIDTH), row)]
        + [pl.BlockSpec((tm, NSA_KV_WIDTH), row)] * 4,
        out_shape=[jax.ShapeDtypeStruct((t, NSA_WIDTH), BF16), jax.ShapeDtypeStruct((t, 2 * NSA_KV_WIDTH), F32)]
        + [jax.ShapeDtypeStruct((t, NSA_KV_WIDTH), BF16)] * 4,
        compiler_params=_cparams(("parallel",), 40),
        name="nsa_prep",
    )(p_nsa, qk_norm)


def _compress_kernel(x_ref, pos_ref, w1lo_ref, w1hi_ref, w2_ref, nw_ref, o_ref, lo_acc, hi_acc, *, n_chunk):
    kv = pl.program_id(0)
    l = pl.program_id(1)

    @pl.when(l == 0)
    def _():
        lo_acc[...] = jnp.zeros(lo_acc.shape, F32)
        hi_acc[...] = jnp.zeros(hi_acc.shape, F32)

    pos_lo = pos_ref[0, pl.ds(l, 1), :]
    pos_hi = pos_ref[0, pl.ds(l + CMP_STRIDE, 1), :]
    w_lo = w1lo_ref[0, 0]
    w_hi = w1hi_ref[0, 0]
    for g in range(NSA_KV_HEADS):
        x = x_ref[:, g * HEAD_DIM:(g + 1) * HEAD_DIM]
        lo_acc[g] += _dot((x + pos_lo).astype(BF16), w_lo)
        hi_acc[g] += _dot((x + pos_hi).astype(BF16), w_hi)

    @pl.when(l == CMP_STRIDE - 1)
    def _():
        for g in range(NSA_KV_HEADS):
            hidden = _silu(lo_acc[g] + pltpu.roll(hi_acc[g], shift=n_chunk - 1, axis=0))
            out = _dot(hidden.astype(BF16), w2_ref[0])
            normed = _head_rms(out, nw_ref[...])
            o_ref[0, g] = jnp.where(kv == 0, normed, out).astype(BF16)


def _compress(kv_c, cmp_pos, cmp_w1, cmp_w2, k_norm_w):
    t = kv_c.shape[0]
    n_chunk = t // CMP_STRIDE
    x = kv_c.reshape(n_chunk, CMP_STRIDE * 2 * NSA_KV_WIDTH)
    w1 = cmp_w1.astype(BF16)
    w2 = cmp_w2.astype(BF16)
    return pl.pallas_call(
        functools.partial(_compress_kernel, n_chunk=n_chunk),
        grid=(2, CMP_STRIDE),
        in_specs=[pl.BlockSpec((n_chunk, NSA_KV_WIDTH), lambda a, l: (0, 2 * l + a)),
                  pl.BlockSpec((1, CMP_BLOCK, HEAD_DIM), lambda a, l: (a, 0, 0)),
                  pl.BlockSpec((1, 1, HEAD_DIM, CMP_HIDDEN), lambda a, l: (a, l, 0, 0)),
                  pl.BlockSpec((1, 1, HEAD_DIM, CMP_HIDDEN), lambda a, l: (a, l + CMP_STRIDE, 0, 0)),
                  pl.BlockSpec((1, CMP_HIDDEN, HEAD_DIM), lambda a, l: (a, 0, 0)),
                  pl.BlockSpec((1, HEAD_DIM), lambda a, l: (0, 0))],
        out_specs=pl.BlockSpec((1, NSA_KV_HEADS, n_chunk, HEAD_DIM), lambda a, l: (a, 0, 0, 0)),
        out_shape=jax.ShapeDtypeStruct((2, NSA_KV_HEADS, n_chunk, HEAD_DIM), BF16),
        scratch_shapes=[pltpu.VMEM((NSA_KV_HEADS, n_chunk, CMP_HIDDEN), F32),
                        pltpu.VMEM((NSA_KV_HEADS, n_chunk, CMP_HIDDEN), F32)],
        compiler_params=_cparams(("arbitrary", "arbitrary"), 40),
        name="nsa_compress",
    )(x, cmp_pos, w1, w1, w2, k_norm_w.reshape(1, HEAD_DIM))


def _cmp_attn_kernel(slopes_ref, q_ref, k_ref, v_ref, cov_ref, o_ref, sel_ref, any_ref, *, tq, n_chunk):
    g = pl.program_id(0)
    qi = pl.program_id(1)
    k = k_ref[0, 0]
    v_t = v_ref[0, 0].T
    t0 = qi * tq
    t_pos = t0 + lax.broadcasted_iota(jnp.int32, (n_chunk, tq), 1)
    k_end = lax.broadcasted_iota(jnp.int32, (n_chunk, tq), 0) * CMP_STRIDE + (CMP_BLOCK - 1)
    mask_bias = jnp.where(t_pos >= k_end, 0.0, MASK_BIAS)
    k_rel = (k_end - t0).astype(F32)
    psum = jnp.zeros((n_chunk, tq), F32)
    sls = [slice(z * HEAD_DIM, (z + 1) * HEAD_DIM) for z in range(NSA_GROUP)]
    qk_next = _dot_nt(k, q_ref[:, sls[0]])
    pending = None
    for z in range(NSA_GROUP):
        qk = qk_next
        if z + 1 < NSA_GROUP:
            qk_next = _dot_nt(k, q_ref[:, sls[z + 1]])
        s = qk + (mask_bias + slopes_ref[g * NSA_GROUP + z] * k_rel)
        m = jnp.maximum(jnp.max(s, axis=0, keepdims=True), NEG_INF)
        e = jnp.exp(s - m)
        p = e * (1.0 / jnp.maximum(jnp.sum(e, axis=0, keepdims=True), 1e-30))
        psum = psum + p
        if pending is not None:
            o_ref[:, sls[pending[0]]] = _dot(v_t, pending[1]).T
        pending = (z, p.astype(BF16))
    o_ref[:, sls[pending[0]]] = _dot(v_t, pending[1]).T

    p_hi = psum.astype(BF16)
    p_lo = (psum - p_hi.astype(F32)).astype(BF16)
    cov_t = cov_ref[...]
    imp = _dot(cov_t, p_hi) + _dot(cov_t, p_lo)

    j = lax.broadcasted_iota(jnp.int32, (N_BLK_LANES, tq), 0)
    cur = (t0 + lax.broadcasted_iota(jnp.int32, (N_BLK_LANES, tq), 1)) // SEL_BLOCK
    valid = j <= cur
    forced = (j == 0) | (j == cur) | (j == cur - 1)
    val = jnp.where(valid, jnp.where(forced, FORCE_SCORE, imp), -1.0)
    sel = jnp.zeros((N_BLK_LANES, tq), F32)
    jf = j.astype(F32)
    for _ in range(SEL_TOPK):
        m = jnp.max(val, axis=0, keepdims=True)
        idx = jnp.min(jnp.where(val == m, jf, float(N_BLK_LANES)), axis=0, keepdims=True)
        hit = jf == idx
        sel = jnp.where(hit & (m > -0.5), 1.0, sel)
        val = jnp.where(hit, -2.0, val)
    sel_q = sel.T
    sel_ref[0] = sel_q.astype(BF16)
    any_ref[0, 0] = jnp.broadcast_to(jnp.max(sel_q, axis=0, keepdims=True), (SUBLANES, N_BLK_LANES))


def _cmp_attention(q, kv_cmp, slopes, tq=256):
    t = q.shape[0]
    n_chunk = kv_cmp.shape[2]
    n_cmp = n_chunk - CMP_BLOCK // CMP_STRIDE + 1
    n_blk = t // SEL_BLOCK
    assert n_blk <= N_BLK_LANES
    tq = min(tq, t)
    cs = jnp.arange(n_chunk)[:, None] * CMP_STRIDE
    bs = jnp.arange(N_BLK_LANES)[None, :] * SEL_BLOCK
    cover = ((cs <= bs + SEL_BLOCK - 1) & (cs + CMP_BLOCK - 1 >= bs)
             & (jnp.arange(n_chunk)[:, None] < n_cmp) & (jnp.arange(N_BLK_LANES)[None, :] < n_blk))
    cover = cover.astype(BF16).T
    return pl.pallas_call(
        functools.partial(_cmp_attn_kernel, tq=tq, n_chunk=n_chunk),
        grid=(NSA_KV_HEADS, t // tq),
        in_specs=[pl.BlockSpec(memory_space=pltpu.SMEM),
                  pl.BlockSpec((tq, NSA_GROUP * HEAD_DIM), lambda g, i: (i, g)),
                  pl.BlockSpec((1, 1, n_chunk, HEAD_DIM), lambda g, i: (0, g, 0, 0)),
                  pl.BlockSpec((1, 1, n_chunk, HEAD_DIM), lambda g, i: (1, g, 0, 0)),
                  pl.BlockSpec((N_BLK_LANES, n_chunk), lambda g, i: (0, 0))],
        out_specs=[pl.BlockSpec((tq, NSA_GROUP * HEAD_DIM), lambda g, i: (i, g)),
                   pl.BlockSpec((1, tq, N_BLK_LANES), lambda g, i: (g, i, 0)),
                   pl.BlockSpec((1, 1, SUBLANES, N_BLK_LANES), lambda g, i: (g, i, 0, 0))],
        out_shape=[jax.ShapeDtypeStruct((t, NSA_WIDTH), F32),
                   jax.ShapeDtypeStruct((NSA_KV_HEADS, t, N_BLK_LANES), BF16),
                   jax.ShapeDtypeStruct((NSA_KV_HEADS, t // tq, SUBLANES, N_BLK_LANES), F32)],
        compiler_params=_cparams(("parallel", "parallel"), 40),
        name="nsa_cmp_attn",
    )(slopes, q, kv_cmp, kv_cmp, cover)


def _sel_attn_kernel(cnt_ref, kjs_ref, slopes_ref, q_ref, k_ref, v_ref, eb_ref, sel_ref, o_ref,
                     qa_sc, m_sc, l_sc, acc_sc, *, tq, tk, max_tiles):
    g = pl.program_id(0)
    qi = pl.program_id(1)
    item = g * pl.num_programs(1) + qi
    t0 = qi * tq

    m_sc[...] = jnp.full(m_sc.shape, NEG_INF, F32)
    l_sc[...] = jnp.zeros(l_sc.shape, F32)
    acc_sc[...] = jnp.zeros(acc_sc.shape, F32)
    unselected = sel_ref[0] - 1.0
    for z in range(NSA_GROUP):
        qa_sc[z] = jnp.concatenate([q_ref[:, z * HEAD_DIM:(z + 1) * HEAD_DIM], unselected], axis=1)

    def tile(kj, causal):
        k0 = kj * tk
        rows = pl.ds(pl.multiple_of(k0, tk), tk)
        k_aug = jnp.concatenate([k_ref[rows, :], eb_ref[kj]], axis=1)
        v_t = v_ref[rows, :].T
        row = lax.broadcasted_iota(jnp.int32, (tk, tq), 0)
        k_rel = (k0 - t0 + row).astype(F32)
        if causal:
            col = lax.broadcasted_iota(jnp.int32, (tk, tq), 1)
            causal_bias = jnp.where(t0 + col >= k0 + row, 0.0, MASK_BIAS)
        qk_next = _dot_nt(k_aug, qa_sc[0])
        pending = None
        for z in range(NSA_GROUP):
            qk = qk_next
            if z + 1 < NSA_GROUP:
                qk_next = _dot_nt(k_aug, qa_sc[z + 1])
            s = qk + slopes_ref[g * NSA_GROUP + z] * k_rel
            if causal:
                s = s + causal_bias
            m_prev = m_sc[z]
            m_new = jnp.maximum(m_prev, jnp.max(s, axis=0, keepdims=True))
            alpha = jnp.exp(m_prev - m_new)
            e = jnp.exp(s - m_new)
            l_sc[z] = alpha * l_sc[z] + jnp.sum(e, axis=0, keepdims=True)
            m_sc[z] = m_new
            if pending is not None:
                pz, palpha, pe = pending
                acc_sc[pz] = palpha * acc_sc[pz] + _dot(v_t, pe)
            pending = (z, alpha, e.astype(BF16))
        pz, palpha, pe = pending
        acc_sc[pz] = palpha * acc_sc[pz] + _dot(v_t, pe)

    def visit(i, carry):
        tile(kjs_ref[item * max_tiles + i], False)
        return carry

    lax.fori_loop(0, cnt_ref[item], visit, 0)
    tile(t0 // tk, True)

    for z in range(NSA_GROUP):
        out_t = acc_sc[z] * (1.0 / jnp.maximum(l_sc[z], 1e-30))
        o_ref[:, z * HEAD_DIM:(z + 1) * HEAD_DIM] = out_t.T


def _sel_work_list(blk_any, t, tq, tk):
    nq, nk, bpt = t // tq, t // tk, tk // SEL_BLOCK
    n_blk = t // SEL_BLOCK
    picked = blk_any[:, :, 0, :n_blk].reshape(NSA_KV_HEADS, nq, nk, bpt).max(axis=-1) > 0.5
    diagonal = (jnp.arange(nq) * tq) // tk
    before = picked & (jnp.arange(nk)[None, :] < diagonal[:, None])[None]
    count = before.sum(axis=-1).astype(jnp.int32)
    tiles = jnp.argsort(jnp.logical_not(before), axis=-1, stable=True).astype(jnp.int32)
    return count.reshape(-1), tiles.reshape(-1)


def _sel_attention(q, k_s, v_s, sel, blk_any, slopes, tq=256, tk=512):
    t = q.shape[0]
    tq = min(tq, t)
    tk = min(tk, t)
    assert tk % tq == 0
    nk = t // tk
    count, tiles = _sel_work_list(blk_any, t, tq, tk)

    blk_of_key = jnp.arange(t)[:, None] // SEL_BLOCK
    block_onehot = jnp.where(blk_of_key == jnp.arange(N_BLK_LANES)[None, :], -MASK_BIAS, 0.0)
    block_onehot = block_onehot.astype(BF16).reshape(nk, tk, N_BLK_LANES)

    smem = pl.BlockSpec(memory_space=pltpu.SMEM)
    return pl.pallas_call(
        functools.partial(_sel_attn_kernel, tq=tq, tk=tk, max_tiles=nk),
        grid=(NSA_KV_HEADS, t // tq),
        in_specs=[smem, smem, smem,
                  pl.BlockSpec((tq, NSA_GROUP * HEAD_DIM), lambda g, i: (i, g)),
                  pl.BlockSpec((t, HEAD_DIM), lambda g, i: (0, g)),
                  pl.BlockSpec((t, HEAD_DIM), lambda g, i: (0, g)),
                  pl.BlockSpec((nk, tk, N_BLK_LANES), lambda g, i: (0, 0, 0)),
                  pl.BlockSpec((1, tq, N_BLK_LANES), lambda g, i: (g, i, 0))],
        out_specs=pl.BlockSpec((tq, NSA_GROUP * HEAD_DIM), lambda g, i: (i, g)),
        out_shape=jax.ShapeDtypeStruct((t, NSA_WIDTH), F32),
        scratch_shapes=[pltpu.VMEM((NSA_GROUP, tq, 2 * HEAD_DIM), BF16),
                        pltpu.VMEM((NSA_GROUP, 1, tq), F32), pltpu.VMEM((NSA_GROUP, 1, tq), F32),
                        pltpu.VMEM((NSA_GROUP, HEAD_DIM, tq), F32)],
        compiler_params=_cparams(("arbitrary", "arbitrary"), 40),
        name="nsa_sel_attn",
    )(count, tiles, slopes, q, k_s, v_s, block_onehot, sel)


def _win_attn_kernel(slopes_ref, q_ref, *refs, tq, n_tile):
    k_refs = refs[:n_tile]
    v_refs = refs[n_tile:2 * n_tile]
    o_ref = refs[2 * n_tile]
    g = pl.program_id(0)
    qi = pl.program_id(1)
    span = n_tile * tq
    k = jnp.concatenate([r[...] for r in k_refs], axis=0)
    v = jnp.concatenate([r[...] for r in v_refs], axis=0)
    t_pos = qi * tq + lax.broadcasted_iota(jnp.int32, (tq, span), 0)
    k_pos = (qi - (n_tile - 1)) * tq + lax.broadcasted_iota(jnp.int32, (tq, span), 1)
    dist = t_pos - k_pos
    mask_bias = jnp.where((dist >= 0) & (dist < WINDOW) & (k_pos >= 0), 0.0, MASK_BIAS)
    k_rel = (lax.broadcasted_iota(jnp.int32, (1, span), 1) - (n_tile - 1) * tq).astype(F32)
    sls = [slice(z * HEAD_DIM, (z + 1) * HEAD_DIM) for z in range(NSA_GROUP)]
    qk_next = _dot_nt(q_ref[:, sls[0]], k)
    pending = None
    for z in range(NSA_GROUP):
        qk = qk_next
        if z + 1 < NSA_GROUP:
            qk_next = _dot_nt(q_ref[:, sls[z + 1]], k)
        s = qk + (mask_bias + slopes_ref[g * NSA_GROUP + z] * k_rel)
        m = jnp.maximum(jnp.max(s, axis=-1, keepdims=True), NEG_INF)
        e = jnp.exp(s - m)
        inv_l = 1.0 / jnp.maximum(jnp.sum(e, axis=-1, keepdims=True), 1e-30)
        if pending is not None:
            o_ref[:, sls[pending[0]]] = _dot(pending[1], v) * pending[2]
        pending = (z, e.astype(BF16), inv_l)
    o_ref[:, sls[pending[0]]] = _dot(pending[1], v) * pending[2]


def _win_attention(q, k_w, v_w, slopes, tq=256):
    t = q.shape[0]
    tq = min(tq, t)
    n_tile = -(-WINDOW // tq) + 1

    def kv_spec(c):
        return pl.BlockSpec((tq, HEAD_DIM), lambda g, i: (jnp.maximum(i - (n_tile - 1) + c, 0), g))

    return pl.pallas_call(
        functools.partial(_win_attn_kernel, tq=tq, n_tile=n_tile),
        grid=(NSA_KV_HEADS, t // tq),
        in_specs=[pl.BlockSpec(memory_space=pltpu.SMEM),
                  pl.BlockSpec((tq, NSA_GROUP * HEAD_DIM), lambda g, i: (i, g))]
        + [kv_spec(c) for c in range(n_tile)] * 2,
        out_specs=pl.BlockSpec((tq, NSA_GROUP * HEAD_DIM), lambda g, i: (i, g)),
        out_shape=jax.ShapeDtypeStruct((t, NSA_WIDTH), F32),
        compiler_params=_cparams(("parallel", "parallel"), 40),
        name="nsa_win_attn",
    )(slopes, q, *([k_w] * n_tile), *([v_w] * n_tile))


def _nsa_combine_kernel(sm_ref, ex_ref, oc_ref, os_ref, ow_ref, z_ref, o_ref):
    logits = sm_ref[...]
    hi = logits.astype(BF16)
    lo = (logits - hi.astype(F32)).astype(BF16)
    acc = None
    for br, branch_ref in enumerate((oc_ref, os_ref, ow_ref)):
        ex = ex_ref[br]
        gate = _sigmoid(_dot(hi, ex) + _dot(lo, ex))
        term = gate * branch_ref[...]
        acc = term if acc is None else acc + term
    o_ref[...] = (acc * _silu(z_ref[...])).astype(o_ref.dtype)


def _nsa_combine(small, o_cmp, o_slc, o_win, z_a, tm=256):
    t = small.shape[0]
    tm = min(tm, t)
    lane = jnp.arange(LANES)[None, :, None]
    br = jnp.arange(N_BRANCH)[:, None, None]
    head = (jnp.arange(NSA_WIDTH) // HEAD_DIM)[None, None, :]
    expand = (lane == _SM_GATE + head * N_BRANCH + br).astype(BF16)
    row = lambda i: (i, 0)
    wide = pl.BlockSpec((tm, NSA_WIDTH), row)
    return pl.pallas_call(
        _nsa_combine_kernel,
        grid=(t // tm,),
        in_specs=[pl.BlockSpec((tm, LANES), row),
                  pl.BlockSpec((N_BRANCH, LANES, NSA_WIDTH), lambda i: (0, 0, 0)),
                  wide, wide, wide, wide],
        out_specs=wide,
        out_shape=jax.ShapeDtypeStruct((t, NSA_WIDTH), BF16),
        compiler_params=_cparams(("parallel",), 40),
        name="nsa_combine",
    )(small, expand, o_cmp, o_slc, o_win, z_a)


def _shift_rows(cur, halo, s):
    rolled = pltpu.roll(cur, shift=s, axis=0)
    halo_rolled = pltpu.roll(halo, shift=s, axis=0)
    row = lax.broadcasted_iota(jnp.int32, halo.shape, 0)
    head = jnp.where(row < s, halo_rolled, rolled[0:SUBLANES])
    return jnp.concatenate([head, rolled[SUBLANES:]], axis=0)


def _causal_conv(cur, halo, w_ref, k):
    acc = None
    for j in range(k):
        s = k - 1 - j
        term = (cur if s == 0 else _shift_rows(cur, halo, s)) * w_ref[j:j + 1, :]
        acc = term if acc is None else acc + term
    return acc


def _gdn_prep_kernel(p_ref, halo_ref, cw_ref, sm_ref, alog_ref, dtb_ref, q_ref, k_ref, v_ref, gb_ref):
    i = pl.program_id(0)
    cur = p_ref[...]
    halo = jnp.where(i > 0, halo_ref[...], 0.0)
    y = _silu(_causal_conv(cur, halo, cw_ref, GDN_CONV))
    for h in range(GDN_HEADS):
        sl = slice(h * GDN_DK, (h + 1) * GDN_DK)
        qh = y[:, sl]
        q_ref[:, sl] = qh * lax.rsqrt(jnp.sum(qh * qh, axis=-1, keepdims=True) + NORM_EPS) * (GDN_DK ** -0.5)
        kh = y[:, GDN_KEY_WIDTH + h * GDN_DK:GDN_KEY_WIDTH + (h + 1) * GDN_DK]
        k_ref[:, sl] = kh * lax.rsqrt(jnp.sum(kh * kh, axis=-1, keepdims=True) + NORM_EPS)
    v_ref[...] = y[:, 2 * GDN_KEY_WIDTH:]
    sm = sm_ref[...]
    lane = lax.broadcasted_iota(jnp.int32, sm.shape, 1)
    gdecay = -jnp.exp(alog_ref[...]) * jax.nn.softplus(sm + dtb_ref[...])
    beta = _sigmoid(sm)
    gb_ref[...] = jnp.where(lane < _SM_BETA, gdecay, jnp.where(lane < _SM_GATE, beta, 0.0))


def _gdn_prep(p_gdn, small, conv_w, a_log, dt_bias, tm=256):
    t, width = p_gdn.shape
    tm = min(tm, t)
    hb = tm // SUBLANES
    row = lambda i: (i, 0)
    alog = jnp.zeros((1, LANES), F32).at[0, _SM_A:_SM_A + GDN_HEADS].set(a_log)
    dtb = jnp.zeros((1, LANES), F32).at[0, _SM_A:_SM_A + GDN_HEADS].set(dt_bias)
    const = lambda i: (0, 0)
    return pl.pallas_call(
        _gdn_prep_kernel,
        grid=(t // tm,),
        in_specs=[pl.BlockSpec((tm, width), row),
                  pl.BlockSpec((SUBLANES, width), lambda i: (jnp.maximum(i * hb - 1, 0), 0)),
                  pl.BlockSpec((GDN_CONV, width), const),
                  pl.BlockSpec((tm, LANES), row),
                  pl.BlockSpec((1, LANES), const), pl.BlockSpec((1, LANES), const)],
        out_specs=[pl.BlockSpec((tm, GDN_KEY_WIDTH), row), pl.BlockSpec((tm, GDN_KEY_WIDTH), row),
                   pl.BlockSpec((tm, GDN_VAL_WIDTH), row), pl.BlockSpec((tm, LANES), row)],
        out_shape=[jax.ShapeDtypeStruct((t, GDN_KEY_WIDTH), F32), jax.ShapeDtypeStruct((t, GDN_KEY_WIDTH), F32),
                   jax.ShapeDtypeStruct((t, GDN_VAL_WIDTH), F32), jax.ShapeDtypeStruct((t, LANES), F32)],
        compiler_params=_cparams(("parallel",), 48),
        name="gdn_prep",
    )(p_gdn, p_gdn, conv_w, small, alog, dtb)


def _row_pad(x):
    return jnp.concatenate([x, jnp.zeros_like(x)], axis=0)


def _gdn_intra_kernel(q_ref, k_ref, v_ref, gb_ref, u_ref, w_ref, qe_ref, ke_ref, at_ref, eg_ref, *, heads):
    c = GDN_CHUNK
    gb = gb_ref[...]
    row = lax.broadcasted_iota(jnp.int32, gb.shape, 0)
    gcum = gb
    shift = 1
    while shift < c:
        gcum = gcum + jnp.where(row >= shift, pltpu.roll(gcum, shift=shift, axis=0), 0.0)
        shift *= 2
    lane = lax.broadcasted_iota(jnp.int32, gb.shape, 1)
    ri = lax.broadcasted_iota(jnp.int32, (c, LANES), 0)
    ci = lax.broadcasted_iota(jnp.int32, (c, LANES), 1)
    tri = ri >= ci
    strict = ri > ci
    eye = ri == ci
    eyef = eye.astype(F32)
    lane8 = lax.broadcasted_iota(jnp.int32, (SUBLANES, LANES), 1)
    eg_all = jnp.zeros((SUBLANES, LANES), F32)
    hs = range(heads)
    sls = [slice(h * GDN_DK, (h + 1) * GDN_DK) for h in hs]

    gcol = [jnp.sum(jnp.where(lane == _SM_A + h, gcum, 0.0), axis=-1, keepdims=True) for h in hs]
    beta = [jnp.sum(jnp.where(lane == _SM_BETA + h, gb, 0.0), axis=-1, keepdims=True) for h in hs]
    decay = []
    for h in hs:
        gmat = jnp.broadcast_to(gcol[h], (c, LANES))
        grow = jnp.sum(jnp.where(eye, gmat, 0.0), axis=0, keepdims=True)
        decay.append(jnp.where(tri, jnp.exp(jnp.where(tri, gmat - grow, 0.0)), 0.0))
    glast = [gcol[h][c - 1:c, :] for h in hs]
    egc = [jnp.exp(gcol[h]) for h in hs]
    q = [q_ref[:, sls[h]] for h in hs]
    k = [k_ref[:, sls[h]] for h in hs]
    k16 = [k[h].astype(BF16) for h in hs]
    qk = [_dot_nt(jnp.concatenate([q[h].astype(BF16), k16[h]], axis=0), _row_pad(k16[h])) for h in hs]
    for h in hs:
        at_ref[:, sls[h]] = (qk[h][:c] * decay[h]).astype(BF16)

    pw = [-jnp.where(strict, beta[h] * qk[h][c:] * decay[h], 0.0) for h in hs]
    inv = [eyef + pw[h] for h in hs]
    pw16 = [pw[h].astype(BF16) for h in hs]
    pw = [_dot(pw16[h], _row_pad(pw16[h])) for h in hs]
    span = 2
    while span < c:
        pw16 = [pw[h].astype(BF16) for h in hs]
        if 2 * span < c:
            both = [_dot(jnp.concatenate([pw16[h], inv[h].astype(BF16)], axis=0), _row_pad(pw16[h])) for h in hs]
            pw = [both[h][:c] for h in hs]
            inv = [inv[h] + both[h][c:] for h in hs]
        else:
            inv = [inv[h] + _dot(inv[h].astype(BF16), _row_pad(pw16[h])) for h in hs]
        span *= 2

    uw = []
    for h in hs:
        rhs = jnp.concatenate([(v_ref[:, sls[h]] * beta[h]).astype(BF16),
                               (k[h] * (beta[h] * egc[h])).astype(BF16)], axis=1)
        uw.append(_dot(inv[h].astype(BF16), _row_pad(rhs)))
    for h in hs:
        u_ref[:, sls[h]] = uw[h][:, :GDN_DV]
        w_ref[:, sls[h]] = uw[h][:, GDN_DV:].astype(BF16)
        qe_ref[:, sls[h]] = (q[h] * egc[h]).astype(BF16)
        ke_ref[:, sls[h]] = (k[h] * jnp.exp(glast[h] - gcol[h])).astype(BF16)
        eg_all = jnp.where(lane8 == h, jnp.exp(glast[h]), eg_all)
    eg_ref[0] = eg_all


def _gdn_scan_kernel(eg_ref, u_ref, w_ref, qe_ref, ke_ref, at_ref, z_ref, nw_ref, o_ref, state, *, heads, cps):
    n = pl.program_id(0)
    c = GDN_CHUNK

    @pl.when(n == 0)
    def _():
        state[...] = jnp.zeros(state.shape, F32)

    nw = nw_ref[...]
    hs = range(heads)
    sls = [slice(h * GDN_DK, (h + 1) * GDN_DK) for h in hs]
    s_cur = [state[h] for h in hs]
    for cc in range(cps):
        rows = slice(cc * c, (cc + 1) * c)
        ws_qs = [_dot(jnp.concatenate([w_ref[rows, sls[h]], qe_ref[rows, sls[h]]], axis=0), s_cur[h].astype(BF16))
                 for h in hs]
        v_new16 = [(u_ref[rows, sls[h]] - ws_qs[h][:c]).astype(BF16) for h in hs]
        o = [ws_qs[h][c:] + _dot(at_ref[rows, h * GDN_DK:h * GDN_DK + c], v_new16[h]) for h in hs]
        s_cur = [s_cur[h] * eg_ref[(n * cps + cc) * heads + h] + _dot_tn(ke_ref[rows, sls[h]], v_new16[h])
                 for h in hs]
        for h in hs:
            o_ref[rows, sls[h]] = (_head_rms(o[h], nw) * _silu(z_ref[rows, sls[h]])).astype(o_ref.dtype)
    for h in hs:
        state[h] = s_cur[h]


def _gdn_chunks(q, k, v, gb, z_b, norm_w):
    t = q.shape[0]
    c = GDN_CHUNK
    n_chunk = t // c
    heads = GDN_HEADS
    wide = pl.BlockSpec((c, GDN_VAL_WIDTH), lambda n: (n, 0))
    u, w, qe, ke, attn, eg = pl.pallas_call(
        functools.partial(_gdn_intra_kernel, heads=heads),
        grid=(n_chunk,),
        in_specs=[wide, wide, wide, pl.BlockSpec((c, LANES), lambda n: (n, 0))],
        out_specs=[wide, wide, wide, wide, wide, pl.BlockSpec((1, SUBLANES, LANES), lambda n: (n, 0, 0))],
        out_shape=[jax.ShapeDtypeStruct((t, GDN_VAL_WIDTH), F32)]
        + [jax.ShapeDtypeStruct((t, GDN_VAL_WIDTH), BF16)] * 4
        + [jax.ShapeDtypeStruct((n_chunk, SUBLANES, LANES), F32)],
        compiler_params=_cparams(("parallel",), 40),
        name="gdn_intra",
    )(q, k, v, gb)
    eg_flat = eg[:, 0, :heads].reshape(n_chunk * heads)
    cps = 2 if n_chunk % 2 == 0 else 1
    wide = pl.BlockSpec((cps * c, GDN_VAL_WIDTH), lambda n: (n, 0))
    return pl.pallas_call(
        functools.partial(_gdn_scan_kernel, heads=heads, cps=cps),
        grid=(n_chunk // cps,),
        in_specs=[pl.BlockSpec(memory_space=pltpu.SMEM), wide, wide, wide, wide, wide, wide,
                  pl.BlockSpec((1, GDN_DV), lambda n: (0, 0))],
        out_specs=wide,
        out_shape=jax.ShapeDtypeStruct((t, GDN_VAL_WIDTH), BF16),
        scratch_shapes=[pltpu.VMEM((heads, GDN_DK, GDN_DV), F32)],
        compiler_params=_cparams(("arbitrary",), 40),
        name="gdn_scan",
    )(eg_flat, u, w, qe, ke, attn, z_b, norm_w.reshape(1, GDN_DV))


def _short_conv_kernel(bg_ref, cg_ref, x_ref, z_ref, cgh_ref, xh_ref, cw_ref, o_ref):
    i = pl.program_id(0)
    cur = cg_ref[...] * x_ref[...]
    halo = jnp.where(i > 0, cgh_ref[...] * xh_ref[...], 0.0)
    y = _causal_conv(cur, halo, cw_ref, SC_CONV)
    o_ref[...] = (bg_ref[...] * y * _silu(z_ref[...])).astype(o_ref.dtype)


def _short_conv(p_sc, conv_w, tm=256):
    t = p_sc.shape[0]
    tm = min(tm, t)
    hb = tm // SUBLANES
    main = lambda c: pl.BlockSpec((tm, SC_WIDTH), functools.partial(lambda i, c: (i, c), c=c))
    halo = lambda c: pl.BlockSpec((SUBLANES, SC_WIDTH),
                                  functools.partial(lambda i, c: (jnp.maximum(i * hb - 1, 0), c), c=c))
    return pl.pallas_call(
        _short_conv_kernel,
        grid=(t // tm,),
        in_specs=[main(0), main(1), main(2), main(3), halo(1), halo(2),
                  pl.BlockSpec((SC_CONV, SC_WIDTH), lambda i: (0, 0))],
        out_specs=pl.BlockSpec((tm, SC_WIDTH), lambda i: (i, 0)),
        out_shape=jax.ShapeDtypeStruct((t, SC_WIDTH), BF16),
        compiler_params=_cparams(("parallel",), 40),
        name="short_conv",
    )(p_sc, p_sc, p_sc, p_sc, p_sc, p_sc, conv_w)


_IN_GROUPS = (("nsa", _OFF_QA, _OFF_GATE), ("za", _OFF_ZA, _OFF_QB), ("gdn", _OFF_QB, _OFF_AB),
              ("zb", _OFF_ZB, _OFF_SC), ("sc", _OFF_SC, _OFF_MERGE), ("merge", _OFF_MERGE, None))


def _stage_w_in(w_in):
    depth, d, _ = w_in.shape
    w_t = jnp.transpose(w_in, (0, 2, 1)).astype(BF16)
    small_pad = LANES - 2 * GDN_HEADS - 3 * NSA_HEADS
    narrow = jnp.concatenate([w_in[:, :, _OFF_AB:_OFF_ZB], w_in[:, :, _OFF_GATE:_OFF_ZA],
                              jnp.zeros((depth, d, small_pad), F32)], axis=2)
    w_small_t = jnp.transpose(narrow, (0, 2, 1)).astype(BF16)
    return w_t, w_small_t


def _layer(x, layer, norm_w, w_t, w_small_t, nsa_qk_norm, cmp_pos, cmp_w1, cmp_w2, gdn_conv_w, gdn_a_log,
           gdn_dt_bias, gdn_norm_w, sc_conv_w, wb_nsa, wb_gdn, wb_sc, w_out, slopes):
    hn = _rmsnorm(x, norm_w)
    proj = {}
    for name, a, b in _IN_GROUPS:
        n = (w_t.shape[1] if b is None else b) - a
        proj[name] = _matmul_nt(hn, w_t, layer, a, n, F32, 1024, 512, "in_proj_" + name)
    small = _matmul_nt(hn, w_small_t, layer, 0, LANES, F32, 1024, LANES, "in_proj_small")
    p_nsa, z_a, p_gdn, z_b, p_sc, p_mg = (proj[n] for n, _, _ in _IN_GROUPS)

    q, kv_c, k_s, v_s, k_w, v_w = _nsa_prep(p_nsa, nsa_qk_norm)
    kv_cmp = _compress(kv_c, cmp_pos, cmp_w1, cmp_w2, nsa_qk_norm[1])
    o_cmp, sel, blk_any = _cmp_attention(q, kv_cmp, slopes)
    o_slc = _sel_attention(q, k_s, v_s, sel, blk_any, slopes)
    o_win = _win_attention(q, k_w, v_w, slopes)
    o_a = _nsa_combine(small, o_cmp, o_slc, o_win, z_a)

    q_b, k_b, v_b, gb = _gdn_prep(p_gdn, small, gdn_conv_w, gdn_a_log, gdn_dt_bias)
    o_b = _gdn_chunks(q_b, k_b, v_b, gb, z_b, gdn_norm_w)

    o_c = _short_conv(p_sc, sc_conv_w)

    merged = _merge(o_a, o_b, o_c, wb_nsa, wb_gdn, wb_sc, layer, p_mg)
    return _matmul_residual(merged, w_out, layer, x, 1024, 512)


def kernel(x, norm_w, w_in, nsa_qk_norm, cmp_pos, cmp_w1, cmp_w2, gdn_conv_w, gdn_a_log, gdn_dt_bias,
           gdn_norm_w, sc_conv_w, w_branch_nsa, w_branch_gdn, w_branch_sc, w_out):
    b, t, d = x.shape
    depth = norm_w.shape[0]
    heads = jnp.arange(1, NSA_HEADS + 1, dtype=F32)
    slopes = jnp.exp2(-8.0 * heads / NSA_HEADS)
    w_t, w_small_t = _stage_w_in(w_in)
    wb_nsa, wb_gdn, wb_sc, w_out16 = (w.astype(BF16) for w in (w_branch_nsa, w_branch_gdn, w_branch_sc, w_out))
    outs = []
    for bi in range(b):
        xb = x[bi]
        for l in range(depth):
            xb = _layer(xb, l, norm_w[l], w_t, w_small_t, nsa_qk_norm[l], cmp_pos[l], cmp_w1[l], cmp_w2[l],
                        gdn_conv_w[l], gdn_a_log[l], gdn_dt_bias[l], gdn_norm_w[l], sc_conv_w[l],
                        wb_nsa, wb_gdn, wb_sc, w_out16, slopes)
        outs.append(xb)
    return jnp.stack(outs, axis=0)
```

```python
import functools
import math

import jax
import jax.numpy as jnp
from jax import lax
from jax.experimental import pallas as pl
from jax.experimental.pallas import tpu as pltpu

F32 = jnp.float32
BF16 = jnp.bfloat16

HEAD_DIM = 128
NSA_HEADS = 16
NSA_KV_HEADS = 4
NSA_GROUP = NSA_HEADS // NSA_KV_HEADS
NSA_WIDTH = NSA_HEADS * HEAD_DIM
NSA_KV_WIDTH = NSA_KV_HEADS * HEAD_DIM
CMP_BLOCK = 32
CMP_STRIDE = 16
CMP_HIDDEN = 256
SEL_BLOCK = 64
SEL_TOPK = 16
WINDOW = 512
GDN_HEADS = 16
GDN_DK = 128
GDN_DV = 128
GDN_KEY_WIDTH = GDN_HEADS * GDN_DK
GDN_VAL_WIDTH = GDN_HEADS * GDN_DV
GDN_CONV = 4
GDN_CHUNK = 64
SC_WIDTH = 2048
SC_CONV = 3
N_BRANCH = 3
NORM_EPS = 1e-6
NEG_INF = -1e30
MASK_BIAS = -2e30
FORCE_SCORE = 1e6

LANES = 128
SUBLANES = 8
BF16_SUBLANES = 16
N_BLK_LANES = 128
MIB = 1024 * 1024

_OFF_QA = 0
_OFF_KVC = _OFF_QA + NSA_WIDTH
_OFF_KVS = _OFF_KVC + 2 * NSA_KV_WIDTH
_OFF_KVW = _OFF_KVS + 2 * NSA_KV_WIDTH
_OFF_GATE = _OFF_KVW + 2 * NSA_KV_WIDTH
_OFF_ZA = _OFF_GATE + 3 * NSA_HEADS
_OFF_QB = _OFF_ZA + NSA_WIDTH
_OFF_AB = _OFF_QB + 2 * GDN_KEY_WIDTH + GDN_VAL_WIDTH
_OFF_BETA = _OFF_AB + GDN_HEADS
_OFF_ZB = _OFF_BETA + GDN_HEADS
_OFF_SC = _OFF_ZB + GDN_VAL_WIDTH
_OFF_MERGE = _OFF_SC + 4 * SC_WIDTH

_SM_A = 0
_SM_BETA = GDN_HEADS
_SM_GATE = 2 * GDN_HEADS


def _cparams(sem, vmem_mib):
    return pltpu.CompilerParams(dimension_semantics=sem, vmem_limit_bytes=vmem_mib * MIB)


def _sigmoid(x):
    return jax.nn.sigmoid(x)


def _silu(x):
    return x * jax.nn.sigmoid(x)


def _dot(a, b):
    return jnp.dot(a, b, preferred_element_type=F32)


def _dot_nt(a, b):
    return lax.dot_general(a, b, (((1,), (1,)), ((), ())), preferred_element_type=F32)


def _dot_tn(a, b):
    return lax.dot_general(a, b, (((0,), (0,)), ((), ())), preferred_element_type=F32)


def _head_rms(x, w):
    return x * lax.rsqrt(jnp.mean(x * x, axis=-1, keepdims=True) + NORM_EPS) * w


def _rmsnorm_kernel(x_ref, w_ref, o_ref):
    x = x_ref[...]
    y = x * lax.rsqrt(jnp.mean(x * x, axis=-1, keepdims=True) + NORM_EPS)
    o_ref[...] = (y * w_ref[...]).astype(o_ref.dtype)


def _rmsnorm(x, w, tm=256):
    t, d = x.shape
    return pl.pallas_call(
        _rmsnorm_kernel,
        grid=(t // tm,),
        in_specs=[pl.BlockSpec((tm, d), lambda i: (i, 0)), pl.BlockSpec((1, d), lambda i: (0, 0))],
        out_specs=pl.BlockSpec((tm, d), lambda i: (i, 0)),
        out_shape=jax.ShapeDtypeStruct((t, d), BF16),
        compiler_params=_cparams(("parallel",), 32),
        name="rmsnorm",
    )(x, w.reshape(1, d))


def _mm_kernel(a_ref, b_ref, o_ref):
    o_ref[...] = _dot(a_ref[...], b_ref[...]).astype(o_ref.dtype)


def _mm_nt_kernel(a_ref, b_ref, o_ref):
    o_ref[...] = _dot_nt(a_ref[...], b_ref[...]).astype(o_ref.dtype)


def _matmul_nt(a, b_t, layer, row_off, n, out_dtype, tm, tn, name):
    m, k = a.shape
    tm = min(tm, m)
    tn = min(tn, n)
    depth, rows, _ = b_t.shape
    first_row = layer * rows + row_off
    assert first_row % BF16_SUBLANES == 0 and n % tn == 0
    b_t = b_t.reshape(depth * rows, k)
    return pl.pallas_call(
        _mm_nt_kernel,
        grid=(m // tm, n // tn),
        in_specs=[pl.BlockSpec((tm, k), lambda i, j: (i, 0)),
                  pl.BlockSpec((pl.Element(tn), pl.Element(k)),
                               lambda i, j: (pl.multiple_of(first_row + j * tn, BF16_SUBLANES), 0))],
        out_specs=pl.BlockSpec((tm, tn), lambda i, j: (i, j)),
        out_shape=jax.ShapeDtypeStruct((m, n), out_dtype),
        compiler_params=_cparams(("parallel", "arbitrary"), 52),
        name=name,
    )(a, b_t)


def _mm_res_kernel(a_ref, b_ref, r_ref, o_ref):
    o_ref[...] = r_ref[...] + _dot(a_ref[...], b_ref[...])


def _matmul_residual(a, b, layer, r, tm, tn):
    m, k = a.shape
    n = b.shape[2]
    tm = min(tm, m)
    return pl.pallas_call(
        _mm_res_kernel,
        grid=(m // tm, n // tn),
        in_specs=[pl.BlockSpec((tm, k), lambda i, j: (i, 0)),
                  pl.BlockSpec((None, k, tn), lambda i, j: (layer, 0, j)),
                  pl.BlockSpec((tm, tn), lambda i, j: (i, j))],
        out_specs=pl.BlockSpec((tm, tn), lambda i, j: (i, j)),
        out_shape=jax.ShapeDtypeStruct((m, n), F32),
        compiler_params=_cparams(("parallel", "arbitrary"), 48),
        name="out_proj_residual",
    )(a, b, r)


def _merge_kernel(oa_ref, ob_ref, oc_ref, wa_ref, wb_ref, wc_ref, ga_ref, gb_ref, gc_ref, o_ref):
    half = o_ref.shape[1] // 2
    cols = [slice(0, half), slice(half, 2 * half)]
    prods = [[_dot(a_ref[...], w_ref[:, c]) for a_ref, w_ref in ((oa_ref, wa_ref), (ob_ref, wb_ref), (oc_ref, wc_ref))]
             for c in cols]
    for c, (pa, pb, pc) in zip(cols, prods):
        acc = _sigmoid(ga_ref[:, c]) * pa
        acc = acc + _sigmoid(gb_ref[:, c]) * pb
        acc = acc + _sigmoid(gc_ref[:, c]) * pc
        o_ref[:, c] = acc.astype(o_ref.dtype)


def _merge(o_a, o_b, o_c, wa, wb, wc, layer, gates, tm=512, tn=512):
    m, k = o_a.shape
    n = wa.shape[2]
    tm = min(tm, m)
    nb = n // tn
    a_spec = pl.BlockSpec((tm, k), lambda i, j: (i, 0))
    w_spec = pl.BlockSpec((None, k, tn), lambda i, j: (layer, 0, j))
    g_specs = [pl.BlockSpec((tm, tn), functools.partial(lambda i, j, br: (i, br * nb + j), br=br))
               for br in range(N_BRANCH)]
    return pl.pallas_call(
        _merge_kernel,
        grid=(m // tm, nb),
        in_specs=[a_spec, a_spec, a_spec, w_spec, w_spec, w_spec] + g_specs,
        out_specs=pl.BlockSpec((tm, tn), lambda i, j: (i, j)),
        out_shape=jax.ShapeDtypeStruct((m, n), BF16),
        compiler_params=_cparams(("parallel", "arbitrary"), 48),
        name="branch_merge",
    )(o_a, o_b, o_c, wa, wb, wc, gates, gates, gates)


def _nsa_prep_kernel(p_ref, nw_ref, q_ref, kvc_ref, ks_ref, vs_ref, kw_ref, vw_ref):
    nw = nw_ref[...]
    for h in range(NSA_HEADS):
        sl = slice(h * HEAD_DIM, (h + 1) * HEAD_DIM)
        q_ref[:, sl] = (_head_rms(p_ref[:, sl], nw[0:1]) * (HEAD_DIM ** -0.5)).astype(BF16)
    kvc_ref[...] = p_ref[:, _OFF_KVC:_OFF_KVS]
    for g in range(NSA_KV_HEADS):
        sl = slice(g * HEAD_DIM, (g + 1) * HEAD_DIM)
        ks = p_ref[:, _OFF_KVS + g * HEAD_DIM:_OFF_KVS + (g + 1) * HEAD_DIM]
        ks_ref[:, sl] = _head_rms(ks, nw[2:3]).astype(BF16)
        vs_ref[:, sl] = p_ref[:, _OFF_KVS + NSA_KV_WIDTH + g * HEAD_DIM:
                              _OFF_KVS + NSA_KV_WIDTH + (g + 1) * HEAD_DIM].astype(BF16)
        kw = p_ref[:, _OFF_KVW + g * HEAD_DIM:_OFF_KVW + (g + 1) * HEAD_DIM]
        kw_ref[:, sl] = _head_rms(kw, nw[3:4]).astype(BF16)
        vw_ref[:, sl] = p_ref[:, _OFF_KVW + NSA_KV_WIDTH + g * HEAD_DIM:
                              _OFF_KVW + NSA_KV_WIDTH + (g + 1) * HEAD_DIM].astype(BF16)


def _nsa_prep(p_nsa, qk_norm, tm=256):
    t, width = p_nsa.shape
    row = lambda i: (i, 0)
    return pl.pallas_call(
        _nsa_prep_kernel,
        grid=(t // tm,),
        in_specs=[pl.BlockSpec((tm, width), row), pl.BlockSpec((4, HEAD_DIM), lambda i: (0, 0))],
        out_specs=[pl.BlockSpec((tm, NSA_WIDTH), row)] + [pl.BlockSpec((tm, 2 * NSA_KV_WIDTH), row)]
        + [pl.BlockSpec((tm, NSA_KV_WIDTH), row)] * 4,
        out_shape=[jax.ShapeDtypeStruct((t, NSA_WIDTH), BF16), jax.ShapeDtypeStruct((t, 2 * NSA_KV_WIDTH), F32)]
        + [jax.ShapeDtypeStruct((t, NSA_KV_WIDTH), BF16)] * 4,
        compiler_params=_cparams(("parallel",), 40),
        name="nsa_prep",
    )(p_nsa, qk_norm)


def _compress_kernel(x_ref, pos_ref, w1lo_ref, w1hi_ref, w2_ref, nw_ref, o_ref, lo_acc, hi_acc, *, n_chunk):
    kv = pl.program_id(0)
    l = pl.program_id(1)

    @pl.when(l == 0)
    def _():
        lo_acc[...] = jnp.zeros(lo_acc.shape, F32)
        hi_acc[...] = jnp.zeros(hi_acc.shape, F32)

    pos_lo = pos_ref[0, pl.ds(l, 1), :]
    pos_hi = pos_ref[0, pl.ds(l + CMP_STRIDE, 1), :]
    w_lo = w1lo_ref[0, 0]
    w_hi = w1hi_ref[0, 0]
    for g in range(NSA_KV_HEADS):
        x = x_ref[:, g * HEAD_DIM:(g + 1) * HEAD_DIM]
        lo_acc[g] += _dot((x + pos_lo).astype(BF16), w_lo)
        hi_acc[g] += _dot((x + pos_hi).astype(BF16), w_hi)

    @pl.when(l == CMP_STRIDE - 1)
    def _():
        for g in range(NSA_KV_HEADS):
            hidden = _silu(lo_acc[g] + pltpu.roll(hi_acc[g], shift=n_chunk - 1, axis=0))
            out = _dot(hidden.astype(BF16), w2_ref[0])
            normed = _head_rms(out, nw_ref[...])
            o_ref[0, g] = jnp.where(kv == 0, normed, out).astype(BF16)


def _compress(kv_c, cmp_pos, cmp_w1, cmp_w2, k_norm_w):
    t = kv_c.shape[0]
    n_chunk = t // CMP_STRIDE
    x = kv_c.reshape(n_chunk, CMP_STRIDE * 2 * NSA_KV_WIDTH)
    w1 = cmp_w1.astype(BF16)
    w2 = cmp_w2.astype(BF16)
    return pl.pallas_call(
        functools.partial(_compress_kernel, n_chunk=n_chunk),
        grid=(2, CMP_STRIDE),
        in_specs=[pl.BlockSpec((n_chunk, NSA_KV_WIDTH), lambda a, l: (0, 2 * l + a)),
                  pl.BlockSpec((1, CMP_BLOCK, HEAD_DIM), lambda a, l: (a, 0, 0)),
                  pl.BlockSpec((1, 1, HEAD_DIM, CMP_HIDDEN), lambda a, l: (a, l, 0, 0)),
                  pl.BlockSpec((1, 1, HEAD_DIM, CMP_HIDDEN), lambda a, l: (a, l + CMP_STRIDE, 0, 0)),
                  pl.BlockSpec((1, CMP_HIDDEN, HEAD_DIM), lambda a, l: (a, 0, 0)),
                  pl.BlockSpec((1, HEAD_DIM), lambda a, l: (0, 0))],
        out_specs=pl.BlockSpec((1, NSA_KV_HEADS, n_chunk, HEAD_DIM), lambda a, l: (a, 0, 0, 0)),
        out_shape=jax.ShapeDtypeStruct((2, NSA_KV_HEADS, n_chunk, HEAD_DIM), BF16),
        scratch_shapes=[pltpu.VMEM((NSA_KV_HEADS, n_chunk, CMP_HIDDEN), F32),
                        pltpu.VMEM((NSA_KV_HEADS, n_chunk, CMP_HIDDEN), F32)],
        compiler_params=_cparams(("arbitrary", "arbitrary"), 40),
        name="nsa_compress",
    )(x, cmp_pos, w1, w1, w2, k_norm_w.reshape(1, HEAD_DIM))


def _cmp_attn_kernel(slopes_ref, q_ref, k_ref, v_ref, cov_ref, o_ref, sel_ref, any_ref, *, tq, n_chunk):
    g = pl.program_id(0)
    qi = pl.program_id(1)
    k = k_ref[0, 0]
    v_t = v_ref[0, 0].T
    t0 = qi * tq
    t_pos = t0 + lax.broadcasted_iota(jnp.int32, (n_chunk, tq), 1)
    k_end = lax.broadcasted_iota(jnp.int32, (n_chunk, tq), 0) * CMP_STRIDE + (CMP_BLOCK - 1)
    mask_bias = jnp.where(t_pos >= k_end, 0.0, MASK_BIAS)
    k_rel = (k_end - t0).astype(F32)
    psum = jnp.zeros((n_chunk, tq), F32)
    sls = [slice(z * HEAD_DIM, (z + 1) * HEAD_DIM) for z in range(NSA_GROUP)]
    qk_next = _dot_nt(k, q_ref[:, sls[0]])
    pending = None
    for z in range(NSA_GROUP):
        qk = qk_next
        if z + 1 < NSA_GROUP:
            qk_next = _dot_nt(k, q_ref[:, sls[z + 1]])
        s = qk + (mask_bias + slopes_ref[g * NSA_GROUP + z] * k_rel)
        m = jnp.maximum(jnp.max(s, axis=0, keepdims=True), NEG_INF)
        e = jnp.exp(s - m)
        p = e * (1.0 / jnp.maximum(jnp.sum(e, axis=0, keepdims=True), 1e-30))
        psum = psum + p
        if pending is not None:
            o_ref[:, sls[pending[0]]] = _dot(v_t, pending[1]).T
        pending = (z, p.astype(BF16))
    o_ref[:, sls[pending[0]]] = _dot(v_t, pending[1]).T

    p_hi = psum.astype(BF16)
    p_lo = (psum - p_hi.astype(F32)).astype(BF16)
    cov_t = cov_ref[...]
    imp = _dot(cov_t, p_hi) + _dot(cov_t, p_lo)

    j = lax.broadcasted_iota(jnp.int32, (N_BLK_LANES, tq), 0)
    cur = (t0 + lax.broadcasted_iota(jnp.int32, (N_BLK_LANES, tq), 1)) // SEL_BLOCK
    valid = j <= cur
    forced = (j == 0) | (j == cur) | (j == cur - 1)
    chosen_first = valid & forced
    val = jnp.where(valid, jnp.where(forced, -2.0, imp), -1.0)
    sel = chosen_first.astype(F32)
    jf = j.astype(F32)
    for _ in range(SEL_TOPK - 3):
        m = jnp.max(val, axis=0, keepdims=True)
        idx = jnp.min(jnp.where(val == m, jf, float(N_BLK_LANES)), axis=0, keepdims=True)
        hit = jf == idx
        sel = jnp.where(hit & (m > -0.5), 1.0, sel)
        val = jnp.where(hit, -2.0, val)
    sel_q = sel.T
    sel_ref[0] = sel_q.astype(BF16)
    any_ref[0, 0] = jnp.broadcast_to(jnp.max(sel_q, axis=0, keepdims=True), (SUBLANES, N_BLK_LANES))


def _cmp_attention(q, kv_cmp, slopes, tq=256):
    t = q.shape[0]
    n_chunk = kv_cmp.shape[2]
    n_cmp = n_chunk - CMP_BLOCK // CMP_STRIDE + 1
    n_blk = t // SEL_BLOCK
    assert n_blk <= N_BLK_LANES
    tq = min(tq, t)
    cs = jnp.arange(n_chunk)[:, None] * CMP_STRIDE
    bs = jnp.arange(N_BLK_LANES)[None, :] * SEL_BLOCK
    cover = ((cs <= bs + SEL_BLOCK - 1) & (cs + CMP_BLOCK - 1 >= bs)
             & (jnp.arange(n_chunk)[:, None] < n_cmp) & (jnp.arange(N_BLK_LANES)[None, :] < n_blk))
    cover = cover.astype(BF16).T
    return pl.pallas_call(
        functools.partial(_cmp_attn_kernel, tq=tq, n_chunk=n_chunk),
        grid=(NSA_KV_HEADS, t // tq),
        in_specs=[pl.BlockSpec(memory_space=pltpu.SMEM),
                  pl.BlockSpec((tq, NSA_GROUP * HEAD_DIM), lambda g, i: (i, g)),
                  pl.BlockSpec((1, 1, n_chunk, HEAD_DIM), lambda g, i: (0, g, 0, 0)),
                  pl.BlockSpec((1, 1, n_chunk, HEAD_DIM), lambda g, i: (1, g, 0, 0)),
                  pl.BlockSpec((N_BLK_LANES, n_chunk), lambda g, i: (0, 0))],
        out_specs=[pl.BlockSpec((tq, NSA_GROUP * HEAD_DIM), lambda g, i: (i, g)),
                   pl.BlockSpec((1, tq, N_BLK_LANES), lambda g, i: (g, i, 0)),
                   pl.BlockSpec((1, 1, SUBLANES, N_BLK_LANES), lambda g, i: (g, i, 0, 0))],
        out_shape=[jax.ShapeDtypeStruct((t, NSA_WIDTH), F32),
                   jax.ShapeDtypeStruct((NSA_KV_HEADS, t, N_BLK_LANES), BF16),
                   jax.ShapeDtypeStruct((NSA_KV_HEADS, t // tq, SUBLANES, N_BLK_LANES), F32)],
        compiler_params=_cparams(("parallel", "parallel"), 40),
        name="nsa_cmp_attn",
    )(slopes, q, kv_cmp, kv_cmp, cover)


def _sel_attn_kernel(cnt_ref, kjs_ref, slopes_ref, q_ref, k_ref, v_ref, eb_ref, sel_ref, o_ref,
                     qa_sc, m_sc, l_sc, acc_sc, *, tq, tk, max_tiles):
    g = pl.program_id(0)
    qi = pl.program_id(1)
    item = g * pl.num_programs(1) + qi
    t0 = qi * tq

    m_sc[...] = jnp.full(m_sc.shape, NEG_INF, F32)
    l_sc[...] = jnp.zeros(l_sc.shape, F32)
    acc_sc[...] = jnp.zeros(acc_sc.shape, F32)
    unselected = sel_ref[0] - 1.0
    for z in range(NSA_GROUP):
        qa_sc[z] = jnp.concatenate([q_ref[:, z * HEAD_DIM:(z + 1) * HEAD_DIM], unselected], axis=1)

    def tile(kj, causal):
        k0 = kj * tk
        rows = pl.ds(pl.multiple_of(k0, tk), tk)
        k_aug = jnp.concatenate([k_ref[rows, :], eb_ref[kj]], axis=1)
        v_t = v_ref[rows, :].T
        row = lax.broadcasted_iota(jnp.int32, (tk, tq), 0)
        k_rel = (k0 - t0 + row).astype(F32)
        if causal:
            col = lax.broadcasted_iota(jnp.int32, (tk, tq), 1)
            causal_bias = jnp.where(t0 + col >= k0 + row, 0.0, MASK_BIAS)
        qk_next = _dot_nt(k_aug, qa_sc[0])
        pending = None
        for z in range(NSA_GROUP):
            qk = qk_next
            if z + 1 < NSA_GROUP:
                qk_next = _dot_nt(k_aug, qa_sc[z + 1])
            s = qk + slopes_ref[g * NSA_GROUP + z] * k_rel
            if causal:
                s = s + causal_bias
            m_prev = m_sc[z]
            m_new = jnp.maximum(m_prev, jnp.max(s, axis=0, keepdims=True))
            alpha = jnp.exp(m_prev - m_new)
            e = jnp.exp(s - m_new)
            l_sc[z] = alpha * l_sc[z] + jnp.sum(e, axis=0, keepdims=True)
            m_sc[z] = m_new
            if pending is not None:
                pz, palpha, pe = pending
                acc_sc[pz] = palpha * acc_sc[pz] + _dot(v_t, pe)
            pending = (z, alpha, e.astype(BF16))
        pz, palpha, pe = pending
        acc_sc[pz] = palpha * acc_sc[pz] + _dot(v_t, pe)

    def visit(i, carry):
        tile(kjs_ref[item * max_tiles + i], False)
        return carry

    lax.fori_loop(0, cnt_ref[item], visit, 0)
    tile(t0 // tk, True)

    for z in range(NSA_GROUP):
        out_t = acc_sc[z] * (1.0 / jnp.maximum(l_sc[z], 1e-30))
        o_ref[:, z * HEAD_DIM:(z + 1) * HEAD_DIM] = out_t.T


def _sel_work_list(blk_any, t, tq, tk):
    nq, nk, bpt = t // tq, t // tk, tk // SEL_BLOCK
    n_blk = t // SEL_BLOCK
    picked = blk_any[:, :, 0, :n_blk].reshape(NSA_KV_HEADS, nq, nk, bpt).max(axis=-1) > 0.5
    diagonal = (jnp.arange(nq) * tq) // tk
    before = picked & (jnp.arange(nk)[None, :] < diagonal[:, None])[None]
    count = before.sum(axis=-1).astype(jnp.int32)
    tiles = jnp.argsort(jnp.logical_not(before), axis=-1, stable=True).astype(jnp.int32)
    return count.reshape(-1), tiles.reshape(-1)


def _sel_attention(q, k_s, v_s, sel, blk_any, slopes, tq=256, tk=512):
    t = q.shape[0]
    tq = min(tq, t)
    tk = min(tk, t)
    assert tk % tq == 0
    nk = t // tk
    count, tiles = _sel_work_list(blk_any, t, tq, tk)

    blk_of_key = jnp.arange(t)[:, None] // SEL_BLOCK
    block_onehot = jnp.where(blk_of_key == jnp.arange(N_BLK_LANES)[None, :], -MASK_BIAS, 0.0)
    block_onehot = block_onehot.astype(BF16).reshape(nk, tk, N_BLK_LANES)

    smem = pl.BlockSpec(memory_space=pltpu.SMEM)
    return pl.pallas_call(
        functools.partial(_sel_attn_kernel, tq=tq, tk=tk, max_tiles=nk),
        grid=(NSA_KV_HEADS, t // tq),
        in_specs=[smem, smem, smem,
                  pl.BlockSpec((tq, NSA_GROUP * HEAD_DIM), lambda g, i: (i, g)),
                  pl.BlockSpec((t, HEAD_DIM), lambda g, i: (0, g)),
                  pl.BlockSpec((t, HEAD_DIM), lambda g, i: (0, g)),
                  pl.BlockSpec((nk, tk, N_BLK_LANES), lambda g, i: (0, 0, 0)),
                  pl.BlockSpec((1, tq, N_BLK_LANES), lambda g, i: (g, i, 0))],
        out_specs=pl.BlockSpec((tq, NSA_GROUP * HEAD_DIM), lambda g, i: (i, g)),
        out_shape=jax.ShapeDtypeStruct((t, NSA_WIDTH), F32),
        scratch_shapes=[pltpu.VMEM((NSA_GROUP, tq, 2 * HEAD_DIM), BF16),
                        pltpu.VMEM((NSA_GROUP, 1, tq), F32), pltpu.VMEM((NSA_GROUP, 1, tq), F32),
                        pltpu.VMEM((NSA_GROUP, HEAD_DIM, tq), F32)],
        compiler_params=_cparams(("arbitrary", "arbitrary"), 40),
        name="nsa_sel_attn",
    )(count, tiles, slopes, q, k_s, v_s, block_onehot, sel)


def _win_attn_kernel(slopes_ref, q_ref, *refs, tq, n_tile):
    k_refs = refs[:n_tile]
    v_refs = refs[n_tile:2 * n_tile]
    o_ref = refs[2 * n_tile]
    g = pl.program_id(0)
    qi = pl.program_id(1)
    span = n_tile * tq
    k = jnp.concatenate([r[...] for r in k_refs], axis=0)
    v = jnp.concatenate([r[...] for r in v_refs], axis=0)
    t_pos = qi * tq + lax.broadcasted_iota(jnp.int32, (tq, span), 0)
    k_pos = (qi - (n_tile - 1)) * tq + lax.broadcasted_iota(jnp.int32, (tq, span), 1)
    dist = t_pos - k_pos
    mask_bias = jnp.where((dist >= 0) & (dist < WINDOW) & (k_pos >= 0), 0.0, MASK_BIAS)
    k_rel = (lax.broadcasted_iota(jnp.int32, (1, span), 1) - (n_tile - 1) * tq).astype(F32)
    sls = [slice(z * HEAD_DIM, (z + 1) * HEAD_DIM) for z in range(NSA_GROUP)]
    qk_next = _dot_nt(q_ref[:, sls[0]], k)
    pending = None
    for z in range(NSA_GROUP):
        qk = qk_next
        if z + 1 < NSA_GROUP:
            qk_next = _dot_nt(q_ref[:, sls[z + 1]], k)
        s = qk + (mask_bias + slopes_ref[g * NSA_GROUP + z] * k_rel)
        m = jnp.maximum(jnp.max(s, axis=-1, keepdims=True), NEG_INF)
        e = jnp.exp(s - m)
        inv_l = 1.0 / jnp.maximum(jnp.sum(e, axis=-1, keepdims=True), 1e-30)
        if pending is not None:
            o_ref[:, sls[pending[0]]] = _dot(pending[1], v) * pending[2]
        pending = (z, e.astype(BF16), inv_l)
    o_ref[:, sls[pending[0]]] = _dot(pending[1], v) * pending[2]


def _win_attention(q, k_w, v_w, slopes, tq=256):
    t = q.shape[0]
    tq = min(tq, t)
    n_tile = -(-WINDOW // tq) + 1

    def kv_spec(c):
        return pl.BlockSpec((tq, HEAD_DIM), lambda g, i: (jnp.maximum(i - (n_tile - 1) + c, 0), g))

    return pl.pallas_call(
        functools.partial(_win_attn_kernel, tq=tq, n_tile=n_tile),
        grid=(NSA_KV_HEADS, t // tq),
        in_specs=[pl.BlockSpec(memory_space=pltpu.SMEM),
                  pl.BlockSpec((tq, NSA_GROUP * HEAD_DIM), lambda g, i: (i, g))]
        + [kv_spec(c) for c in range(n_tile)] * 2,
        out_specs=pl.BlockSpec((tq, NSA_GROUP * HEAD_DIM), lambda g, i: (i, g)),
        out_shape=jax.ShapeDtypeStruct((t, NSA_WIDTH), F32),
        compiler_params=_cparams(("parallel", "parallel"), 40),
        name="nsa_win_attn",
    )(slopes, q, *([k_w] * n_tile), *([v_w] * n_tile))


def _nsa_combine_kernel(sm_ref, ex_ref, oc_ref, os_ref, ow_ref, z_ref, o_ref):
    logits = sm_ref[...]
    hi = logits.astype(BF16)
    lo = (logits - hi.astype(F32)).astype(BF16)
    acc = None
    for br, branch_ref in enumerate((oc_ref, os_ref, ow_ref)):
        ex = ex_ref[br]
        gate = _sigmoid(_dot(hi, ex) + _dot(lo, ex))
        term = gate * branch_ref[...]
        acc = term if acc is None else acc + term
    o_ref[...] = (acc * _silu(z_ref[...])).astype(o_ref.dtype)


def _nsa_combine(small, o_cmp, o_slc, o_win, z_a, tm=256):
    t = small.shape[0]
    tm = min(tm, t)
    lane = jnp.arange(LANES)[None, :, None]
    br = jnp.arange(N_BRANCH)[:, None, None]
    head = (jnp.arange(NSA_WIDTH) // HEAD_DIM)[None, None, :]
    expand = (lane == _SM_GATE + head * N_BRANCH + br).astype(BF16)
    row = lambda i: (i, 0)
    wide = pl.BlockSpec((tm, NSA_WIDTH), row)
    return pl.pallas_call(
        _nsa_combine_kernel,
        grid=(t // tm,),
        in_specs=[pl.BlockSpec((tm, LANES), row),
                  pl.BlockSpec((N_BRANCH, LANES, NSA_WIDTH), lambda i: (0, 0, 0)),
                  wide, wide, wide, wide],
        out_specs=wide,
        out_shape=jax.ShapeDtypeStruct((t, NSA_WIDTH), BF16),
        compiler_params=_cparams(("parallel",), 40),
        name="nsa_combine",
    )(small, expand, o_cmp, o_slc, o_win, z_a)


def _shift_rows(cur, halo, s):
    rolled = pltpu.roll(cur, shift=s, axis=0)
    halo_rolled = pltpu.roll(halo, shift=s, axis=0)
    row = lax.broadcasted_iota(jnp.int32, halo.shape, 0)
    head = jnp.where(row < s, halo_rolled, rolled[0:SUBLANES])
    return jnp.concatenate([head, rolled[SUBLANES:]], axis=0)


def _causal_conv(cur, halo, w_ref, k):
    acc = None
    for j in range(k):
        s = k - 1 - j
        term = (cur if s == 0 else _shift_rows(cur, halo, s)) * w_ref[j:j + 1, :]
        acc = term if acc is None else acc + term
    return acc


def _gdn_prep_kernel(p_ref, halo_ref, cw_ref, sm_ref, alog_ref, dtb_ref, q_ref, k_ref, v_ref, gb_ref, xx):
    i = pl.program_id(0)
    tm = p_ref.shape[0]
    xx[0:SUBLANES, :] = jnp.where(i > 0, halo_ref[...], 0.0)
    xx[SUBLANES:, :] = p_ref[...]

    def conv_silu(cols):
        acc = None
        for j in range(GDN_CONV):
            delay = GDN_CONV - 1 - j
            term = xx[SUBLANES - delay:SUBLANES - delay + tm, cols] * cw_ref[j:j + 1, cols]
            acc = term if acc is None else acc + term
        return _silu(acc)

    for h in range(GDN_HEADS):
        sl = slice(h * GDN_DK, (h + 1) * GDN_DK)
        qh = conv_silu(sl)
        q_ref[:, sl] = qh * lax.rsqrt(jnp.sum(qh * qh, axis=-1, keepdims=True) + NORM_EPS) * (GDN_DK ** -0.5)
        kh = conv_silu(slice(GDN_KEY_WIDTH + h * GDN_DK, GDN_KEY_WIDTH + (h + 1) * GDN_DK))
        k_ref[:, sl] = kh * lax.rsqrt(jnp.sum(kh * kh, axis=-1, keepdims=True) + NORM_EPS)
        v_ref[:, sl] = conv_silu(slice(2 * GDN_KEY_WIDTH + h * GDN_DV, 2 * GDN_KEY_WIDTH + (h + 1) * GDN_DV))
    sm = sm_ref[...]
    lane = lax.broadcasted_iota(jnp.int32, sm.shape, 1)
    gdecay = -jnp.exp(alog_ref[...]) * jax.nn.softplus(sm + dtb_ref[...])
    beta = _sigmoid(sm)
    gb_ref[...] = jnp.where(lane < _SM_BETA, gdecay, jnp.where(lane < _SM_GATE, beta, 0.0))


def _gdn_prep(p_gdn, small, conv_w, a_log, dt_bias, tm=256):
    t, width = p_gdn.shape
    tm = min(tm, t)
    hb = tm // SUBLANES
    row = lambda i: (i, 0)
    alog = jnp.zeros((1, LANES), F32).at[0, _SM_A:_SM_A + GDN_HEADS].set(a_log)
    dtb = jnp.zeros((1, LANES), F32).at[0, _SM_A:_SM_A + GDN_HEADS].set(dt_bias)
    const = lambda i: (0, 0)
    return pl.pallas_call(
        _gdn_prep_kernel,
        grid=(t // tm,),
        in_specs=[pl.BlockSpec((tm, width), row),
                  pl.BlockSpec((SUBLANES, width), lambda i: (jnp.maximum(i * hb - 1, 0), 0)),
                  pl.BlockSpec((GDN_CONV, width), const),
                  pl.BlockSpec((tm, LANES), row),
                  pl.BlockSpec((1, LANES), const), pl.BlockSpec((1, LANES), const)],
        out_specs=[pl.BlockSpec((tm, GDN_KEY_WIDTH), row), pl.BlockSpec((tm, GDN_KEY_WIDTH), row),
                   pl.BlockSpec((tm, GDN_VAL_WIDTH), row), pl.BlockSpec((tm, LANES), row)],
        out_shape=[jax.ShapeDtypeStruct((t, GDN_KEY_WIDTH), F32), jax.ShapeDtypeStruct((t, GDN_KEY_WIDTH), F32),
                   jax.ShapeDtypeStruct((t, GDN_VAL_WIDTH), F32), jax.ShapeDtypeStruct((t, LANES), F32)],
        scratch_shapes=[pltpu.VMEM((tm + SUBLANES, width), F32)],
        compiler_params=_cparams(("parallel",), 48),
        name="gdn_prep",
    )(p_gdn, p_gdn, conv_w, small, alog, dtb)


def _row_pad(x):
    return jnp.concatenate([x, jnp.zeros_like(x)], axis=0)


def _gdn_intra_kernel(q_ref, k_ref, v_ref, gb_ref, u_ref, w_ref, qe_ref, ke_ref, at_ref, eg_ref, *, heads):
    c = GDN_CHUNK
    gb = gb_ref[...]
    row = lax.broadcasted_iota(jnp.int32, gb.shape, 0)
    gcum = gb
    shift = 1
    while shift < c:
        gcum = gcum + jnp.where(row >= shift, pltpu.roll(gcum, shift=shift, axis=0), 0.0)
        shift *= 2
    lane = lax.broadcasted_iota(jnp.int32, gb.shape, 1)
    ri = lax.broadcasted_iota(jnp.int32, (c, LANES), 0)
    ci = lax.broadcasted_iota(jnp.int32, (c, LANES), 1)
    tri = ri >= ci
    strict = ri > ci
    eye = ri == ci
    eyef = eye.astype(F32)
    lane8 = lax.broadcasted_iota(jnp.int32, (SUBLANES, LANES), 1)
    eg_all = jnp.zeros((SUBLANES, LANES), F32)
    hs = range(heads)
    sls = [slice(h * GDN_DK, (h + 1) * GDN_DK) for h in hs]

    gcol = [jnp.sum(jnp.where(lane == _SM_A + h, gcum, 0.0), axis=-1, keepdims=True) for h in hs]
    beta = [jnp.sum(jnp.where(lane == _SM_BETA + h, gb, 0.0), axis=-1, keepdims=True) for h in hs]
    decay = []
    for h in hs:
        gmat = jnp.broadcast_to(gcol[h], (c, LANES))
        grow = jnp.sum(jnp.where(eye, gmat, 0.0), axis=0, keepdims=True)
        decay.append(jnp.where(tri, jnp.exp(jnp.where(tri, gmat - grow, 0.0)), 0.0))
    glast = [gcol[h][c - 1:c, :] for h in hs]
    egc = [jnp.exp(gcol[h]) for h in hs]
    q = [q_ref[:, sls[h]] for h in hs]
    k = [k_ref[:, sls[h]] for h in hs]
    k16 = [k[h].astype(BF16) for h in hs]
    qk = [_dot_nt(jnp.concatenate([q[h].astype(BF16), k16[h]], axis=0), _row_pad(k16[h])) for h in hs]
    for h in hs:
        at_ref[:, sls[h]] = (qk[h][:c] * decay[h]).astype(BF16)

    pw = [-jnp.where(strict, beta[h] * qk[h][c:] * decay[h], 0.0) for h in hs]
    inv = [eyef + pw[h] for h in hs]
    pw16 = [pw[h].astype(BF16) for h in hs]
    pw = [_dot(pw16[h], _row_pad(pw16[h])) for h in hs]
    span = 2
    while span < c:
        pw16 = [pw[h].astype(BF16) for h in hs]
        if 2 * span < c:
            both = [_dot(jnp.concatenate([pw16[h], inv[h].astype(BF16)], axis=0), _row_pad(pw16[h])) for h in hs]
            pw = [both[h][:c] for h in hs]
            inv = [inv[h] + both[h][c:] for h in hs]
        else:
            inv = [inv[h] + _dot(inv[h].astype(BF16), _row_pad(pw16[h])) for h in hs]
        span *= 2

    uw = []
    for h in hs:
        rhs = jnp.concatenate([(v_ref[:, sls[h]] * beta[h]).astype(BF16),
                               (k[h] * (beta[h] * egc[h])).astype(BF16)], axis=1)
        uw.append(_dot(inv[h].astype(BF16), _row_pad(rhs)))
    for h in hs:
        u_ref[:, sls[h]] = uw[h][:, :GDN_DV]
        w_ref[:, sls[h]] = uw[h][:, GDN_DV:].astype(BF16)
        qe_ref[:, sls[h]] = (q[h] * egc[h]).astype(BF16)
        ke_ref[:, sls[h]] = (k[h] * jnp.exp(glast[h] - gcol[h])).astype(BF16)
        eg_all = jnp.where(lane8 == h, jnp.exp(glast[h]), eg_all)
    eg_ref[0] = eg_all


def _gdn_scan_kernel(eg_ref, u_ref, w_ref, qe_ref, ke_ref, at_ref, z_ref, nw_ref, o_ref, state, *, heads, cps):
    n = pl.program_id(0)
    c = GDN_CHUNK

    @pl.when(n == 0)
    def _():
        state[...] = jnp.zeros(state.shape, F32)

    nw = nw_ref[...]
    hs = range(heads)
    sls = [slice(h * GDN_DK, (h + 1) * GDN_DK) for h in hs]
    s_cur = [state[h] for h in hs]
    for cc in range(cps):
        rows = slice(cc * c, (cc + 1) * c)
        ws_qs = [_dot(jnp.concatenate([w_ref[rows, sls[h]], qe_ref[rows, sls[h]]], axis=0), s_cur[h].astype(BF16))
                 for h in hs]
        v_new16 = [(u_ref[rows, sls[h]] - ws_qs[h][:c]).astype(BF16) for h in hs]
        o = [ws_qs[h][c:] + _dot(at_ref[rows, h * GDN_DK:h * GDN_DK + c], v_new16[h]) for h in hs]
        s_cur = [s_cur[h] * eg_ref[(n * cps + cc) * heads + h] + _dot_tn(ke_ref[rows, sls[h]], v_new16[h])
                 for h in hs]
        for h in hs:
            o_ref[rows, sls[h]] = (_head_rms(o[h], nw) * _silu(z_ref[rows, sls[h]])).astype(o_ref.dtype)
    for h in hs:
        state[h] = s_cur[h]


def _gdn_chunks(q, k, v, gb, z_b, norm_w):
    t = q.shape[0]
    c = GDN_CHUNK
    n_chunk = t // c
    heads = GDN_HEADS
    wide = pl.BlockSpec((c, GDN_VAL_WIDTH), lambda n: (n, 0))
    u, w, qe, ke, attn, eg = pl.pallas_call(
        functools.partial(_gdn_intra_kernel, heads=heads),
        grid=(n_chunk,),
        in_specs=[wide, wide, wide, pl.BlockSpec((c, LANES), lambda n: (n, 0))],
        out_specs=[wide, wide, wide, wide, wide, pl.BlockSpec((1, SUBLANES, LANES), lambda n: (n, 0, 0))],
        out_shape=[jax.ShapeDtypeStruct((t, GDN_VAL_WIDTH), F32)]
        + [jax.ShapeDtypeStruct((t, GDN_VAL_WIDTH), BF16)] * 4
        + [jax.ShapeDtypeStruct((n_chunk, SUBLANES, LANES), F32)],
        compiler_params=_cparams(("parallel",), 40),
        name="gdn_intra",
    )(q, k, v, gb)
    eg_flat = eg[:, 0, :heads].reshape(n_chunk * heads)
    cps = 2 if n_chunk % 2 == 0 else 1
    wide = pl.BlockSpec((cps * c, GDN_VAL_WIDTH), lambda n: (n, 0))
    return pl.pallas_call(
        functools.partial(_gdn_scan_kernel, heads=heads, cps=cps),
        grid=(n_chunk // cps,),
        in_specs=[pl.BlockSpec(memory_space=pltpu.SMEM), wide, wide, wide, wide, wide, wide,
                  pl.BlockSpec((1, GDN_DV), lambda n: (0, 0))],
        out_specs=wide,
        out_shape=jax.ShapeDtypeStruct((t, GDN_VAL_WIDTH), BF16),
        scratch_shapes=[pltpu.VMEM((heads, GDN_DK, GDN_DV), F32)],
        compiler_params=_cparams(("arbitrary",), 40),
        name="gdn_scan",
    )(eg_flat, u, w, qe, ke, attn, z_b, norm_w.reshape(1, GDN_DV))


def _short_conv_kernel(bg_ref, cg_ref, x_ref, z_ref, cgh_ref, xh_ref, cw_ref, o_ref):
    i = pl.program_id(0)
    cur = cg_ref[...] * x_ref[...]
    halo = jnp.where(i > 0, cgh_ref[...] * xh_ref[...], 0.0)
    y = _causal_conv(cur, halo, cw_ref, SC_CONV)
    o_ref[...] = (bg_ref[...] * y * _silu(z_ref[...])).astype(o_ref.dtype)


def _short_conv(p_sc, conv_w, tm=256):
    t = p_sc.shape[0]
    tm = min(tm, t)
    hb = tm // SUBLANES
    main = lambda c: pl.BlockSpec((tm, SC_WIDTH), functools.partial(lambda i, c: (i, c), c=c))
    halo = lambda c: pl.BlockSpec((SUBLANES, SC_WIDTH),
                                  functools.partial(lambda i, c: (jnp.maximum(i * hb - 1, 0), c), c=c))
    return pl.pallas_call(
        _short_conv_kernel,
        grid=(t // tm,),
        in_specs=[main(0), main(1), main(2), main(3), halo(1), halo(2),
                  pl.BlockSpec((SC_CONV, SC_WIDTH), lambda i: (0, 0))],
        out_specs=pl.BlockSpec((tm, SC_WIDTH), lambda i: (i, 0)),
        out_shape=jax.ShapeDtypeStruct((t, SC_WIDTH), BF16),
        compiler_params=_cparams(("parallel",), 40),
        name="short_conv",
    )(p_sc, p_sc, p_sc, p_sc, p_sc, p_sc, conv_w)


_IN_GROUPS = (("nsa", _OFF_QA, _OFF_GATE), ("za", _OFF_ZA, _OFF_QB), ("gdn", _OFF_QB, _OFF_AB),
              ("zb", _OFF_ZB, _OFF_SC), ("sc", _OFF_SC, _OFF_MERGE), ("merge", _OFF_MERGE, None))


def _stage_w_in(w_in):
    depth, d, _ = w_in.shape
    w_t = jnp.transpose(w_in, (0, 2, 1)).astype(BF16)
    small_pad = LANES - 2 * GDN_HEADS - 3 * NSA_HEADS
    narrow = jnp.concatenate([w_in[:, :, _OFF_AB:_OFF_ZB], w_in[:, :, _OFF_GATE:_OFF_ZA],
                              jnp.zeros((depth, d, small_pad), F32)], axis=2)
    w_small_t = jnp.transpose(narrow, (0, 2, 1)).astype(BF16)
    return w_t, w_small_t


def _layer(x, layer, norm_w, w_t, w_small_t, nsa_qk_norm, cmp_pos, cmp_w1, cmp_w2, gdn_conv_w, gdn_a_log,
           gdn_dt_bias, gdn_norm_w, sc_conv_w, wb_nsa, wb_gdn, wb_sc, w_out, slopes):
    hn = _rmsnorm(x, norm_w)
    proj = {}
    for name, a, b in _IN_GROUPS:
        n = (w_t.shape[1] if b is None else b) - a
        proj[name] = _matmul_nt(hn, w_t, layer, a, n, F32, 1024, 1024, "in_proj_" + name)
    small = _matmul_nt(hn, w_small_t, layer, 0, LANES, F32, 1024, LANES, "in_proj_small")
    p_nsa, z_a, p_gdn, z_b, p_sc, p_mg = (proj[n] for n, _, _ in _IN_GROUPS)

    q, kv_c, k_s, v_s, k_w, v_w = _nsa_prep(p_nsa, nsa_qk_norm)
    kv_cmp = _compress(kv_c, cmp_pos, cmp_w1, cmp_w2, nsa_qk_norm[1])
    o_cmp, sel, blk_any = _cmp_attention(q, kv_cmp, slopes)
    o_slc = _sel_attention(q, k_s, v_s, sel, blk_any, slopes)
    o_win = _win_attention(q, k_w, v_w, slopes)
    o_a = _nsa_combine(small, o_cmp, o_slc, o_win, z_a)

    q_b, k_b, v_b, gb = _gdn_prep(p_gdn, small, gdn_conv_w, gdn_a_log, gdn_dt_bias)
    o_b = _gdn_chunks(q_b, k_b, v_b, gb, z_b, gdn_norm_w)

    o_c = _short_conv(p_sc, sc_conv_w)

    merged = _merge(o_a, o_b, o_c, wb_nsa, wb_gdn, wb_sc, layer, p_mg)
    return _matmul_residual(merged, w_out, layer, x, 1024, 512)


def kernel(x, norm_w, w_in, nsa_qk_norm, cmp_pos, cmp_w1, cmp_w2, gdn_conv_w, gdn_a_log, gdn_dt_bias,
           gdn_norm_w, sc_conv_w, w_branch_nsa, w_branch_gdn, w_branch_sc, w_out):
    b, t, d = x.shape
    depth = norm_w.shape[0]
    heads = jnp.arange(1, NSA_HEADS + 1, dtype=F32)
    slopes = jnp.exp2(-8.0 * heads / NSA_HEADS)
    w_t, w_small_t = _stage_w_in(w_in)
    wb_nsa, wb_gdn, wb_sc, w_out16 = (w.astype(BF16) for w in (w_branch_nsa, w_branch_gdn, w_branch_sc, w_out))
    outs = []
    for bi in range(b):
        xb = x[bi]
        for l in range(depth):
            xb = _layer(xb, l, norm_w[l], w_t, w_small_t, nsa_qk_norm[l], cmp_pos[l], cmp_w1[l], cmp_w2[l],
                        gdn_conv_w[l], gdn_a_log[l], gdn_dt_bias[l], gdn_norm_w[l], sc_conv_w[l],
                        wb_nsa, wb_gdn, wb_sc, w_out16, slopes)
        outs.append(xb)
    return jnp.stack(outs, axis=0)
```

```python
import functools
import math

import jax
import jax.numpy as jnp
from jax import lax
from jax.experimental import pallas as pl
from jax.experimental.pallas import tpu as pltpu

F32 = jnp.float32
BF16 = jnp.bfloat16

HEAD_DIM = 128
NSA_HEADS = 16
NSA_KV_HEADS = 4
NSA_GROUP = NSA_HEADS // NSA_KV_HEADS
NSA_WIDTH = NSA_HEADS * HEAD_DIM
NSA_KV_WIDTH = NSA_KV_HEADS * HEAD_DIM
CMP_BLOCK = 32
CMP_STRIDE = 16
CMP_HIDDEN = 256
SEL_BLOCK = 64
SEL_TOPK = 16
WINDOW = 512
GDN_HEADS = 16
GDN_DK = 128
GDN_DV = 128
GDN_KEY_WIDTH = GDN_HEADS * GDN_DK
GDN_VAL_WIDTH = GDN_HEADS * GDN_DV
GDN_CONV = 4
GDN_CHUNK = 64
SC_WIDTH = 2048
SC_CONV = 3
N_BRANCH = 3
NORM_EPS = 1e-6
NEG_INF = -1e30
MASK_BIAS = -2e30
FORCE_SCORE = 1e6

LANES = 128
SUBLANES = 8
BF16_SUBLANES = 16
N_BLK_LANES = 128
CMP_KEY_STEP = 128
MIB = 1024 * 1024

_OFF_QA = 0
_OFF_KVC = _OFF_QA + NSA_WIDTH
_OFF_KVS = _OFF_KVC + 2 * NSA_KV_WIDTH
_OFF_KVW = _OFF_KVS + 2 * NSA_KV_WIDTH
_OFF_GATE = _OFF_KVW + 2 * NSA_KV_WIDTH
_OFF_ZA = _OFF_GATE + 3 * NSA_HEADS
_OFF_QB = _OFF_ZA + NSA_WIDTH
_OFF_AB = _OFF_QB + 2 * GDN_KEY_WIDTH + GDN_VAL_WIDTH
_OFF_BETA = _OFF_AB + GDN_HEADS
_OFF_ZB = _OFF_BETA + GDN_HEADS
_OFF_SC = _OFF_ZB + GDN_VAL_WIDTH
_OFF_MERGE = _OFF_SC + 4 * SC_WIDTH

_SM_A = 0
_SM_BETA = GDN_HEADS
_SM_GATE = 2 * GDN_HEADS


def _cparams(sem, vmem_mib):
    return pltpu.CompilerParams(dimension_semantics=sem, vmem_limit_bytes=vmem_mib * MIB)


def _sigmoid(x):
    return jax.nn.sigmoid(x)


def _silu(x):
    return x * jax.nn.sigmoid(x)


def _dot(a, b):
    return jnp.dot(a, b, preferred_element_type=F32)


def _dot_nt(a, b):
    return lax.dot_general(a, b, (((1,), (1,)), ((), ())), preferred_element_type=F32)


def _dot_tn(a, b):
    return lax.dot_general(a, b, (((0,), (0,)), ((), ())), preferred_element_type=F32)


def _head_rms(x, w):
    return x * lax.rsqrt(jnp.mean(x * x, axis=-1, keepdims=True) + NORM_EPS) * w


def _rmsnorm_kernel(x_ref, w_ref, o_ref):
    x = x_ref[...]
    y = x * lax.rsqrt(jnp.mean(x * x, axis=-1, keepdims=True) + NORM_EPS)
    o_ref[...] = (y * w_ref[...]).astype(o_ref.dtype)


def _rmsnorm(x, w, tm=256):
    t, d = x.shape
    return pl.pallas_call(
        _rmsnorm_kernel,
        grid=(t // tm,),
        in_specs=[pl.BlockSpec((tm, d), lambda i: (i, 0)), pl.BlockSpec((1, d), lambda i: (0, 0))],
        out_specs=pl.BlockSpec((tm, d), lambda i: (i, 0)),
        out_shape=jax.ShapeDtypeStruct((t, d), BF16),
        compiler_params=_cparams(("parallel",), 32),
        name="rmsnorm",
    )(x, w.reshape(1, d))


def _mm_kernel(a_ref, b_ref, o_ref):
    o_ref[...] = _dot(a_ref[...], b_ref[...]).astype(o_ref.dtype)


def _mm_nt_kernel(a_ref, b_ref, o_ref):
    o_ref[...] = _dot_nt(a_ref[...], b_ref[...]).astype(o_ref.dtype)


def _mm_nt_cast_kernel(a_ref, b_ref, o_ref, b16):
    @pl.when(pl.program_id(1) == 0)
    def _():
        b16[...] = b_ref[...].astype(BF16)

    o_ref[...] = _dot_nt(a_ref[...], b16[...]).astype(o_ref.dtype)


def _matmul_nt_f32w(a, b_t, layer, row_off, n, out_dtype, tm, tn, name):
    m, k = a.shape
    tm = min(tm, m)
    tn = min(tn, n)
    depth, rows, _ = b_t.shape
    first_row = layer * rows + row_off
    assert first_row % SUBLANES == 0 and n % tn == 0
    b_t = b_t.reshape(depth * rows, k)
    return pl.pallas_call(
        _mm_nt_cast_kernel,
        grid=(n // tn, m // tm),
        in_specs=[pl.BlockSpec((tm, k), lambda j, i: (i, 0)),
                  pl.BlockSpec((pl.Element(tn), pl.Element(k)),
                               lambda j, i: (pl.multiple_of(first_row + j * tn, SUBLANES), 0))],
        out_specs=pl.BlockSpec((tm, tn), lambda j, i: (i, j)),
        out_shape=jax.ShapeDtypeStruct((m, n), out_dtype),
        scratch_shapes=[pltpu.VMEM((tn, k), BF16)],
        compiler_params=_cparams(("parallel", "arbitrary"), 48),
        name=name,
    )(a, b_t)


def _matmul_nt(a, b_t, layer, row_off, n, out_dtype, tm, tn, name):
    m, k = a.shape
    tm = min(tm, m)
    tn = min(tn, n)
    depth, rows, _ = b_t.shape
    first_row = layer * rows + row_off
    assert first_row % BF16_SUBLANES == 0 and n % tn == 0
    b_t = b_t.reshape(depth * rows, k)
    return pl.pallas_call(
        _mm_nt_kernel,
        grid=(m // tm, n // tn),
        in_specs=[pl.BlockSpec((tm, k), lambda i, j: (i, 0)),
                  pl.BlockSpec((pl.Element(tn), pl.Element(k)),
                               lambda i, j: (pl.multiple_of(first_row + j * tn, BF16_SUBLANES), 0))],
        out_specs=pl.BlockSpec((tm, tn), lambda i, j: (i, j)),
        out_shape=jax.ShapeDtypeStruct((m, n), out_dtype),
        compiler_params=_cparams(("parallel", "arbitrary"), 52),
        name=name,
    )(a, b_t)


def _mm_res_kernel(a_ref, b_ref, r_ref, o_ref):
    o_ref[...] = r_ref[...] + _dot(a_ref[...], b_ref[...])


def _matmul_residual(a, b, layer, r, tm, tn):
    m, k = a.shape
    n = b.shape[2]
    tm = min(tm, m)
    return pl.pallas_call(
        _mm_res_kernel,
        grid=(m // tm, n // tn),
        in_specs=[pl.BlockSpec((tm, k), lambda i, j: (i, 0)),
                  pl.BlockSpec((None, k, tn), lambda i, j: (layer, 0, j)),
                  pl.BlockSpec((tm, tn), lambda i, j: (i, j))],
        out_specs=pl.BlockSpec((tm, tn), lambda i, j: (i, j)),
        out_shape=jax.ShapeDtypeStruct((m, n), F32),
        compiler_params=_cparams(("parallel", "arbitrary"), 48),
        name="out_proj_residual",
    )(a, b, r)


def _merge_kernel(oa_ref, ob_ref, oc_ref, wa_ref, wb_ref, wc_ref, ga_ref, gb_ref, gc_ref, o_ref):
    half = o_ref.shape[1] // 2
    cols = [slice(0, half), slice(half, 2 * half)]
    prods = [[_dot(a_ref[...], w_ref[:, c]) for a_ref, w_ref in ((oa_ref, wa_ref), (ob_ref, wb_ref), (oc_ref, wc_ref))]
             for c in cols]
    for c, (pa, pb, pc) in zip(cols, prods):
        acc = _sigmoid(ga_ref[:, c]) * pa
        acc = acc + _sigmoid(gb_ref[:, c]) * pb
        acc = acc + _sigmoid(gc_ref[:, c]) * pc
        o_ref[:, c] = acc.astype(o_ref.dtype)


def _merge(o_a, o_b, o_c, wa, wb, wc, layer, gates, tm=512, tn=512):
    m, k = o_a.shape
    n = wa.shape[2]
    tm = min(tm, m)
    nb = n // tn
    a_spec = pl.BlockSpec((tm, k), lambda i, j: (i, 0))
    w_spec = pl.BlockSpec((None, k, tn), lambda i, j: (layer, 0, j))
    g_specs = [pl.BlockSpec((tm, tn), functools.partial(lambda i, j, br: (i, br * nb + j), br=br))
               for br in range(N_BRANCH)]
    return pl.pallas_call(
        _merge_kernel,
        grid=(m // tm, nb),
        in_specs=[a_spec, a_spec, a_spec, w_spec, w_spec, w_spec] + g_specs,
        out_specs=pl.BlockSpec((tm, tn), lambda i, j: (i, j)),
        out_shape=jax.ShapeDtypeStruct((m, n), BF16),
        compiler_params=_cparams(("parallel", "arbitrary"), 48),
        name="branch_merge",
    )(o_a, o_b, o_c, wa, wb, wc, gates, gates, gates)


def _nsa_prep_kernel(p_ref, nw_ref, q_ref, kvc_ref, ks_ref, vs_ref, kw_ref, vw_ref):
    nw = nw_ref[...]
    for h in range(NSA_HEADS):
        sl = slice(h * HEAD_DIM, (h + 1) * HEAD_DIM)
        q_ref[:, sl] = (_head_rms(p_ref[:, sl], nw[0:1]) * (HEAD_DIM ** -0.5)).astype(BF16)
    kvc_ref[...] = p_ref[:, _OFF_KVC:_OFF_KVS]
    for g in range(NSA_KV_HEADS):
        sl = slice(g * HEAD_DIM, (g + 1) * HEAD_DIM)
        ks = p_ref[:, _OFF_KVS + g * HEAD_DIM:_OFF_KVS + (g + 1) * HEAD_DIM]
        ks_ref[:, sl] = _head_rms(ks, nw[2:3]).astype(BF16)
        vs_ref[:, sl] = p_ref[:, _OFF_KVS + NSA_KV_WIDTH + g * HEAD_DIM:
                              _OFF_KVS + NSA_KV_WIDTH + (g + 1) * HEAD_DIM].astype(BF16)
        kw = p_ref[:, _OFF_KVW + g * HEAD_DIM:_OFF_KVW + (g + 1) * HEAD_DIM]
        kw_ref[:, sl] = _head_rms(kw, nw[3:4]).astype(BF16)
        vw_ref[:, sl] = p_ref[:, _OFF_KVW + NSA_KV_WIDTH + g * HEAD_DIM:
                              _OFF_KVW + NSA_KV_WIDTH + (g + 1) * HEAD_DIM].astype(BF16)


def _nsa_prep(p_nsa, qk_norm, tm=256):
    t, width = p_nsa.shape
    row = lambda i: (i, 0)
    return pl.pallas_call(
        _nsa_prep_kernel,
        grid=(t // tm,),
        in_specs=[pl.BlockSpec((tm, width), row), pl.BlockSpec((4, HEAD_DIM), lambda i: (0, 0))],
        out_specs=[pl.BlockSpec((tm, NSA_WIDTH), row)] + [pl.BlockSpec((tm, 2 * NSA_KV_WIDTH), row)]
        + [pl.BlockSpec((tm, NSA_KV_WIDTH), row)] * 4,
        out_shape=[jax.ShapeDtypeStruct((t, NSA_WIDTH), BF16), jax.ShapeDtypeStruct((t, 2 * NSA_KV_WIDTH), F32)]
        + [jax.ShapeDtypeStruct((t, NSA_KV_WIDTH), BF16)] * 4,
        compiler_params=_cparams(("parallel",), 40),
        name="nsa_prep",
    )(p_nsa, qk_norm)


def _compress_kernel(x_ref, pos_ref, w1lo_ref, w1hi_ref, w2_ref, nw_ref, o_ref, lo_acc, hi_acc, *, n_chunk):
    kv = pl.program_id(0)
    l = pl.program_id(1)

    @pl.when(l == 0)
    def _():
        lo_acc[...] = jnp.zeros(lo_acc.shape, F32)
        hi_acc[...] = jnp.zeros(hi_acc.shape, F32)

    pos_lo = pos_ref[0, pl.ds(l, 1), :]
    pos_hi = pos_ref[0, pl.ds(l + CMP_STRIDE, 1), :]
    w_lo = w1lo_ref[0, 0]
    w_hi = w1hi_ref[0, 0]
    for g in range(NSA_KV_HEADS):
        x = x_ref[:, g * HEAD_DIM:(g + 1) * HEAD_DIM]
        lo_acc[g] += _dot((x + pos_lo).astype(BF16), w_lo)
        hi_acc[g] += _dot((x + pos_hi).astype(BF16), w_hi)

    @pl.when(l == CMP_STRIDE - 1)
    def _():
        for g in range(NSA_KV_HEADS):
            hidden = _silu(lo_acc[g] + pltpu.roll(hi_acc[g], shift=n_chunk - 1, axis=0))
            out = _dot(hidden.astype(BF16), w2_ref[0])
            normed = _head_rms(out, nw_ref[...])
            o_ref[0, g] = jnp.where(kv == 0, normed, out).astype(BF16)


def _compress(kv_c, cmp_pos, cmp_w1, cmp_w2, k_norm_w):
    t = kv_c.shape[0]
    n_chunk = t // CMP_STRIDE
    x = kv_c.reshape(n_chunk, CMP_STRIDE * 2 * NSA_KV_WIDTH)
    w1 = cmp_w1.astype(BF16)
    w2 = cmp_w2.astype(BF16)
    return pl.pallas_call(
        functools.partial(_compress_kernel, n_chunk=n_chunk),
        grid=(2, CMP_STRIDE),
        in_specs=[pl.BlockSpec((n_chunk, NSA_KV_WIDTH), lambda a, l: (0, 2 * l + a)),
                  pl.BlockSpec((1, CMP_BLOCK, HEAD_DIM), lambda a, l: (a, 0, 0)),
                  pl.BlockSpec((1, 1, HEAD_DIM, CMP_HIDDEN), lambda a, l: (a, l, 0, 0)),
                  pl.BlockSpec((1, 1, HEAD_DIM, CMP_HIDDEN), lambda a, l: (a, l + CMP_STRIDE, 0, 0)),
                  pl.BlockSpec((1, CMP_HIDDEN, HEAD_DIM), lambda a, l: (a, 0, 0)),
                  pl.BlockSpec((1, HEAD_DIM), lambda a, l: (0, 0))],
        out_specs=pl.BlockSpec((1, NSA_KV_HEADS, n_chunk, HEAD_DIM), lambda a, l: (a, 0, 0, 0)),
        out_shape=jax.ShapeDtypeStruct((2, NSA_KV_HEADS, n_chunk, HEAD_DIM), BF16),
        scratch_shapes=[pltpu.VMEM((NSA_KV_HEADS, n_chunk, CMP_HIDDEN), F32),
                        pltpu.VMEM((NSA_KV_HEADS, n_chunk, CMP_HIDDEN), F32)],
        compiler_params=_cparams(("arbitrary", "arbitrary"), 40),
        name="nsa_compress",
    )(x, cmp_pos, w1, w1, w2, k_norm_w.reshape(1, HEAD_DIM))


def _cmp_attn_kernel(slopes_ref, q_ref, k_ref, v_ref, cov_ref, o_ref, sel_ref, any_ref, imp_sc, *, tq, n_chunk):
    g = pl.program_id(0)
    qi = pl.program_id(1)
    t0 = qi * tq
    sls = [slice(z * HEAD_DIM, (z + 1) * HEAD_DIM) for z in range(NSA_GROUP)]

    def attend(nk):
        k = k_ref[0, 0, 0:nk, :]
        v_t = v_ref[0, 0, 0:nk, :].T
        t_pos = t0 + lax.broadcasted_iota(jnp.int32, (nk, tq), 1)
        k_end = lax.broadcasted_iota(jnp.int32, (nk, tq), 0) * CMP_STRIDE + (CMP_BLOCK - 1)
        mask_bias = jnp.where(t_pos >= k_end, 0.0, MASK_BIAS)
        k_rel = (k_end - t0).astype(F32)
        psum = jnp.zeros((nk, tq), F32)
        qk_next = _dot_nt(k, q_ref[:, sls[0]])
        pending = None
        for z in range(NSA_GROUP):
            qk = qk_next
            if z + 1 < NSA_GROUP:
                qk_next = _dot_nt(k, q_ref[:, sls[z + 1]])
            s = qk + (mask_bias + slopes_ref[g * NSA_GROUP + z] * k_rel)
            m = jnp.maximum(jnp.max(s, axis=0, keepdims=True), NEG_INF)
            e = jnp.exp(s - m)
            p = e * (1.0 / jnp.maximum(jnp.sum(e, axis=0, keepdims=True), 1e-30))
            psum = psum + p
            if pending is not None:
                o_ref[:, sls[pending[0]]] = _dot(v_t, pending[1]).T
            pending = (z, p.astype(BF16))
        o_ref[:, sls[pending[0]]] = _dot(v_t, pending[1]).T
        p_hi = psum.astype(BF16)
        p_lo = (psum - p_hi.astype(F32)).astype(BF16)
        cov_t = cov_ref[:, 0:nk]
        imp_sc[...] = _dot(cov_t, p_hi) + _dot(cov_t, p_lo)

    visible = (t0 + tq - CMP_BLOCK) // CMP_STRIDE + 1
    n_var = n_chunk // CMP_KEY_STEP if n_chunk % CMP_KEY_STEP == 0 else 1
    if n_var <= 1:
        attend(n_chunk)
    else:
        for var in range(1, n_var + 1):
            lo, hi = (var - 1) * CMP_KEY_STEP, var * CMP_KEY_STEP
            if var == 1:
                cond = visible <= hi
            elif var < n_var:
                cond = (visible > lo) & (visible <= hi)
            else:
                cond = visible > lo
            pl.when(cond)(functools.partial(attend, hi))
    imp = imp_sc[...]

    j = lax.broadcasted_iota(jnp.int32, (N_BLK_LANES, tq), 0)
    cur = (t0 + lax.broadcasted_iota(jnp.int32, (N_BLK_LANES, tq), 1)) // SEL_BLOCK
    valid = j <= cur
    forced = (j == 0) | (j == cur) | (j == cur - 1)
    chosen_first = valid & forced
    val = jnp.where(valid, jnp.where(forced, -2.0, imp), -1.0)
    sel = chosen_first.astype(F32)
    jf = j.astype(F32)
    for _ in range(SEL_TOPK - 3):
        m = jnp.max(val, axis=0, keepdims=True)
        idx = jnp.min(jnp.where(val == m, jf, float(N_BLK_LANES)), axis=0, keepdims=True)
        hit = jf == idx
        sel = jnp.where(hit & (m > -0.5), 1.0, sel)
        val = jnp.where(hit, -2.0, val)
    sel_q = sel.T
    sel_ref[0] = sel_q.astype(BF16)
    any_ref[0, 0] = jnp.broadcast_to(jnp.max(sel_q, axis=0, keepdims=True), (SUBLANES, N_BLK_LANES))


def _cmp_attention(q, kv_cmp, slopes, tq=256):
    t = q.shape[0]
    n_chunk = kv_cmp.shape[2]
    n_cmp = n_chunk - CMP_BLOCK // CMP_STRIDE + 1
    n_blk = t // SEL_BLOCK
    assert n_blk <= N_BLK_LANES
    tq = min(tq, t)
    cs = jnp.arange(n_chunk)[:, None] * CMP_STRIDE
    bs = jnp.arange(N_BLK_LANES)[None, :] * SEL_BLOCK
    cover = ((cs <= bs + SEL_BLOCK - 1) & (cs + CMP_BLOCK - 1 >= bs)
             & (jnp.arange(n_chunk)[:, None] < n_cmp) & (jnp.arange(N_BLK_LANES)[None, :] < n_blk))
    cover = cover.astype(BF16).T
    return pl.pallas_call(
        functools.partial(_cmp_attn_kernel, tq=tq, n_chunk=n_chunk),
        grid=(NSA_KV_HEADS, t // tq),
        in_specs=[pl.BlockSpec(memory_space=pltpu.SMEM),
                  pl.BlockSpec((tq, NSA_GROUP * HEAD_DIM), lambda g, i: (i, g)),
                  pl.BlockSpec((1, 1, n_chunk, HEAD_DIM), lambda g, i: (0, g, 0, 0)),
                  pl.BlockSpec((1, 1, n_chunk, HEAD_DIM), lambda g, i: (1, g, 0, 0)),
                  pl.BlockSpec((N_BLK_LANES, n_chunk), lambda g, i: (0, 0))],
        out_specs=[pl.BlockSpec((tq, NSA_GROUP * HEAD_DIM), lambda g, i: (i, g)),
                   pl.BlockSpec((1, tq, N_BLK_LANES), lambda g, i: (g, i, 0)),
                   pl.BlockSpec((1, 1, SUBLANES, N_BLK_LANES), lambda g, i: (g, i, 0, 0))],
        out_shape=[jax.ShapeDtypeStruct((t, NSA_WIDTH), F32),
                   jax.ShapeDtypeStruct((NSA_KV_HEADS, t, N_BLK_LANES), BF16),
                   jax.ShapeDtypeStruct((NSA_KV_HEADS, t // tq, SUBLANES, N_BLK_LANES), F32)],
        scratch_shapes=[pltpu.VMEM((N_BLK_LANES, tq), F32)],
        compiler_params=_cparams(("parallel", "parallel"), 40),
        name="nsa_cmp_attn",
    )(slopes, q, kv_cmp, kv_cmp, cover)


def _sel_attn_kernel(cnt_ref, kjs_ref, slopes_ref, q_ref, k_ref, v_ref, eb_ref, sel_ref, o_ref,
                     qa_sc, m_sc, l_sc, acc_sc, *, tq, tk, max_tiles):
    g = pl.program_id(0)
    qi = pl.program_id(1)
    item = g * pl.num_programs(1) + qi
    t0 = qi * tq

    m_sc[...] = jnp.full(m_sc.shape, NEG_INF, F32)
    l_sc[...] = jnp.zeros(l_sc.shape, F32)
    acc_sc[...] = jnp.zeros(acc_sc.shape, F32)
    unselected = sel_ref[0] - 1.0
    for z in range(NSA_GROUP):
        qa_sc[z] = jnp.concatenate([q_ref[:, z * HEAD_DIM:(z + 1) * HEAD_DIM], unselected], axis=1)

    def tile(kj, causal):
        k0 = kj * tk
        rows = pl.ds(pl.multiple_of(k0, tk), tk)
        k_aug = jnp.concatenate([k_ref[rows, :], eb_ref[kj]], axis=1)
        v_t = v_ref[rows, :].T
        row = lax.broadcasted_iota(jnp.int32, (tk, tq), 0)
        k_rel = (k0 - t0 + row).astype(F32)
        if causal:
            col = lax.broadcasted_iota(jnp.int32, (tk, tq), 1)
            causal_bias = jnp.where(t0 + col >= k0 + row, 0.0, MASK_BIAS)
        qk_next = _dot_nt(k_aug, qa_sc[0])
        pending = None
        for z in range(NSA_GROUP):
            qk = qk_next
            if z + 1 < NSA_GROUP:
                qk_next = _dot_nt(k_aug, qa_sc[z + 1])
            s = qk + slopes_ref[g * NSA_GROUP + z] * k_rel
            if causal:
                s = s + causal_bias
            m_prev = m_sc[z]
            m_new = jnp.maximum(m_prev, jnp.max(s, axis=0, keepdims=True))
            alpha = jnp.exp(m_prev - m_new)
            e = jnp.exp(s - m_new)
            l_sc[z] = alpha * l_sc[z] + jnp.sum(e, axis=0, keepdims=True)
            m_sc[z] = m_new
            if pending is not None:
                pz, palpha, pe = pending
                acc_sc[pz] = palpha * acc_sc[pz] + _dot(v_t, pe)
            pending = (z, alpha, e.astype(BF16))
        pz, palpha, pe = pending
        acc_sc[pz] = palpha * acc_sc[pz] + _dot(v_t, pe)

    def visit(i, carry):
        tile(kjs_ref[item * max_tiles + i], False)
        return carry

    lax.fori_loop(0, cnt_ref[item], visit, 0)
    tile(t0 // tk, True)

    for z in range(NSA_GROUP):
        out_t = acc_sc[z] * (1.0 / jnp.maximum(l_sc[z], 1e-30))
        o_ref[:, z * HEAD_DIM:(z + 1) * HEAD_DIM] = out_t.T


def _sel_work_list(blk_any, t, tq, tk):
    nq, nk, bpt = t // tq, t // tk, tk // SEL_BLOCK
    n_blk = t // SEL_BLOCK
    picked = blk_any[:, :, 0, :n_blk].reshape(NSA_KV_HEADS, nq, nk, bpt).max(axis=-1) > 0.5
    diagonal = (jnp.arange(nq) * tq) // tk
    before = picked & (jnp.arange(nk)[None, :] < diagonal[:, None])[None]
    count = before.sum(axis=-1).astype(jnp.int32)
    tiles = jnp.argsort(jnp.logical_not(before), axis=-1, stable=True).astype(jnp.int32)
    return count.reshape(-1), tiles.reshape(-1)


def _sel_attention(q, k_s, v_s, sel, blk_any, slopes, tq=256, tk=512):
    t = q.shape[0]
    tq = min(tq, t)
    tk = min(tk, t)
    assert tk % tq == 0
    nk = t // tk
    count, tiles = _sel_work_list(blk_any, t, tq, tk)

    blk_of_key = jnp.arange(t)[:, None] // SEL_BLOCK
    block_onehot = jnp.where(blk_of_key == jnp.arange(N_BLK_LANES)[None, :], -MASK_BIAS, 0.0)
    block_onehot = block_onehot.astype(BF16).reshape(nk, tk, N_BLK_LANES)

    smem = pl.BlockSpec(memory_space=pltpu.SMEM)
    return pl.pallas_call(
        functools.partial(_sel_attn_kernel, tq=tq, tk=tk, max_tiles=nk),
        grid=(NSA_KV_HEADS, t // tq),
        in_specs=[smem, smem, smem,
                  pl.BlockSpec((tq, NSA_GROUP * HEAD_DIM), lambda g, i: (i, g)),
                  pl.BlockSpec((t, HEAD_DIM), lambda g, i: (0, g)),
                  pl.BlockSpec((t, HEAD_DIM), lambda g, i: (0, g)),
                  pl.BlockSpec((nk, tk, N_BLK_LANES), lambda g, i: (0, 0, 0)),
                  pl.BlockSpec((1, tq, N_BLK_LANES), lambda g, i: (g, i, 0))],
        out_specs=pl.BlockSpec((tq, NSA_GROUP * HEAD_DIM), lambda g, i: (i, g)),
        out_shape=jax.ShapeDtypeStruct((t, NSA_WIDTH), F32),
        scratch_shapes=[pltpu.VMEM((NSA_GROUP, tq, 2 * HEAD_DIM), BF16),
                        pltpu.VMEM((NSA_GROUP, 1, tq), F32), pltpu.VMEM((NSA_GROUP, 1, tq), F32),
                        pltpu.VMEM((NSA_GROUP, HEAD_DIM, tq), F32)],
        compiler_params=_cparams(("arbitrary", "arbitrary"), 40),
        name="nsa_sel_attn",
    )(count, tiles, slopes, q, k_s, v_s, block_onehot, sel)


def _win_attn_kernel(slopes_ref, q_ref, *refs, tq, n_tile):
    k_refs = refs[:n_tile]
    v_refs = refs[n_tile:2 * n_tile]
    o_ref = refs[2 * n_tile]
    g = pl.program_id(0)
    qi = pl.program_id(1)
    span = n_tile * tq
    k = jnp.concatenate([r[...] for r in k_refs], axis=0)
    v = jnp.concatenate([r[...] for r in v_refs], axis=0)
    t_pos = qi * tq + lax.broadcasted_iota(jnp.int32, (tq, span), 0)
    k_pos = (qi - (n_tile - 1)) * tq + lax.broadcasted_iota(jnp.int32, (tq, span), 1)
    dist = t_pos - k_pos
    mask_bias = jnp.where((dist >= 0) & (dist < WINDOW) & (k_pos >= 0), 0.0, MASK_BIAS)
    k_rel = (lax.broadcasted_iota(jnp.int32, (1, span), 1) - (n_tile - 1) * tq).astype(F32)
    sls = [slice(z * HEAD_DIM, (z + 1) * HEAD_DIM) for z in range(NSA_GROUP)]
    qk_next = _dot_nt(q_ref[:, sls[0]], k)
    pending = None
    for z in range(NSA_GROUP):
        qk = qk_next
        if z + 1 < NSA_GROUP:
            qk_next = _dot_nt(q_ref[:, sls[z + 1]], k)
        s = qk + (mask_bias + slopes_ref[g * NSA_GROUP + z] * k_rel)
        m = jnp.maximum(jnp.max(s, axis=-1, keepdims=True), NEG_INF)
        e = jnp.exp(s - m)
        inv_l = 1.0 / jnp.maximum(jnp.sum(e, axis=-1, keepdims=True), 1e-30)
        if pending is not None:
            o_ref[:, sls[pending[0]]] = _dot(pending[1], v) * pending[2]
        pending = (z, e.astype(BF16), inv_l)
    o_ref[:, sls[pending[0]]] = _dot(pending[1], v) * pending[2]


def _win_attention(q, k_w, v_w, slopes, tq=256):
    t = q.shape[0]
    tq = min(tq, t)
    n_tile = -(-WINDOW // tq) + 1

    def kv_spec(c):
        return pl.BlockSpec((tq, HEAD_DIM), lambda g, i: (jnp.maximum(i - (n_tile - 1) + c, 0), g))

    return pl.pallas_call(
        functools.partial(_win_attn_kernel, tq=tq, n_tile=n_tile),
        grid=(NSA_KV_HEADS, t // tq),
        in_specs=[pl.BlockSpec(memory_space=pltpu.SMEM),
                  pl.BlockSpec((tq, NSA_GROUP * HEAD_DIM), lambda g, i: (i, g))]
        + [kv_spec(c) for c in range(n_tile)] * 2,
        out_specs=pl.BlockSpec((tq, NSA_GROUP * HEAD_DIM), lambda g, i: (i, g)),
        out_shape=jax.ShapeDtypeStruct((t, NSA_WIDTH), F32),
        compiler_params=_cparams(("parallel", "parallel"), 40),
        name="nsa_win_attn",
    )(slopes, q, *([k_w] * n_tile), *([v_w] * n_tile))


def _nsa_combine_kernel(sm_ref, ex_ref, oc_ref, os_ref, ow_ref, z_ref, o_ref):
    logits = sm_ref[...]
    hi = logits.astype(BF16)
    lo = (logits - hi.astype(F32)).astype(BF16)
    acc = None
    for br, branch_ref in enumerate((oc_ref, os_ref, ow_ref)):
        ex = ex_ref[br]
        gate = _sigmoid(_dot(hi, ex) + _dot(lo, ex))
        term = gate * branch_ref[...]
        acc = term if acc is None else acc + term
    o_ref[...] = (acc * _silu(z_ref[...])).astype(o_ref.dtype)


def _nsa_combine(small, o_cmp, o_slc, o_win, z_a, tm=256):
    t = small.shape[0]
    tm = min(tm, t)
    lane = jnp.arange(LANES)[None, :, None]
    br = jnp.arange(N_BRANCH)[:, None, None]
    head = (jnp.arange(NSA_WIDTH) // HEAD_DIM)[None, None, :]
    expand = (lane == _SM_GATE + head * N_BRANCH + br).astype(BF16)
    row = lambda i: (i, 0)
    wide = pl.BlockSpec((tm, NSA_WIDTH), row)
    return pl.pallas_call(
        _nsa_combine_kernel,
        grid=(t // tm,),
        in_specs=[pl.BlockSpec((tm, LANES), row),
                  pl.BlockSpec((N_BRANCH, LANES, NSA_WIDTH), lambda i: (0, 0, 0)),
                  wide, wide, wide, wide],
        out_specs=wide,
        out_shape=jax.ShapeDtypeStruct((t, NSA_WIDTH), BF16),
        compiler_params=_cparams(("parallel",), 40),
        name="nsa_combine",
    )(small, expand, o_cmp, o_slc, o_win, z_a)


def _shift_rows(cur, halo, s):
    rolled = pltpu.roll(cur, shift=s, axis=0)
    halo_rolled = pltpu.roll(halo, shift=s, axis=0)
    row = lax.broadcasted_iota(jnp.int32, halo.shape, 0)
    head = jnp.where(row < s, halo_rolled, rolled[0:SUBLANES])
    return jnp.concatenate([head, rolled[SUBLANES:]], axis=0)


def _causal_conv(cur, halo, w_ref, k):
    acc = None
    for j in range(k):
        s = k - 1 - j
        term = (cur if s == 0 else _shift_rows(cur, halo, s)) * w_ref[j:j + 1, :]
        acc = term if acc is None else acc + term
    return acc


def _gdn_prep_kernel(p_ref, halo_ref, cw_ref, sm_ref, alog_ref, dtb_ref, q_ref, k_ref, v_ref, gb_ref, xx):
    i = pl.program_id(0)
    tm = p_ref.shape[0]
    xx[0:SUBLANES, :] = jnp.where(i > 0, halo_ref[...], 0.0)
    xx[SUBLANES:, :] = p_ref[...]

    def conv_silu(cols):
        acc = None
        for j in range(GDN_CONV):
            delay = GDN_CONV - 1 - j
            term = xx[SUBLANES - delay:SUBLANES - delay + tm, cols] * cw_ref[j:j + 1, cols]
            acc = term if acc is None else acc + term
        return _silu(acc)

    for h in range(GDN_HEADS):
        sl = slice(h * GDN_DK, (h + 1) * GDN_DK)
        qh = conv_silu(sl)
        q_ref[:, sl] = qh * lax.rsqrt(jnp.sum(qh * qh, axis=-1, keepdims=True) + NORM_EPS) * (GDN_DK ** -0.5)
        kh = conv_silu(slice(GDN_KEY_WIDTH + h * GDN_DK, GDN_KEY_WIDTH + (h + 1) * GDN_DK))
        k_ref[:, sl] = kh * lax.rsqrt(jnp.sum(kh * kh, axis=-1, keepdims=True) + NORM_EPS)
        v_ref[:, sl] = conv_silu(slice(2 * GDN_KEY_WIDTH + h * GDN_DV, 2 * GDN_KEY_WIDTH + (h + 1) * GDN_DV))
    sm = sm_ref[...]
    lane = lax.broadcasted_iota(jnp.int32, sm.shape, 1)
    gdecay = -jnp.exp(alog_ref[...]) * jax.nn.softplus(sm + dtb_ref[...])
    beta = _sigmoid(sm)
    gb_ref[...] = jnp.where(lane < _SM_BETA, gdecay, jnp.where(lane < _SM_GATE, beta, 0.0))


def _gdn_prep(p_gdn, small, conv_w, a_log, dt_bias, tm=256):
    t, width = p_gdn.shape
    tm = min(tm, t)
    hb = tm // SUBLANES
    row = lambda i: (i, 0)
    alog = jnp.zeros((1, LANES), F32).at[0, _SM_A:_SM_A + GDN_HEADS].set(a_log)
    dtb = jnp.zeros((1, LANES), F32).at[0, _SM_A:_SM_A + GDN_HEADS].set(dt_bias)
    const = lambda i: (0, 0)
    return pl.pallas_call(
        _gdn_prep_kernel,
        grid=(t // tm,),
        in_specs=[pl.BlockSpec((tm, width), row),
                  pl.BlockSpec((SUBLANES, width), lambda i: (jnp.maximum(i * hb - 1, 0), 0)),
                  pl.BlockSpec((GDN_CONV, width), const),
                  pl.BlockSpec((tm, LANES), row),
                  pl.BlockSpec((1, LANES), const), pl.BlockSpec((1, LANES), const)],
        out_specs=[pl.BlockSpec((tm, GDN_KEY_WIDTH), row), pl.BlockSpec((tm, GDN_KEY_WIDTH), row),
                   pl.BlockSpec((tm, GDN_VAL_WIDTH), row), pl.BlockSpec((tm, LANES), row)],
        out_shape=[jax.ShapeDtypeStruct((t, GDN_KEY_WIDTH), F32), jax.ShapeDtypeStruct((t, GDN_KEY_WIDTH), F32),
                   jax.ShapeDtypeStruct((t, GDN_VAL_WIDTH), F32), jax.ShapeDtypeStruct((t, LANES), F32)],
        scratch_shapes=[pltpu.VMEM((tm + SUBLANES, width), F32)],
        compiler_params=_cparams(("parallel",), 48),
        name="gdn_prep",
    )(p_gdn, p_gdn, conv_w, small, alog, dtb)


def _row_pad(x):
    return jnp.concatenate([x, jnp.zeros_like(x)], axis=0)


def _gdn_intra_kernel(q_ref, k_ref, v_ref, gb_ref, u_ref, w_ref, qe_ref, ke_ref, at_ref, eg_ref, *, heads):
    c = GDN_CHUNK
    gb = gb_ref[...]
    row = lax.broadcasted_iota(jnp.int32, gb.shape, 0)
    gcum = gb
    shift = 1
    while shift < c:
        gcum = gcum + jnp.where(row >= shift, pltpu.roll(gcum, shift=shift, axis=0), 0.0)
        shift *= 2
    lane = lax.broadcasted_iota(jnp.int32, gb.shape, 1)
    ri = lax.broadcasted_iota(jnp.int32, (c, LANES), 0)
    ci = lax.broadcasted_iota(jnp.int32, (c, LANES), 1)
    tri = ri >= ci
    strict = ri > ci
    eye = ri == ci
    eyef = eye.astype(F32)
    lane8 = lax.broadcasted_iota(jnp.int32, (SUBLANES, LANES), 1)
    eg_all = jnp.zeros((SUBLANES, LANES), F32)
    hs = range(heads)
    sls = [slice(h * GDN_DK, (h + 1) * GDN_DK) for h in hs]

    gcol = [jnp.sum(jnp.where(lane == _SM_A + h, gcum, 0.0), axis=-1, keepdims=True) for h in hs]
    beta = [jnp.sum(jnp.where(lane == _SM_BETA + h, gb, 0.0), axis=-1, keepdims=True) for h in hs]
    decay = []
    for h in hs:
        gmat = jnp.broadcast_to(gcol[h], (c, LANES))
        grow = jnp.sum(jnp.where(eye, gmat, 0.0), axis=0, keepdims=True)
        decay.append(jnp.where(tri, jnp.exp(jnp.where(tri, gmat - grow, 0.0)), 0.0))
    glast = [gcol[h][c - 1:c, :] for h in hs]
    egc = [jnp.exp(gcol[h]) for h in hs]
    q = [q_ref[:, sls[h]] for h in hs]
    k = [k_ref[:, sls[h]] for h in hs]
    k16 = [k[h].astype(BF16) for h in hs]
    qk = [_dot_nt(jnp.concatenate([q[h].astype(BF16), k16[h]], axis=0), _row_pad(k16[h])) for h in hs]
    for h in hs:
        at_ref[:, sls[h]] = (qk[h][:c] * decay[h]).astype(BF16)

    pw = [-jnp.where(strict, beta[h] * qk[h][c:] * decay[h], 0.0) for h in hs]
    inv = [eyef + pw[h] for h in hs]
    pw16 = [pw[h].astype(BF16) for h in hs]
    pw = [_dot(pw16[h], _row_pad(pw16[h])) for h in hs]
    span = 2
    while span < c:
        pw16 = [pw[h].astype(BF16) for h in hs]
        if 2 * span < c:
            both = [_dot(jnp.concatenate([pw16[h], inv[h].astype(BF16)], axis=0), _row_pad(pw16[h])) for h in hs]
            pw = [both[h][:c] for h in hs]
            inv = [inv[h] + both[h][c:] for h in hs]
        else:
            inv = [inv[h] + _dot(inv[h].astype(BF16), _row_pad(pw16[h])) for h in hs]
        span *= 2

    uw = []
    for h in hs:
        rhs = jnp.concatenate([(v_ref[:, sls[h]] * beta[h]).astype(BF16),
                               (k[h] * (beta[h] * egc[h])).astype(BF16)], axis=1)
        uw.append(_dot(inv[h].astype(BF16), _row_pad(rhs)))
    for h in hs:
        u_ref[:, sls[h]] = uw[h][:, :GDN_DV]
        w_ref[:, sls[h]] = uw[h][:, GDN_DV:].astype(BF16)
        qe_ref[:, sls[h]] = (q[h] * egc[h]).astype(BF16)
        ke_ref[:, sls[h]] = (k[h] * jnp.exp(glast[h] - gcol[h])).astype(BF16)
        eg_all = jnp.where(lane8 == h, jnp.exp(glast[h]), eg_all)
    eg_ref[0] = eg_all


def _gdn_scan_kernel(eg_ref, u_ref, w_ref, qe_ref, ke_ref, at_ref, z_ref, nw_ref, o_ref, state, *, heads, cps):
    n = pl.program_id(0)
    c = GDN_CHUNK

    @pl.when(n == 0)
    def _():
        state[...] = jnp.zeros(state.shape, F32)

    nw = nw_ref[...]
    hs = range(heads)
    sls = [slice(h * GDN_DK, (h + 1) * GDN_DK) for h in hs]
    s_cur = [state[h] for h in hs]
    for cc in range(cps):
        rows = slice(cc * c, (cc + 1) * c)
        ws_qs = [_dot(jnp.concatenate([w_ref[rows, sls[h]], qe_ref[rows, sls[h]]], axis=0), s_cur[h].astype(BF16))
                 for h in hs]
        v_new16 = [(u_ref[rows, sls[h]] - ws_qs[h][:c]).astype(BF16) for h in hs]
        o = [ws_qs[h][c:] + _dot(at_ref[rows, h * GDN_DK:h * GDN_DK + c], v_new16[h]) for h in hs]
        s_cur = [s_cur[h] * eg_ref[(n * cps + cc) * heads + h] + _dot_tn(ke_ref[rows, sls[h]], v_new16[h])
                 for h in hs]
        for h in hs:
            o_ref[rows, sls[h]] = (_head_rms(o[h], nw) * _silu(z_ref[rows, sls[h]])).astype(o_ref.dtype)
    for h in hs:
        state[h] = s_cur[h]


def _gdn_chunks(q, k, v, gb, z_b, norm_w):
    t = q.shape[0]
    c = GDN_CHUNK
    n_chunk = t // c
    heads = GDN_HEADS
    wide = pl.BlockSpec((c, GDN_VAL_WIDTH), lambda n: (n, 0))
    u, w, qe, ke, attn, eg = pl.pallas_call(
        functools.partial(_gdn_intra_kernel, heads=heads),
        grid=(n_chunk,),
        in_specs=[wide, wide, wide, pl.BlockSpec((c, LANES), lambda n: (n, 0))],
        out_specs=[wide, wide, wide, wide, wide, pl.BlockSpec((1, SUBLANES, LANES), lambda n: (n, 0, 0))],
        out_shape=[jax.ShapeDtypeStruct((t, GDN_VAL_WIDTH), F32)]
        + [jax.ShapeDtypeStruct((t, GDN_VAL_WIDTH), BF16)] * 4
        + [jax.ShapeDtypeStruct((n_chunk, SUBLANES, LANES), F32)],
        compiler_params=_cparams(("parallel",), 40),
        name="gdn_intra",
    )(q, k, v, gb)
    eg_flat = eg[:, 0, :heads].reshape(n_chunk * heads)
    cps = 2 if n_chunk % 2 == 0 else 1
    wide = pl.BlockSpec((cps * c, GDN_VAL_WIDTH), lambda n: (n, 0))
    return pl.pallas_call(
        functools.partial(_gdn_scan_kernel, heads=heads, cps=cps),
        grid=(n_chunk // cps,),
        in_specs=[pl.BlockSpec(memory_space=pltpu.SMEM), wide, wide, wide, wide, wide, wide,
                  pl.BlockSpec((1, GDN_DV), lambda n: (0, 0))],
        out_specs=wide,
        out_shape=jax.ShapeDtypeStruct((t, GDN_VAL_WIDTH), BF16),
        scratch_shapes=[pltpu.VMEM((heads, GDN_DK, GDN_DV), F32)],
        compiler_params=_cparams(("arbitrary",), 40),
        name="gdn_scan",
    )(eg_flat, u, w, qe, ke, attn, z_b, norm_w.reshape(1, GDN_DV))


def _short_conv_kernel(bg_ref, cg_ref, x_ref, z_ref, cgh_ref, xh_ref, cw_ref, o_ref):
    i = pl.program_id(0)
    cur = cg_ref[...] * x_ref[...]
    halo = jnp.where(i > 0, cgh_ref[...] * xh_ref[...], 0.0)
    y = _causal_conv(cur, halo, cw_ref, SC_CONV)
    o_ref[...] = (bg_ref[...] * y * _silu(z_ref[...])).astype(o_ref.dtype)


def _short_conv(p_sc, conv_w, tm=256):
    t = p_sc.shape[0]
    tm = min(tm, t)
    hb = tm // SUBLANES
    main = lambda c: pl.BlockSpec((tm, SC_WIDTH), functools.partial(lambda i, c: (i, c), c=c))
    halo = lambda c: pl.BlockSpec((SUBLANES, SC_WIDTH),
                                  functools.partial(lambda i, c: (jnp.maximum(i * hb - 1, 0), c), c=c))
    return pl.pallas_call(
        _short_conv_kernel,
        grid=(t // tm,),
        in_specs=[main(0), main(1), main(2), main(3), halo(1), halo(2),
                  pl.BlockSpec((SC_CONV, SC_WIDTH), lambda i: (0, 0))],
        out_specs=pl.BlockSpec((tm, SC_WIDTH), lambda i: (i, 0)),
        out_shape=jax.ShapeDtypeStruct((t, SC_WIDTH), BF16),
        compiler_params=_cparams(("parallel",), 40),
        name="short_conv",
    )(p_sc, p_sc, p_sc, p_sc, p_sc, p_sc, conv_w)


_IN_GROUPS = (("nsa", _OFF_QA, _OFF_GATE), ("za", _OFF_ZA, _OFF_QB), ("gdn", _OFF_QB, _OFF_AB),
              ("zb", _OFF_ZB, _OFF_SC), ("sc", _OFF_SC, _OFF_MERGE), ("merge", _OFF_MERGE, None))


def _stage_w_in(w_in):
    depth, d, _ = w_in.shape
    w_t = jnp.transpose(w_in, (0, 2, 1))
    small_pad = LANES - 2 * GDN_HEADS - 3 * NSA_HEADS
    narrow = jnp.concatenate([w_in[:, :, _OFF_AB:_OFF_ZB], w_in[:, :, _OFF_GATE:_OFF_ZA],
                              jnp.zeros((depth, d, small_pad), F32)], axis=2)
    w_small_t = jnp.transpose(narrow, (0, 2, 1)).astype(BF16)
    return w_t, w_small_t


def _layer(x, layer, norm_w, w_t, w_small_t, nsa_qk_norm, cmp_pos, cmp_w1, cmp_w2, gdn_conv_w, gdn_a_log,
           gdn_dt_bias, gdn_norm_w, sc_conv_w, wb_nsa, wb_gdn, wb_sc, w_out, slopes):
    hn = _rmsnorm(x, norm_w)
    proj = {}
    for name, a, b in _IN_GROUPS:
        n = (w_t.shape[1] if b is None else b) - a
        proj[name] = _matmul_nt_f32w(hn, w_t, layer, a, n, F32, 1024, 512, "in_proj_" + name)
    small = _matmul_nt(hn, w_small_t, layer, 0, LANES, F32, 1024, LANES, "in_proj_small")
    p_nsa, z_a, p_gdn, z_b, p_sc, p_mg = (proj[n] for n, _, _ in _IN_GROUPS)

    q, kv_c, k_s, v_s, k_w, v_w = _nsa_prep(p_nsa, nsa_qk_norm)
    kv_cmp = _compress(kv_c, cmp_pos, cmp_w1, cmp_w2, nsa_qk_norm[1])
    o_cmp, sel, blk_any = _cmp_attention(q, kv_cmp, slopes)
    o_slc = _sel_attention(q, k_s, v_s, sel, blk_any, slopes)
    o_win = _win_attention(q, k_w, v_w, slopes)
    o_a = _nsa_combine(small, o_cmp, o_slc, o_win, z_a)

    q_b, k_b, v_b, gb = _gdn_prep(p_gdn, small, gdn_conv_w, gdn_a_log, gdn_dt_bias)
    o_b = _gdn_chunks(q_b, k_b, v_b, gb, z_b, gdn_norm_w)

    o_c = _short_conv(p_sc, sc_conv_w)

    merged = _merge(o_a, o_b, o_c, wb_nsa, wb_gdn, wb_sc, layer, p_mg)
    return _matmul_residual(merged, w_out, layer, x, 1024, 512)


def kernel(x, norm_w, w_in, nsa_qk_norm, cmp_pos, cmp_w1, cmp_w2, gdn_conv_w, gdn_a_log, gdn_dt_bias,
           gdn_norm_w, sc_conv_w, w_branch_nsa, w_branch_gdn, w_branch_sc, w_out):
    b, t, d = x.shape
    depth = norm_w.shape[0]
    heads = jnp.arange(1, NSA_HEADS + 1, dtype=F32)
    slopes = jnp.exp2(-8.0 * heads / NSA_HEADS)
    w_t, w_small_t = _stage_w_in(w_in)
    wb_nsa, wb_gdn, wb_sc, w_out16 = (w.astype(BF16) for w in (w_branch_nsa, w_branch_gdn, w_branch_sc, w_out))
    outs = []
    for bi in range(b):
        xb = x[bi]
        for l in range(depth):
            xb = _layer(xb, l, norm_w[l], w_t, w_small_t, nsa_qk_norm[l], cmp_pos[l], cmp_w1[l], cmp_w2[l],
                        gdn_conv_w[l], gdn_a_log[l], gdn_dt_bias[l], gdn_norm_w[l], sc_conv_w[l],
                        wb_nsa, wb_gdn, wb_sc, w_out16, slopes)
        outs.append(xb)
    return jnp.stack(outs, axis=0)
```

```python
import functools
import math

import jax
import jax.numpy as jnp
from jax import lax
from jax.experimental import pallas as pl
from jax.experimental.pallas import tpu as pltpu

F32 = jnp.float32
BF16 = jnp.bfloat16

HEAD_DIM = 128
NSA_HEADS = 16
NSA_KV_HEADS = 4
NSA_GROUP = NSA_HEADS // NSA_KV_HEADS
NSA_WIDTH = NSA_HEADS * HEAD_DIM
NSA_KV_WIDTH = NSA_KV_HEADS * HEAD_DIM
CMP_BLOCK = 32
CMP_STRIDE = 16
CMP_HIDDEN = 256
SEL_BLOCK = 64
SEL_TOPK = 16
WINDOW = 512
GDN_HEADS = 16
GDN_DK = 128
GDN_DV = 128
GDN_KEY_WIDTH = GDN_HEADS * GDN_DK
GDN_VAL_WIDTH = GDN_HEADS * GDN_DV
GDN_CONV = 4
GDN_CHUNK = 64
SC_WIDTH = 2048
SC_CONV = 3
N_BRANCH = 3
NORM_EPS = 1e-6
NEG_INF = -1e30
MASK_BIAS = -2e30
FORCE_SCORE = 1e6

LANES = 128
SUBLANES = 8
BF16_SUBLANES = 16
N_BLK_LANES = 128
CMP_KEY_STEP = 128
MIB = 1024 * 1024

_OFF_QA = 0
_OFF_KVC = _OFF_QA + NSA_WIDTH
_OFF_KVS = _OFF_KVC + 2 * NSA_KV_WIDTH
_OFF_KVW = _OFF_KVS + 2 * NSA_KV_WIDTH
_OFF_GATE = _OFF_KVW + 2 * NSA_KV_WIDTH
_OFF_ZA = _OFF_GATE + 3 * NSA_HEADS
_OFF_QB = _OFF_ZA + NSA_WIDTH
_OFF_AB = _OFF_QB + 2 * GDN_KEY_WIDTH + GDN_VAL_WIDTH
_OFF_BETA = _OFF_AB + GDN_HEADS
_OFF_ZB = _OFF_BETA + GDN_HEADS
_OFF_SC = _OFF_ZB + GDN_VAL_WIDTH
_OFF_MERGE = _OFF_SC + 4 * SC_WIDTH

_SM_A = 0
_SM_BETA = GDN_HEADS
_SM_GATE = 2 * GDN_HEADS


def _cparams(sem, vmem_mib):
    return pltpu.CompilerParams(dimension_semantics=sem, vmem_limit_bytes=vmem_mib * MIB)


def _sigmoid(x):
    return jax.nn.sigmoid(x)


def _silu(x):
    return x * jax.nn.sigmoid(x)


def _dot(a, b):
    return jnp.dot(a, b, preferred_element_type=F32)


def _dot_nt(a, b):
    return lax.dot_general(a, b, (((1,), (1,)), ((), ())), preferred_element_type=F32)


def _dot_tn(a, b):
    return lax.dot_general(a, b, (((0,), (0,)), ((), ())), preferred_element_type=F32)


def _head_rms(x, w):
    return x * lax.rsqrt(jnp.mean(x * x, axis=-1, keepdims=True) + NORM_EPS) * w


def _rmsnorm_kernel(x_ref, w_ref, o_ref):
    x = x_ref[...]
    y = x * lax.rsqrt(jnp.mean(x * x, axis=-1, keepdims=True) + NORM_EPS)
    o_ref[...] = (y * w_ref[...]).astype(o_ref.dtype)


def _rmsnorm(x, w, tm=256):
    t, d = x.shape
    return pl.pallas_call(
        _rmsnorm_kernel,
        grid=(t // tm,),
        in_specs=[pl.BlockSpec((tm, d), lambda i: (i, 0)), pl.BlockSpec((1, d), lambda i: (0, 0))],
        out_specs=pl.BlockSpec((tm, d), lambda i: (i, 0)),
        out_shape=jax.ShapeDtypeStruct((t, d), BF16),
        compiler_params=_cparams(("parallel",), 32),
        name="rmsnorm",
    )(x, w.reshape(1, d))


def _mm_kernel(a_ref, b_ref, o_ref):
    o_ref[...] = _dot(a_ref[...], b_ref[...]).astype(o_ref.dtype)


def _mm_nt_kernel(a_ref, b_ref, o_ref):
    o_ref[...] = _dot_nt(a_ref[...], b_ref[...]).astype(o_ref.dtype)


def _matmul_nt(a, b_t, layer, row_off, n, out_dtype, tm, tn, name):
    m, k = a.shape
    tm = min(tm, m)
    tn = min(tn, n)
    depth, rows, _ = b_t.shape
    first_row = layer * rows + row_off
    assert first_row % BF16_SUBLANES == 0 and n % tn == 0
    b_t = b_t.reshape(depth * rows, k)
    return pl.pallas_call(
        _mm_nt_kernel,
        grid=(m // tm, n // tn),
        in_specs=[pl.BlockSpec((tm, k), lambda i, j: (i, 0)),
                  pl.BlockSpec((pl.Element(tn), pl.Element(k)),
                               lambda i, j: (pl.multiple_of(first_row + j * tn, BF16_SUBLANES), 0))],
        out_specs=pl.BlockSpec((tm, tn), lambda i, j: (i, j)),
        out_shape=jax.ShapeDtypeStruct((m, n), out_dtype),
        compiler_params=_cparams(("parallel", "arbitrary"), 52),
        name=name,
    )(a, b_t)


def _mm_res_kernel(a_ref, b_ref, r_ref, o_ref):
    o_ref[...] = r_ref[...] + _dot(a_ref[...], b_ref[...])


def _matmul_residual(a, b, layer, r, tm, tn):
    m, k = a.shape
    n = b.shape[2]
    tm = min(tm, m)
    return pl.pallas_call(
        _mm_res_kernel,
        grid=(m // tm, n // tn),
        in_specs=[pl.BlockSpec((tm, k), lambda i, j: (i, 0)),
                  pl.BlockSpec((None, k, tn), lambda i, j: (layer, 0, j)),
                  pl.BlockSpec((tm, tn), lambda i, j: (i, j))],
        out_specs=pl.BlockSpec((tm, tn), lambda i, j: (i, j)),
        out_shape=jax.ShapeDtypeStruct((m, n), F32),
        compiler_params=_cparams(("parallel", "arbitrary"), 48),
        name="out_proj_residual",
    )(a, b, r)


def _merge_kernel(oa_ref, ob_ref, oc_ref, wa_ref, wb_ref, wc_ref, ga_ref, gb_ref, gc_ref, o_ref):
    half = o_ref.shape[1] // 2
    cols = [slice(0, half), slice(half, 2 * half)]
    prods = [[_dot(a_ref[...], w_ref[:, c]) for a_ref, w_ref in ((oa_ref, wa_ref), (ob_ref, wb_ref), (oc_ref, wc_ref))]
             for c in cols]
    for c, (pa, pb, pc) in zip(cols, prods):
        acc = _sigmoid(ga_ref[:, c]) * pa
        acc = acc + _sigmoid(gb_ref[:, c]) * pb
        acc = acc + _sigmoid(gc_ref[:, c]) * pc
        o_ref[:, c] = acc.astype(o_ref.dtype)


def _merge(o_a, o_b, o_c, wa, wb, wc, layer, gates, tm=512, tn=512):
    m, k = o_a.shape
    n = wa.shape[2]
    tm = min(tm, m)
    nb = n // tn
    a_spec = pl.BlockSpec((tm, k), lambda i, j: (i, 0))
    w_spec = pl.BlockSpec((None, k, tn), lambda i, j: (layer, 0, j))
    g_specs = [pl.BlockSpec((tm, tn), functools.partial(lambda i, j, br: (i, br * nb + j), br=br))
               for br in range(N_BRANCH)]
    return pl.pallas_call(
        _merge_kernel,
        grid=(m // tm, nb),
        in_specs=[a_spec, a_spec, a_spec, w_spec, w_spec, w_spec] + g_specs,
        out_specs=pl.BlockSpec((tm, tn), lambda i, j: (i, j)),
        out_shape=jax.ShapeDtypeStruct((m, n), BF16),
        compiler_params=_cparams(("parallel", "arbitrary"), 48),
        name="branch_merge",
    )(o_a, o_b, o_c, wa, wb, wc, gates, gates, gates)


def _nsa_prep_kernel(p_ref, nw_ref, q_ref, kvc_ref, ks_ref, vs_ref, kw_ref, vw_ref):
    nw = nw_ref[...]
    for h in range(NSA_HEADS):
        sl = slice(h * HEAD_DIM, (h + 1) * HEAD_DIM)
        q_ref[:, sl] = (_head_rms(p_ref[:, sl], nw[0:1]) * (HEAD_DIM ** -0.5)).astype(BF16)
    kvc_ref[...] = p_ref[:, _OFF_KVC:_OFF_KVS]
    for g in range(NSA_KV_HEADS):
        sl = slice(g * HEAD_DIM, (g + 1) * HEAD_DIM)
        ks = p_ref[:, _OFF_KVS + g * HEAD_DIM:_OFF_KVS + (g + 1) * HEAD_DIM]
        ks_ref[:, sl] = _head_rms(ks, nw[2:3]).astype(BF16)
        vs_ref[:, sl] = p_ref[:, _OFF_KVS + NSA_KV_WIDTH + g * HEAD_DIM:
                              _OFF_KVS + NSA_KV_WIDTH + (g + 1) * HEAD_DIM].astype(BF16)
        kw = p_ref[:, _OFF_KVW + g * HEAD_DIM:_OFF_KVW + (g + 1) * HEAD_DIM]
        kw_ref[:, sl] = _head_rms(kw, nw[3:4]).astype(BF16)
        vw_ref[:, sl] = p_ref[:, _OFF_KVW + NSA_KV_WIDTH + g * HEAD_DIM:
                              _OFF_KVW + NSA_KV_WIDTH + (g + 1) * HEAD_DIM].astype(BF16)


def _nsa_prep(p_nsa, qk_norm, tm=256):
    t, width = p_nsa.shape
    row = lambda i: (i, 0)
    return pl.pallas_call(
        _nsa_prep_kernel,
        grid=(t // tm,),
        in_specs=[pl.BlockSpec((tm, width), row), pl.BlockSpec((4, HEAD_DIM), lambda i: (0, 0))],
        out_specs=[pl.BlockSpec((tm, NSA_WIDTH), row)] + [pl.BlockSpec((tm, 2 * NSA_KV_WIDTH), row)]
        + [pl.BlockSpec((tm, NSA_KV_WIDTH), row)] * 4,
        out_shape=[jax.ShapeDtypeStruct((t, NSA_WIDTH), BF16), jax.ShapeDtypeStruct((t, 2 * NSA_KV_WIDTH), F32)]
        + [jax.ShapeDtypeStruct((t, NSA_KV_WIDTH), BF16)] * 4,
        compiler_params=_cparams(("parallel",), 40),
        name="nsa_prep",
    )(p_nsa, qk_norm)


def _compress_kernel(x_ref, pos_ref, w1lo_ref, w1hi_ref, w2_ref, nw_ref, o_ref, lo_acc, hi_acc, *, n_chunk):
    kv = pl.program_id(0)
    l = pl.program_id(1)

    @pl.when(l == 0)
    def _():
        lo_acc[...] = jnp.zeros(lo_acc.shape, F32)
        hi_acc[...] = jnp.zeros(hi_acc.shape, F32)

    pos_lo = pos_ref[0, pl.ds(l, 1), :]
    pos_hi = pos_ref[0, pl.ds(l + CMP_STRIDE, 1), :]
    w_lo = w1lo_ref[0, 0]
    w_hi = w1hi_ref[0, 0]
    for g in range(NSA_KV_HEADS):
        x = x_ref[:, g * HEAD_DIM:(g + 1) * HEAD_DIM]
        lo_acc[g] += _dot((x + pos_lo).astype(BF16), w_lo)
        hi_acc[g] += _dot((x + pos_hi).astype(BF16), w_hi)

    @pl.when(l == CMP_STRIDE - 1)
    def _():
        for g in range(NSA_KV_HEADS):
            hidden = _silu(lo_acc[g] + pltpu.roll(hi_acc[g], shift=n_chunk - 1, axis=0))
            out = _dot(hidden.astype(BF16), w2_ref[0])
            normed = _head_rms(out, nw_ref[...])
            o_ref[0, g] = jnp.where(kv == 0, normed, out).astype(BF16)


def _compress(kv_c, cmp_pos, cmp_w1, cmp_w2, k_norm_w):
    t = kv_c.shape[0]
    n_chunk = t // CMP_STRIDE
    x = kv_c.reshape(n_chunk, CMP_STRIDE * 2 * NSA_KV_WIDTH)
    w1 = cmp_w1.astype(BF16)
    w2 = cmp_w2.astype(BF16)
    return pl.pallas_call(
        functools.partial(_compress_kernel, n_chunk=n_chunk),
        grid=(2, CMP_STRIDE),
        in_specs=[pl.BlockSpec((n_chunk, NSA_KV_WIDTH), lambda a, l: (0, 2 * l + a)),
                  pl.BlockSpec((1, CMP_BLOCK, HEAD_DIM), lambda a, l: (a, 0, 0)),
                  pl.BlockSpec((1, 1, HEAD_DIM, CMP_HIDDEN), lambda a, l: (a, l, 0, 0)),
                  pl.BlockSpec((1, 1, HEAD_DIM, CMP_HIDDEN), lambda a, l: (a, l + CMP_STRIDE, 0, 0)),
                  pl.BlockSpec((1, CMP_HIDDEN, HEAD_DIM), lambda a, l: (a, 0, 0)),
                  pl.BlockSpec((1, HEAD_DIM), lambda a, l: (0, 0))],
        out_specs=pl.BlockSpec((1, NSA_KV_HEADS, n_chunk, HEAD_DIM), lambda a, l: (a, 0, 0, 0)),
        out_shape=jax.ShapeDtypeStruct((2, NSA_KV_HEADS, n_chunk, HEAD_DIM), BF16),
        scratch_shapes=[pltpu.VMEM((NSA_KV_HEADS, n_chunk, CMP_HIDDEN), F32),
                        pltpu.VMEM((NSA_KV_HEADS, n_chunk, CMP_HIDDEN), F32)],
        compiler_params=_cparams(("arbitrary", "arbitrary"), 40),
        name="nsa_compress",
    )(x, cmp_pos, w1, w1, w2, k_norm_w.reshape(1, HEAD_DIM))


def _cmp_attn_kernel(slopes_ref, q_ref, k_ref, v_ref, cov_ref, o_ref, sel_ref, any_ref, imp_sc, *, tq, n_chunk):
    g = pl.program_id(0)
    qi = pl.program_id(1)
    t0 = qi * tq
    sls = [slice(z * HEAD_DIM, (z + 1) * HEAD_DIM) for z in range(NSA_GROUP)]

    def attend(nk):
        k = k_ref[0, 0, 0:nk, :]
        v_t = v_ref[0, 0, 0:nk, :].T
        t_pos = t0 + lax.broadcasted_iota(jnp.int32, (nk, tq), 1)
        k_end = lax.broadcasted_iota(jnp.int32, (nk, tq), 0) * CMP_STRIDE + (CMP_BLOCK - 1)
        mask_bias = jnp.where(t_pos >= k_end, 0.0, MASK_BIAS)
        k_rel = (k_end - t0).astype(F32)
        psum = jnp.zeros((nk, tq), F32)
        qk_next = _dot_nt(k, q_ref[:, sls[0]])
        pending = None
        for z in range(NSA_GROUP):
            qk = qk_next
            if z + 1 < NSA_GROUP:
                qk_next = _dot_nt(k, q_ref[:, sls[z + 1]])
            s = qk + (mask_bias + slopes_ref[g * NSA_GROUP + z] * k_rel)
            m = jnp.maximum(jnp.max(s, axis=0, keepdims=True), NEG_INF)
            e = jnp.exp(s - m)
            p = e * (1.0 / jnp.maximum(jnp.sum(e, axis=0, keepdims=True), 1e-30))
            psum = psum + p
            if pending is not None:
                o_ref[:, sls[pending[0]]] = _dot(v_t, pending[1]).T
            pending = (z, p.astype(BF16))
        o_ref[:, sls[pending[0]]] = _dot(v_t, pending[1]).T
        p_hi = psum.astype(BF16)
        p_lo = (psum - p_hi.astype(F32)).astype(BF16)
        cov_t = cov_ref[:, 0:nk]
        imp_sc[...] = _dot(cov_t, p_hi) + _dot(cov_t, p_lo)

    visible = (t0 + tq - CMP_BLOCK) // CMP_STRIDE + 1
    n_var = n_chunk // CMP_KEY_STEP if n_chunk % CMP_KEY_STEP == 0 else 1
    if n_var <= 1:
        attend(n_chunk)
    else:
        for var in range(1, n_var + 1):
            lo, hi = (var - 1) * CMP_KEY_STEP, var * CMP_KEY_STEP
            if var == 1:
                cond = visible <= hi
            elif var < n_var:
                cond = (visible > lo) & (visible <= hi)
            else:
                cond = visible > lo
            pl.when(cond)(functools.partial(attend, hi))
    imp = imp_sc[...]

    j = lax.broadcasted_iota(jnp.int32, (N_BLK_LANES, tq), 0)
    cur = (t0 + lax.broadcasted_iota(jnp.int32, (N_BLK_LANES, tq), 1)) // SEL_BLOCK
    valid = j <= cur
    forced = (j == 0) | (j == cur) | (j == cur - 1)
    chosen_first = valid & forced
    val = jnp.where(valid, jnp.where(forced, -2.0, imp), -1.0)
    sel = chosen_first.astype(F32)
    jf = j.astype(F32)
    for _ in range(SEL_TOPK - 3):
        m = jnp.max(val, axis=0, keepdims=True)
        idx = jnp.min(jnp.where(val == m, jf, float(N_BLK_LANES)), axis=0, keepdims=True)
        hit = jf == idx
        sel = jnp.where(hit & (m > -0.5), 1.0, sel)
        val = jnp.where(hit, -2.0, val)
    sel_q = sel.T
    sel_ref[0] = sel_q.astype(BF16)
    any_ref[0, 0] = jnp.broadcast_to(jnp.max(sel_q, axis=0, keepdims=True), (SUBLANES, N_BLK_LANES))


def _cmp_attention(q, kv_cmp, slopes, tq=256):
    t = q.shape[0]
    n_chunk = kv_cmp.shape[2]
    n_cmp = n_chunk - CMP_BLOCK // CMP_STRIDE + 1
    n_blk = t // SEL_BLOCK
    assert n_blk <= N_BLK_LANES
    tq = min(tq, t)
    cs = jnp.arange(n_chunk)[:, None] * CMP_STRIDE
    bs = jnp.arange(N_BLK_LANES)[None, :] * SEL_BLOCK
    cover = ((cs <= bs + SEL_BLOCK - 1) & (cs + CMP_BLOCK - 1 >= bs)
             & (jnp.arange(n_chunk)[:, None] < n_cmp) & (jnp.arange(N_BLK_LANES)[None, :] < n_blk))
    cover = cover.astype(BF16).T
    return pl.pallas_call(
        functools.partial(_cmp_attn_kernel, tq=tq, n_chunk=n_chunk),
        grid=(NSA_KV_HEADS, t // tq),
        in_specs=[pl.BlockSpec(memory_space=pltpu.SMEM),
                  pl.BlockSpec((tq, NSA_GROUP * HEAD_DIM), lambda g, i: (i, g)),
                  pl.BlockSpec((1, 1, n_chunk, HEAD_DIM), lambda g, i: (0, g, 0, 0)),
                  pl.BlockSpec((1, 1, n_chunk, HEAD_DIM), lambda g, i: (1, g, 0, 0)),
                  pl.BlockSpec((N_BLK_LANES, n_chunk), lambda g, i: (0, 0))],
        out_specs=[pl.BlockSpec((tq, NSA_GROUP * HEAD_DIM), lambda g, i: (i, g)),
                   pl.BlockSpec((1, tq, N_BLK_LANES), lambda g, i: (g, i, 0)),
                   pl.BlockSpec((1, 1, SUBLANES, N_BLK_LANES), lambda g, i: (g, i, 0, 0))],
        out_shape=[jax.ShapeDtypeStruct((t, NSA_WIDTH), F32),
                   jax.ShapeDtypeStruct((NSA_KV_HEADS, t, N_BLK_LANES), BF16),
                   jax.ShapeDtypeStruct((NSA_KV_HEADS, t // tq, SUBLANES, N_BLK_LANES), F32)],
        scratch_shapes=[pltpu.VMEM((N_BLK_LANES, tq), F32)],
        compiler_params=_cparams(("parallel", "parallel"), 40),
        name="nsa_cmp_attn",
    )(slopes, q, kv_cmp, kv_cmp, cover)


def _sel_attn_kernel(cnt_ref, kjs_ref, slopes_ref, q_ref, k_ref, v_ref, eb_ref, sel_ref, o_ref,
                     qa_sc, m_sc, l_sc, acc_sc, *, tq, tk, max_tiles):
    g = pl.program_id(0)
    qi = pl.program_id(1)
    item = g * pl.num_programs(1) + qi
    t0 = qi * tq

    m_sc[...] = jnp.full(m_sc.shape, NEG_INF, F32)
    l_sc[...] = jnp.zeros(l_sc.shape, F32)
    acc_sc[...] = jnp.zeros(acc_sc.shape, F32)
    unselected = sel_ref[0] - 1.0
    for z in range(NSA_GROUP):
        qa_sc[z] = jnp.concatenate([q_ref[:, z * HEAD_DIM:(z + 1) * HEAD_DIM], unselected], axis=1)

    def tile(kj, causal):
        k0 = kj * tk
        rows = pl.ds(pl.multiple_of(k0, tk), tk)
        k_aug = jnp.concatenate([k_ref[rows, :], eb_ref[kj]], axis=1)
        v_t = v_ref[rows, :].T
        row = lax.broadcasted_iota(jnp.int32, (tk, tq), 0)
        k_rel = (k0 - t0 + row).astype(F32)
        if causal:
            col = lax.broadcasted_iota(jnp.int32, (tk, tq), 1)
            causal_bias = jnp.where(t0 + col >= k0 + row, 0.0, MASK_BIAS)
        qk_next = _dot_nt(k_aug, qa_sc[0])
        pending = None
        for z in range(NSA_GROUP):
            qk = qk_next
            if z + 1 < NSA_GROUP:
                qk_next = _dot_nt(k_aug, qa_sc[z + 1])
            s = qk + slopes_ref[g * NSA_GROUP + z] * k_rel
            if causal:
                s = s + causal_bias
            m_prev = m_sc[z]
            m_new = jnp.maximum(m_prev, jnp.max(s, axis=0, keepdims=True))
            alpha = jnp.exp(m_prev - m_new)
            e = jnp.exp(s - m_new)
            l_sc[z] = alpha * l_sc[z] + jnp.sum(e, axis=0, keepdims=True)
            m_sc[z] = m_new
            if pending is not None:
                pz, palpha, pe = pending
                acc_sc[pz] = palpha * acc_sc[pz] + _dot(v_t, pe)
            pending = (z, alpha, e.astype(BF16))
        pz, palpha, pe = pending
        acc_sc[pz] = palpha * acc_sc[pz] + _dot(v_t, pe)

    def visit(i, carry):
        tile(kjs_ref[item * max_tiles + i], False)
        return carry

    lax.fori_loop(0, cnt_ref[item], visit, 0)
    tile(t0 // tk, True)

    for z in range(NSA_GROUP):
        out_t = acc_sc[z] * (1.0 / jnp.maximum(l_sc[z], 1e-30))
        o_ref[:, z * HEAD_DIM:(z + 1) * HEAD_DIM] = out_t.T


def _sel_work_list(blk_any, t, tq, tk):
    nq, nk, bpt = t // tq, t // tk, tk // SEL_BLOCK
    n_blk = t // SEL_BLOCK
    picked = blk_any[:, :, 0, :n_blk].reshape(NSA_KV_HEADS, nq, nk, bpt).max(axis=-1) > 0.5
    diagonal = (jnp.arange(nq) * tq) // tk
    before = picked & (jnp.arange(nk)[None, :] < diagonal[:, None])[None]
    count = before.sum(axis=-1).astype(jnp.int32)
    tiles = jnp.argsort(jnp.logical_not(before), axis=-1, stable=True).astype(jnp.int32)
    return count.reshape(-1), tiles.reshape(-1)


def _sel_attention(q, k_s, v_s, sel, blk_any, slopes, tq=256, tk=512):
    t = q.shape[0]
    tq = min(tq, t)
    tk = min(tk, t)
    assert tk % tq == 0
    nk = t // tk
    count, tiles = _sel_work_list(blk_any, t, tq, tk)

    blk_of_key = jnp.arange(t)[:, None] // SEL_BLOCK
    block_onehot = jnp.where(blk_of_key == jnp.arange(N_BLK_LANES)[None, :], -MASK_BIAS, 0.0)
    block_onehot = block_onehot.astype(BF16).reshape(nk, tk, N_BLK_LANES)

    smem = pl.BlockSpec(memory_space=pltpu.SMEM)
    return pl.pallas_call(
        functools.partial(_sel_attn_kernel, tq=tq, tk=tk, max_tiles=nk),
        grid=(NSA_KV_HEADS, t // tq),
        in_specs=[smem, smem, smem,
                  pl.BlockSpec((tq, NSA_GROUP * HEAD_DIM), lambda g, i: (i, g)),
                  pl.BlockSpec((t, HEAD_DIM), lambda g, i: (0, g)),
                  pl.BlockSpec((t, HEAD_DIM), lambda g, i: (0, g)),
                  pl.BlockSpec((nk, tk, N_BLK_LANES), lambda g, i: (0, 0, 0)),
                  pl.BlockSpec((1, tq, N_BLK_LANES), lambda g, i: (g, i, 0))],
        out_specs=pl.BlockSpec((tq, NSA_GROUP * HEAD_DIM), lambda g, i: (i, g)),
        out_shape=jax.ShapeDtypeStruct((t, NSA_WIDTH), F32),
        scratch_shapes=[pltpu.VMEM((NSA_GROUP, tq, 2 * HEAD_DIM), BF16),
                        pltpu.VMEM((NSA_GROUP, 1, tq), F32), pltpu.VMEM((NSA_GROUP, 1, tq), F32),
                        pltpu.VMEM((NSA_GROUP, HEAD_DIM, tq), F32)],
        compiler_params=_cparams(("arbitrary", "arbitrary"), 40),
        name="nsa_sel_attn",
    )(count, tiles, slopes, q, k_s, v_s, block_onehot, sel)


def _win_attn_kernel(slopes_ref, q_ref, *refs, tq, n_tile):
    k_refs = refs[:n_tile]
    v_refs = refs[n_tile:2 * n_tile]
    o_ref = refs[2 * n_tile]
    g = pl.program_id(0)
    qi = pl.program_id(1)
    span = n_tile * tq
    k = jnp.concatenate([r[...] for r in k_refs], axis=0)
    v = jnp.concatenate([r[...] for r in v_refs], axis=0)
    t_pos = qi * tq + lax.broadcasted_iota(jnp.int32, (tq, span), 0)
    k_pos = (qi - (n_tile - 1)) * tq + lax.broadcasted_iota(jnp.int32, (tq, span), 1)
    dist = t_pos - k_pos
    mask_bias = jnp.where((dist >= 0) & (dist < WINDOW) & (k_pos >= 0), 0.0, MASK_BIAS)
    k_rel = (lax.broadcasted_iota(jnp.int32, (1, span), 1) - (n_tile - 1) * tq).astype(F32)
    sls = [slice(z * HEAD_DIM, (z + 1) * HEAD_DIM) for z in range(NSA_GROUP)]
    qk_next = _dot_nt(q_ref[:, sls[0]], k)
    pending = None
    for z in range(NSA_GROUP):
        qk = qk_next
        if z + 1 < NSA_GROUP:
            qk_next = _dot_nt(q_ref[:, sls[z + 1]], k)
        s = qk + (mask_bias + slopes_ref[g * NSA_GROUP + z] * k_rel)
        m = jnp.maximum(jnp.max(s, axis=-1, keepdims=True), NEG_INF)
        e = jnp.exp(s - m)
        inv_l = 1.0 / jnp.maximum(jnp.sum(e, axis=-1, keepdims=True), 1e-30)
        if pending is not None:
            o_ref[:, sls[pending[0]]] = _dot(pending[1], v) * pending[2]
        pending = (z, e.astype(BF16), inv_l)
    o_ref[:, sls[pending[0]]] = _dot(pending[1], v) * pending[2]


def _win_attention(q, k_w, v_w, slopes, tq=256):
    t = q.shape[0]
    tq = min(tq, t)
    n_tile = -(-WINDOW // tq) + 1

    def kv_spec(c):
        return pl.BlockSpec((tq, HEAD_DIM), lambda g, i: (jnp.maximum(i - (n_tile - 1) + c, 0), g))

    return pl.pallas_call(
        functools.partial(_win_attn_kernel, tq=tq, n_tile=n_tile),
        grid=(NSA_KV_HEADS, t // tq),
        in_specs=[pl.BlockSpec(memory_space=pltpu.SMEM),
                  pl.BlockSpec((tq, NSA_GROUP * HEAD_DIM), lambda g, i: (i, g))]
        + [kv_spec(c) for c in range(n_tile)] * 2,
        out_specs=pl.BlockSpec((tq, NSA_GROUP * HEAD_DIM), lambda g, i: (i, g)),
        out_shape=jax.ShapeDtypeStruct((t, NSA_WIDTH), F32),
        compiler_params=_cparams(("parallel", "parallel"), 40),
        name="nsa_win_attn",
    )(slopes, q, *([k_w] * n_tile), *([v_w] * n_tile))


def _nsa_combine_kernel(sm_ref, ex_ref, oc_ref, os_ref, ow_ref, z_ref, o_ref):
    logits = sm_ref[...]
    hi = logits.astype(BF16)
    lo = (logits - hi.astype(F32)).astype(BF16)
    acc = None
    for br, branch_ref in enumerate((oc_ref, os_ref, ow_ref)):
        ex = ex_ref[br]
        gate = _sigmoid(_dot(hi, ex) + _dot(lo, ex))
        term = gate * branch_ref[...]
        acc = term if acc is None else acc + term
    o_ref[...] = (acc * _silu(z_ref[...])).astype(o_ref.dtype)


def _nsa_combine(small, o_cmp, o_slc, o_win, z_a, tm=256):
    t = small.shape[0]
    tm = min(tm, t)
    lane = jnp.arange(LANES)[None, :, None]
    br = jnp.arange(N_BRANCH)[:, None, None]
    head = (jnp.arange(NSA_WIDTH) // HEAD_DIM)[None, None, :]
    expand = (lane == _SM_GATE + head * N_BRANCH + br).astype(BF16)
    row = lambda i: (i, 0)
    wide = pl.BlockSpec((tm, NSA_WIDTH), row)
    return pl.pallas_call(
        _nsa_combine_kernel,
        grid=(t // tm,),
        in_specs=[pl.BlockSpec((tm, LANES), row),
                  pl.BlockSpec((N_BRANCH, LANES, NSA_WIDTH), lambda i: (0, 0, 0)),
                  wide, wide, wide, wide],
        out_specs=wide,
        out_shape=jax.ShapeDtypeStruct((t, NSA_WIDTH), BF16),
        compiler_params=_cparams(("parallel",), 40),
        name="nsa_combine",
    )(small, expand, o_cmp, o_slc, o_win, z_a)


def _shift_rows(cur, halo, s):
    rolled = pltpu.roll(cur, shift=s, axis=0)
    halo_rolled = pltpu.roll(halo, shift=s, axis=0)
    row = lax.broadcasted_iota(jnp.int32, halo.shape, 0)
    head = jnp.where(row < s, halo_rolled, rolled[0:SUBLANES])
    return jnp.concatenate([head, rolled[SUBLANES:]], axis=0)


def _causal_conv(cur, halo, w_ref, k):
    acc = None
    for j in range(k):
        s = k - 1 - j
        term = (cur if s == 0 else _shift_rows(cur, halo, s)) * w_ref[j:j + 1, :]
        acc = term if acc is None else acc + term
    return acc


def _gdn_prep_kernel(p_ref, halo_ref, cw_ref, sm_ref, alog_ref, dtb_ref, q_ref, k_ref, v_ref, gb_ref, xx):
    i = pl.program_id(0)
    tm = p_ref.shape[0]
    xx[0:SUBLANES, :] = jnp.where(i > 0, halo_ref[...], 0.0)
    xx[SUBLANES:, :] = p_ref[...]

    def conv_silu(cols):
        acc = None
        for j in range(GDN_CONV):
            delay = GDN_CONV - 1 - j
            term = xx[SUBLANES - delay:SUBLANES - delay + tm, cols] * cw_ref[j:j + 1, cols]
            acc = term if acc is None else acc + term
        return _silu(acc)

    for h in range(GDN_HEADS):
        sl = slice(h * GDN_DK, (h + 1) * GDN_DK)
        qh = conv_silu(sl)
        q_ref[:, sl] = qh * lax.rsqrt(jnp.sum(qh * qh, axis=-1, keepdims=True) + NORM_EPS) * (GDN_DK ** -0.5)
        kh = conv_silu(slice(GDN_KEY_WIDTH + h * GDN_DK, GDN_KEY_WIDTH + (h + 1) * GDN_DK))
        k_ref[:, sl] = kh * lax.rsqrt(jnp.sum(kh * kh, axis=-1, keepdims=True) + NORM_EPS)
        v_ref[:, sl] = conv_silu(slice(2 * GDN_KEY_WIDTH + h * GDN_DV, 2 * GDN_KEY_WIDTH + (h + 1) * GDN_DV))
    sm = sm_ref[...]
    lane = lax.broadcasted_iota(jnp.int32, sm.shape, 1)
    gdecay = -jnp.exp(alog_ref[...]) * jax.nn.softplus(sm + dtb_ref[...])
    beta = _sigmoid(sm)
    gb_ref[...] = jnp.where(lane < _SM_BETA, gdecay, jnp.where(lane < _SM_GATE, beta, 0.0))


def _gdn_prep(p_gdn, small, conv_w, a_log, dt_bias, tm=256):
    t, width = p_gdn.shape
    tm = min(tm, t)
    hb = tm // SUBLANES
    row = lambda i: (i, 0)
    alog = jnp.zeros((1, LANES), F32).at[0, _SM_A:_SM_A + GDN_HEADS].set(a_log)
    dtb = jnp.zeros((1, LANES), F32).at[0, _SM_A:_SM_A + GDN_HEADS].set(dt_bias)
    const = lambda i: (0, 0)
    return pl.pallas_call(
        _gdn_prep_kernel,
        grid=(t // tm,),
        in_specs=[pl.BlockSpec((tm, width), row),
                  pl.BlockSpec((SUBLANES, width), lambda i: (jnp.maximum(i * hb - 1, 0), 0)),
                  pl.BlockSpec((GDN_CONV, width), const),
                  pl.BlockSpec((tm, LANES), row),
                  pl.BlockSpec((1, LANES), const), pl.BlockSpec((1, LANES), const)],
        out_specs=[pl.BlockSpec((tm, GDN_KEY_WIDTH), row), pl.BlockSpec((tm, GDN_KEY_WIDTH), row),
                   pl.BlockSpec((tm, GDN_VAL_WIDTH), row), pl.BlockSpec((tm, LANES), row)],
        out_shape=[jax.ShapeDtypeStruct((t, GDN_KEY_WIDTH), F32), jax.ShapeDtypeStruct((t, GDN_KEY_WIDTH), F32),
                   jax.ShapeDtypeStruct((t, GDN_VAL_WIDTH), F32), jax.ShapeDtypeStruct((t, LANES), F32)],
        scratch_shapes=[pltpu.VMEM((tm + SUBLANES, width), F32)],
        compiler_params=_cparams(("parallel",), 48),
        name="gdn_prep",
    )(p_gdn, p_gdn, conv_w, small, alog, dtb)


def _row_pad(x):
    return jnp.concatenate([x, jnp.zeros_like(x)], axis=0)


def _gdn_intra_kernel(q_ref, k_ref, v_ref, gb_ref, u_ref, w_ref, qe_ref, ke_ref, at_ref, eg_ref, *, heads):
    c = GDN_CHUNK
    gb = gb_ref[...]
    row = lax.broadcasted_iota(jnp.int32, gb.shape, 0)
    gcum = gb
    shift = 1
    while shift < c:
        gcum = gcum + jnp.where(row >= shift, pltpu.roll(gcum, shift=shift, axis=0), 0.0)
        shift *= 2
    lane = lax.broadcasted_iota(jnp.int32, gb.shape, 1)
    ri = lax.broadcasted_iota(jnp.int32, (c, LANES), 0)
    ci = lax.broadcasted_iota(jnp.int32, (c, LANES), 1)
    tri = ri >= ci
    strict = ri > ci
    eye = ri == ci
    eyef = eye.astype(F32)
    lane8 = lax.broadcasted_iota(jnp.int32, (SUBLANES, LANES), 1)
    eg_all = jnp.zeros((SUBLANES, LANES), F32)
    hs = range(heads)
    sls = [slice(h * GDN_DK, (h + 1) * GDN_DK) for h in hs]

    gcol = [jnp.sum(jnp.where(lane == _SM_A + h, gcum, 0.0), axis=-1, keepdims=True) for h in hs]
    beta = [jnp.sum(jnp.where(lane == _SM_BETA + h, gb, 0.0), axis=-1, keepdims=True) for h in hs]
    decay = []
    for h in hs:
        gmat = jnp.broadcast_to(gcol[h], (c, LANES))
        grow = jnp.sum(jnp.where(eye, gmat, 0.0), axis=0, keepdims=True)
        decay.append(jnp.where(tri, jnp.exp(jnp.where(tri, gmat - grow, 0.0)), 0.0))
    glast = [gcol[h][c - 1:c, :] for h in hs]
    egc = [jnp.exp(gcol[h]) for h in hs]
    q = [q_ref[:, sls[h]] for h in hs]
    k = [k_ref[:, sls[h]] for h in hs]
    k16 = [k[h].astype(BF16) for h in hs]
    qk = [_dot_nt(jnp.concatenate([q[h].astype(BF16), k16[h]], axis=0), _row_pad(k16[h])) for h in hs]
    for h in hs:
        at_ref[:, sls[h]] = (qk[h][:c] * decay[h]).astype(BF16)

    pw = [-jnp.where(strict, beta[h] * qk[h][c:] * decay[h], 0.0) for h in hs]
    inv = [eyef + pw[h] for h in hs]
    pw16 = [pw[h].astype(BF16) for h in hs]
    pw = [_dot(pw16[h], _row_pad(pw16[h])) for h in hs]
    span = 2
    while span < c:
        pw16 = [pw[h].astype(BF16) for h in hs]
        if 2 * span < c:
            both = [_dot(jnp.concatenate([pw16[h], inv[h].astype(BF16)], axis=0), _row_pad(pw16[h])) for h in hs]
            pw = [both[h][:c] for h in hs]
            inv = [inv[h] + both[h][c:] for h in hs]
        else:
            inv = [inv[h] + _dot(inv[h].astype(BF16), _row_pad(pw16[h])) for h in hs]
        span *= 2

    uw = []
    for h in hs:
        rhs = jnp.concatenate([(v_ref[:, sls[h]] * beta[h]).astype(BF16),
                               (k[h] * (beta[h] * egc[h])).astype(BF16)], axis=1)
        uw.append(_dot(inv[h].astype(BF16), _row_pad(rhs)))
    for h in hs:
        u_ref[:, sls[h]] = uw[h][:, :GDN_DV]
        w_ref[:, sls[h]] = uw[h][:, GDN_DV:].astype(BF16)
        qe_ref[:, sls[h]] = (q[h] * egc[h]).astype(BF16)
        ke_ref[:, sls[h]] = (k[h] * jnp.exp(glast[h] - gcol[h])).astype(BF16)
        eg_all = jnp.where(lane8 == h, jnp.exp(glast[h]), eg_all)
    eg_ref[0] = eg_all


def _gdn_scan_kernel(eg_ref, u_ref, w_ref, qe_ref, ke_ref, at_ref, z_ref, nw_ref, o_ref, state, *, heads, cps):
    n = pl.program_id(0)
    c = GDN_CHUNK

    @pl.when(n == 0)
    def _():
        state[...] = jnp.zeros(state.shape, F32)

    nw = nw_ref[...]
    hs = range(heads)
    sls = [slice(h * GDN_DK, (h + 1) * GDN_DK) for h in hs]
    s_cur = [state[h] for h in hs]
    for cc in range(cps):
        rows = slice(cc * c, (cc + 1) * c)
        ws_qs = [_dot(jnp.concatenate([w_ref[rows, sls[h]], qe_ref[rows, sls[h]]], axis=0), s_cur[h].astype(BF16))
                 for h in hs]
        v_new16 = [(u_ref[rows, sls[h]] - ws_qs[h][:c]).astype(BF16) for h in hs]
        o = [ws_qs[h][c:] + _dot(at_ref[rows, h * GDN_DK:h * GDN_DK + c], v_new16[h]) for h in hs]
        s_cur = [s_cur[h] * eg_ref[(n * cps + cc) * heads + h] + _dot_tn(ke_ref[rows, sls[h]], v_new16[h])
                 for h in hs]
        for h in hs:
            o_ref[rows, sls[h]] = (_head_rms(o[h], nw) * _silu(z_ref[rows, sls[h]])).astype(o_ref.dtype)
    for h in hs:
        state[h] = s_cur[h]


def _gdn_chunks(q, k, v, gb, z_b, norm_w):
    t = q.shape[0]
    c = GDN_CHUNK
    n_chunk = t // c
    heads = GDN_HEADS
    wide = pl.BlockSpec((c, GDN_VAL_WIDTH), lambda n: (n, 0))
    u, w, qe, ke, attn, eg = pl.pallas_call(
        functools.partial(_gdn_intra_kernel, heads=heads),
        grid=(n_chunk,),
        in_specs=[wide, wide, wide, pl.BlockSpec((c, LANES), lambda n: (n, 0))],
        out_specs=[wide, wide, wide, wide, wide, pl.BlockSpec((1, SUBLANES, LANES), lambda n: (n, 0, 0))],
        out_shape=[jax.ShapeDtypeStruct((t, GDN_VAL_WIDTH), F32)]
        + [jax.ShapeDtypeStruct((t, GDN_VAL_WIDTH), BF16)] * 4
        + [jax.ShapeDtypeStruct((n_chunk, SUBLANES, LANES), F32)],
        compiler_params=_cparams(("parallel",), 40),
        name="gdn_intra",
    )(q, k, v, gb)
    eg_flat = eg[:, 0, :heads].reshape(n_chunk * heads)
    cps = 2 if n_chunk % 2 == 0 else 1
    wide = pl.BlockSpec((cps * c, GDN_VAL_WIDTH), lambda n: (n, 0))
    return pl.pallas_call(
        functools.partial(_gdn_scan_kernel, heads=heads, cps=cps),
        grid=(n_chunk // cps,),
        in_specs=[pl.BlockSpec(memory_space=pltpu.SMEM), wide, wide, wide, wide, wide, wide,
                  pl.BlockSpec((1, GDN_DV), lambda n: (0, 0))],
        out_specs=wide,
        out_shape=jax.ShapeDtypeStruct((t, GDN_VAL_WIDTH), BF16),
        scratch_shapes=[pltpu.VMEM((heads, GDN_DK, GDN_DV), F32)],
        compiler_params=_cparams(("arbitrary",), 40),
        name="gdn_scan",
    )(eg_flat, u, w, qe, ke, attn, z_b, norm_w.reshape(1, GDN_DV))


def _short_conv_kernel(bg_ref, cg_ref, x_ref, z_ref, cgh_ref, xh_ref, cw_ref, o_ref):
    i = pl.program_id(0)
    cur = cg_ref[...] * x_ref[...]
    halo = jnp.where(i > 0, cgh_ref[...] * xh_ref[...], 0.0)
    y = _causal_conv(cur, halo, cw_ref, SC_CONV)
    o_ref[...] = (bg_ref[...] * y * _silu(z_ref[...])).astype(o_ref.dtype)


def _short_conv(p_sc, conv_w, tm=256):
    t = p_sc.shape[0]
    tm = min(tm, t)
    hb = tm // SUBLANES
    main = lambda c: pl.BlockSpec((tm, SC_WIDTH), functools.partial(lambda i, c: (i, c), c=c))
    halo = lambda c: pl.BlockSpec((SUBLANES, SC_WIDTH),
                                  functools.partial(lambda i, c: (jnp.maximum(i * hb - 1, 0), c), c=c))
    return pl.pallas_call(
        _short_conv_kernel,
        grid=(t // tm,),
        in_specs=[main(0), main(1), main(2), main(3), halo(1), halo(2),
                  pl.BlockSpec((SC_CONV, SC_WIDTH), lambda i: (0, 0))],
        out_specs=pl.BlockSpec((tm, SC_WIDTH), lambda i: (i, 0)),
        out_shape=jax.ShapeDtypeStruct((t, SC_WIDTH), BF16),
        compiler_params=_cparams(("parallel",), 40),
        name="short_conv",
    )(p_sc, p_sc, p_sc, p_sc, p_sc, p_sc, conv_w)


_IN_GROUPS = (("nsa", _OFF_QA, _OFF_GATE), ("za", _OFF_ZA, _OFF_QB), ("gdn", _OFF_QB, _OFF_AB),
              ("zb", _OFF_ZB, _OFF_SC), ("sc", _OFF_SC, _OFF_MERGE), ("merge", _OFF_MERGE, None))


def _stage_w_in(w_in):
    depth, d, _ = w_in.shape
    w_t = jnp.transpose(w_in, (0, 2, 1)).astype(BF16)
    small_pad = LANES - 2 * GDN_HEADS - 3 * NSA_HEADS
    narrow = jnp.concatenate([w_in[:, :, _OFF_AB:_OFF_ZB], w_in[:, :, _OFF_GATE:_OFF_ZA],
                              jnp.zeros((depth, d, small_pad), F32)], axis=2)
    w_small_t = jnp.transpose(narrow, (0, 2, 1)).astype(BF16)
    return w_t, w_small_t


def _layer(x, layer, norm_w, w_t, w_small_t, nsa_qk_norm, cmp_pos, cmp_w1, cmp_w2, gdn_conv_w, gdn_a_log,
           gdn_dt_bias, gdn_norm_w, sc_conv_w, wb_nsa, wb_gdn, wb_sc, w_out, slopes):
    hn = _rmsnorm(x, norm_w)
    proj = {}
    for name, a, b in _IN_GROUPS:
        n = (w_t.shape[1] if b is None else b) - a
        proj[name] = _matmul_nt(hn, w_t, layer, a, n, F32, 1024, 1024, "in_proj_" + name)
    small = _matmul_nt(hn, w_small_t, layer, 0, LANES, F32, 1024, LANES, "in_proj_small")
    p_nsa, z_a, p_gdn, z_b, p_sc, p_mg = (proj[n] for n, _, _ in _IN_GROUPS)

    q, kv_c, k_s, v_s, k_w, v_w = _nsa_prep(p_nsa, nsa_qk_norm)
    kv_cmp = _compress(kv_c, cmp_pos, cmp_w1, cmp_w2, nsa_qk_norm[1])
    o_cmp, sel, blk_any = _cmp_attention(q, kv_cmp, slopes)
    o_slc = _sel_attention(q, k_s, v_s, sel, blk_any, slopes)
    o_win = _win_attention(q, k_w, v_w, slopes)
    o_a = _nsa_combine(small, o_cmp, o_slc, o_win, z_a)

    q_b, k_b, v_b, gb = _gdn_prep(p_gdn, small, gdn_conv_w, gdn_a_log, gdn_dt_bias)
    o_b = _gdn_chunks(q_b, k_b, v_b, gb, z_b, gdn_norm_w)

    o_c = _short_conv(p_sc, sc_conv_w)

    merged = _merge(o_a, o_b, o_c, wb_nsa, wb_gdn, wb_sc, layer, p_mg)
    return _matmul_residual(merged, w_out, layer, x, 1024, 512)


def kernel(x, norm_w, w_in, nsa_qk_norm, cmp_pos, cmp_w1, cmp_w2, gdn_conv_w, gdn_a_log, gdn_dt_bias,
           gdn_norm_w, sc_conv_w, w_branch_nsa, w_branch_gdn, w_branch_sc, w_out):
    b, t, d = x.shape
    depth = norm_w.shape[0]
    heads = jnp.arange(1, NSA_HEADS + 1, dtype=F32)
    slopes = jnp.exp2(-8.0 * heads / NSA_HEADS)
    w_t, w_small_t = _stage_w_in(w_in)
    wb_nsa, wb_gdn, wb_sc, w_out16 = (w.astype(BF16) for w in (w_branch_nsa, w_branch_gdn, w_branch_sc, w_out))
    outs = []
    for bi in range(b):
        xb = x[bi]
        for l in range(depth):
            xb = _layer(xb, l, norm_w[l], w_t, w_small_t, nsa_qk_norm[l], cmp_pos[l], cmp_w1[l], cmp_w2[l],
                        gdn_conv_w[l], gdn_a_log[l], gdn_dt_bias[l], gdn_norm_w[l], sc_conv_w[l],
                        wb_nsa, wb_gdn, wb_sc, w_out16, slopes)
        outs.append(xb)
    return jnp.stack(outs, axis=0)
```

```python
import functools
import math

import jax
import jax.numpy as jnp
from jax import lax
from jax.experimental import pallas as pl
from jax.experimental.pallas import tpu as pltpu

F32 = jnp.float32
BF16 = jnp.bfloat16

HEAD_DIM = 128
NSA_HEADS = 16
NSA_KV_HEADS = 4
NSA_GROUP = NSA_HEADS // NSA_KV_HEADS
NSA_WIDTH = NSA_HEADS * HEAD_DIM
NSA_KV_WIDTH = NSA_KV_HEADS * HEAD_DIM
CMP_BLOCK = 32
CMP_STRIDE = 16
CMP_HIDDEN = 256
SEL_BLOCK = 64
SEL_TOPK = 16
WINDOW = 512
GDN_HEADS = 16
GDN_DK = 128
GDN_DV = 128
GDN_KEY_WIDTH = GDN_HEADS * GDN_DK
GDN_VAL_WIDTH = GDN_HEADS * GDN_DV
GDN_CONV = 4
GDN_CHUNK = 64
SC_WIDTH = 2048
SC_CONV = 3
N_BRANCH = 3
NORM_EPS = 1e-6
NEG_INF = -1e30
MASK_BIAS = -2e30
FORCE_SCORE = 1e6

LANES = 128
SUBLANES = 8
BF16_SUBLANES = 16
N_BLK_LANES = 128
CMP_KEY_STEP = 128
MIB = 1024 * 1024

_OFF_QA = 0
_OFF_KVC = _OFF_QA + NSA_WIDTH
_OFF_KVS = _OFF_KVC + 2 * NSA_KV_WIDTH
_OFF_KVW = _OFF_KVS + 2 * NSA_KV_WIDTH
_OFF_GATE = _OFF_KVW + 2 * NSA_KV_WIDTH
_OFF_ZA = _OFF_GATE + 3 * NSA_HEADS
_OFF_QB = _OFF_ZA + NSA_WIDTH
_OFF_AB = _OFF_QB + 2 * GDN_KEY_WIDTH + GDN_VAL_WIDTH
_OFF_BETA = _OFF_AB + GDN_HEADS
_OFF_ZB = _OFF_BETA + GDN_HEADS
_OFF_SC = _OFF_ZB + GDN_VAL_WIDTH
_OFF_MERGE = _OFF_SC + 4 * SC_WIDTH

_SM_A = 0
_SM_BETA = GDN_HEADS
_SM_GATE = 2 * GDN_HEADS


def _cparams(sem, vmem_mib):
    return pltpu.CompilerParams(dimension_semantics=sem, vmem_limit_bytes=vmem_mib * MIB)


def _sigmoid(x):
    return jax.nn.sigmoid(x)


def _silu(x):
    return x * jax.nn.sigmoid(x)


def _dot(a, b):
    return jnp.dot(a, b, preferred_element_type=F32)


def _dot_nt(a, b):
    return lax.dot_general(a, b, (((1,), (1,)), ((), ())), preferred_element_type=F32)


def _dot_tn(a, b):
    return lax.dot_general(a, b, (((0,), (0,)), ((), ())), preferred_element_type=F32)


def _head_rms(x, w):
    return x * lax.rsqrt(jnp.mean(x * x, axis=-1, keepdims=True) + NORM_EPS) * w


def _rmsnorm_kernel(x_ref, w_ref, o_ref):
    x = x_ref[...]
    y = x * lax.rsqrt(jnp.mean(x * x, axis=-1, keepdims=True) + NORM_EPS)
    o_ref[...] = (y * w_ref[...]).astype(o_ref.dtype)


def _rmsnorm(x, w, tm=256):
    t, d = x.shape
    return pl.pallas_call(
        _rmsnorm_kernel,
        grid=(t // tm,),
        in_specs=[pl.BlockSpec((tm, d), lambda i: (i, 0)), pl.BlockSpec((1, d), lambda i: (0, 0))],
        out_specs=pl.BlockSpec((tm, d), lambda i: (i, 0)),
        out_shape=jax.ShapeDtypeStruct((t, d), BF16),
        compiler_params=_cparams(("parallel",), 32),
        name="rmsnorm",
    )(x, w.reshape(1, d))


def _mm_kernel(a_ref, b_ref, o_ref):
    o_ref[...] = _dot(a_ref[...], b_ref[...]).astype(o_ref.dtype)


def _mm_nt_kernel(a_ref, b_ref, o_ref):
    o_ref[...] = _dot_nt(a_ref[...], b_ref[...]).astype(o_ref.dtype)


def _matmul_nt(a, b_t, layer, row_off, n, out_dtype, tm, tn, name):
    m, k = a.shape
    tm = min(tm, m)
    tn = min(tn, n)
    depth, rows, _ = b_t.shape
    first_row = layer * rows + row_off
    assert first_row % BF16_SUBLANES == 0 and n % tn == 0
    b_t = b_t.reshape(depth * rows, k)
    return pl.pallas_call(
        _mm_nt_kernel,
        grid=(m // tm, n // tn),
        in_specs=[pl.BlockSpec((tm, k), lambda i, j: (i, 0)),
                  pl.BlockSpec((pl.Element(tn), pl.Element(k)),
                               lambda i, j: (pl.multiple_of(first_row + j * tn, BF16_SUBLANES), 0))],
        out_specs=pl.BlockSpec((tm, tn), lambda i, j: (i, j)),
        out_shape=jax.ShapeDtypeStruct((m, n), out_dtype),
        compiler_params=_cparams(("parallel", "arbitrary"), 52),
        name=name,
    )(a, b_t)


def _mm_res_kernel(a_ref, b_ref, r_ref, o_ref):
    o_ref[...] = r_ref[...] + _dot(a_ref[...], b_ref[...])


def _matmul_residual(a, b, layer, r, tm, tn):
    m, k = a.shape
    n = b.shape[2]
    tm = min(tm, m)
    return pl.pallas_call(
        _mm_res_kernel,
        grid=(m // tm, n // tn),
        in_specs=[pl.BlockSpec((tm, k), lambda i, j: (i, 0)),
                  pl.BlockSpec((None, k, tn), lambda i, j: (layer, 0, j)),
                  pl.BlockSpec((tm, tn), lambda i, j: (i, j))],
        out_specs=pl.BlockSpec((tm, tn), lambda i, j: (i, j)),
        out_shape=jax.ShapeDtypeStruct((m, n), F32),
        compiler_params=_cparams(("parallel", "arbitrary"), 48),
        name="out_proj_residual",
    )(a, b, r)


def _merge_kernel(oa_ref, ob_ref, oc_ref, wa_ref, wb_ref, wc_ref, ga_ref, gb_ref, gc_ref, o_ref):
    half = o_ref.shape[1] // 2
    cols = [slice(0, half), slice(half, 2 * half)]
    prods = [[_dot(a_ref[...], w_ref[:, c]) for a_ref, w_ref in ((oa_ref, wa_ref), (ob_ref, wb_ref), (oc_ref, wc_ref))]
             for c in cols]
    for c, (pa, pb, pc) in zip(cols, prods):
        acc = _sigmoid(ga_ref[:, c]) * pa
        acc = acc + _sigmoid(gb_ref[:, c]) * pb
        acc = acc + _sigmoid(gc_ref[:, c]) * pc
        o_ref[:, c] = acc.astype(o_ref.dtype)


def _merge(o_a, o_b, o_c, wa, wb, wc, layer, gates, tm=512, tn=512):
    m, k = o_a.shape
    n = wa.shape[2]
    tm = min(tm, m)
    nb = n // tn
    a_spec = pl.BlockSpec((tm, k), lambda i, j: (i, 0))
    w_spec = pl.BlockSpec((None, k, tn), lambda i, j: (layer, 0, j))
    g_specs = [pl.BlockSpec((tm, tn), functools.partial(lambda i, j, br: (i, br * nb + j), br=br))
               for br in range(N_BRANCH)]
    return pl.pallas_call(
        _merge_kernel,
        grid=(m // tm, nb),
        in_specs=[a_spec, a_spec, a_spec, w_spec, w_spec, w_spec] + g_specs,
        out_specs=pl.BlockSpec((tm, tn), lambda i, j: (i, j)),
        out_shape=jax.ShapeDtypeStruct((m, n), BF16),
        compiler_params=_cparams(("parallel", "arbitrary"), 48),
        name="branch_merge",
    )(o_a, o_b, o_c, wa, wb, wc, gates, gates, gates)


def _nsa_prep_kernel(p_ref, nw_ref, q_ref, kvc_ref, ks_ref, vs_ref, kw_ref, vw_ref):
    nw = nw_ref[...]
    for h in range(NSA_HEADS):
        sl = slice(h * HEAD_DIM, (h + 1) * HEAD_DIM)
        q_ref[:, sl] = (_head_rms(p_ref[:, sl], nw[0:1]) * (HEAD_DIM ** -0.5)).astype(BF16)
    kvc_ref[...] = p_ref[:, _OFF_KVC:_OFF_KVS]
    for g in range(NSA_KV_HEADS):
        sl = slice(g * HEAD_DIM, (g + 1) * HEAD_DIM)
        ks = p_ref[:, _OFF_KVS + g * HEAD_DIM:_OFF_KVS + (g + 1) * HEAD_DIM]
        ks_ref[:, sl] = _head_rms(ks, nw[2:3]).astype(BF16)
        vs_ref[:, sl] = p_ref[:, _OFF_KVS + NSA_KV_WIDTH + g * HEAD_DIM:
                              _OFF_KVS + NSA_KV_WIDTH + (g + 1) * HEAD_DIM].astype(BF16)
        kw = p_ref[:, _OFF_KVW + g * HEAD_DIM:_OFF_KVW + (g + 1) * HEAD_DIM]
        kw_ref[:, sl] = _head_rms(kw, nw[3:4]).astype(BF16)
        vw_ref[:, sl] = p_ref[:, _OFF_KVW + NSA_KV_WIDTH + g * HEAD_DIM:
                              _OFF_KVW + NSA_KV_WIDTH + (g + 1) * HEAD_DIM].astype(BF16)


def _nsa_prep(p_nsa, qk_norm, tm=256):
    t, width = p_nsa.shape
    row = lambda i: (i, 0)
    return pl.pallas_call(
        _nsa_prep_kernel,
        grid=(t // tm,),
        in_specs=[pl.BlockSpec((tm, width), row), pl.BlockSpec((4, HEAD_DIM), lambda i: (0, 0))],
        out_specs=[pl.BlockSpec((tm, NSA_WIDTH), row)] + [pl.BlockSpec((tm, 2 * NSA_KV_WIDTH), row)]
        + [pl.BlockSpec((tm, NSA_KV_WIDTH), row)] * 4,
        out_shape=[jax.ShapeDtypeStruct((t, NSA_WIDTH), BF16), jax.ShapeDtypeStruct((t, 2 * NSA_KV_WIDTH), F32)]
        + [jax.ShapeDtypeStruct((t, NSA_KV_WIDTH), BF16)] * 4,
        compiler_params=_cparams(("parallel",), 40),
        name="nsa_prep",
    )(p_nsa, qk_norm)


def _compress_kernel(x_ref, pos_ref, w1lo_ref, w1hi_ref, w2_ref, nw_ref, o_ref, lo_acc, hi_acc, *, n_chunk):
    kv = pl.program_id(0)
    l = pl.program_id(1)

    @pl.when(l == 0)
    def _():
        lo_acc[...] = jnp.zeros(lo_acc.shape, F32)
        hi_acc[...] = jnp.zeros(hi_acc.shape, F32)

    pos_lo = pos_ref[0, pl.ds(l, 1), :]
    pos_hi = pos_ref[0, pl.ds(l + CMP_STRIDE, 1), :]
    w_lo = w1lo_ref[0, 0]
    w_hi = w1hi_ref[0, 0]
    for g in range(NSA_KV_HEADS):
        x = x_ref[:, g * HEAD_DIM:(g + 1) * HEAD_DIM]
        lo_acc[g] += _dot((x + pos_lo).astype(BF16), w_lo)
        hi_acc[g] += _dot((x + pos_hi).astype(BF16), w_hi)

    @pl.when(l == CMP_STRIDE - 1)
    def _():
        for g in range(NSA_KV_HEADS):
            hidden = _silu(lo_acc[g] + pltpu.roll(hi_acc[g], shift=n_chunk - 1, axis=0))
            out = _dot(hidden.astype(BF16), w2_ref[0])
            normed = _head_rms(out, nw_ref[...])
            o_ref[0, g] = jnp.where(kv == 0, normed, out).astype(BF16)


def _compress(kv_c, cmp_pos, cmp_w1, cmp_w2, k_norm_w):
    t = kv_c.shape[0]
    n_chunk = t // CMP_STRIDE
    x = kv_c.reshape(n_chunk, CMP_STRIDE * 2 * NSA_KV_WIDTH)
    w1 = cmp_w1.astype(BF16)
    w2 = cmp_w2.astype(BF16)
    return pl.pallas_call(
        functools.partial(_compress_kernel, n_chunk=n_chunk),
        grid=(2, CMP_STRIDE),
        in_specs=[pl.BlockSpec((n_chunk, NSA_KV_WIDTH), lambda a, l: (0, 2 * l + a)),
                  pl.BlockSpec((1, CMP_BLOCK, HEAD_DIM), lambda a, l: (a, 0, 0)),
                  pl.BlockSpec((1, 1, HEAD_DIM, CMP_HIDDEN), lambda a, l: (a, l, 0, 0)),
                  pl.BlockSpec((1, 1, HEAD_DIM, CMP_HIDDEN), lambda a, l: (a, l + CMP_STRIDE, 0, 0)),
                  pl.BlockSpec((1, CMP_HIDDEN, HEAD_DIM), lambda a, l: (a, 0, 0)),
                  pl.BlockSpec((1, HEAD_DIM), lambda a, l: (0, 0))],
        out_specs=pl.BlockSpec((1, NSA_KV_HEADS, n_chunk, HEAD_DIM), lambda a, l: (a, 0, 0, 0)),
        out_shape=jax.ShapeDtypeStruct((2, NSA_KV_HEADS, n_chunk, HEAD_DIM), BF16),
        scratch_shapes=[pltpu.VMEM((NSA_KV_HEADS, n_chunk, CMP_HIDDEN), F32),
                        pltpu.VMEM((NSA_KV_HEADS, n_chunk, CMP_HIDDEN), F32)],
        compiler_params=_cparams(("arbitrary", "arbitrary"), 40),
        name="nsa_compress",
    )(x, cmp_pos, w1, w1, w2, k_norm_w.reshape(1, HEAD_DIM))


def _cmp_attn_kernel(slopes_ref, q_ref, k_ref, v_ref, cov_ref, o_ref, sel_ref, any_ref, imp_sc, *, tq, n_chunk):
    g = pl.program_id(0)
    qi = pl.program_id(1)
    t0 = qi * tq
    sls = [slice(z * HEAD_DIM, (z + 1) * HEAD_DIM) for z in range(NSA_GROUP)]

    def attend(nk):
        k = k_ref[0, 0, 0:nk, :]
        v_t = v_ref[0, 0, 0:nk, :].T
        t_pos = t0 + lax.broadcasted_iota(jnp.int32, (nk, tq), 1)
        k_end = lax.broadcasted_iota(jnp.int32, (nk, tq), 0) * CMP_STRIDE + (CMP_BLOCK - 1)
        mask_bias = jnp.where(t_pos >= k_end, 0.0, MASK_BIAS)
        k_rel = (k_end - t0).astype(F32)
        psum = jnp.zeros((nk, tq), F32)
        qk_next = _dot_nt(k, q_ref[:, sls[0]])
        pending = None
        for z in range(NSA_GROUP):
            qk = qk_next
            if z + 1 < NSA_GROUP:
                qk_next = _dot_nt(k, q_ref[:, sls[z + 1]])
            s = qk + (mask_bias + slopes_ref[g * NSA_GROUP + z] * k_rel)
            m = jnp.maximum(jnp.max(s, axis=0, keepdims=True), NEG_INF)
            e = jnp.exp(s - m)
            p = e * (1.0 / jnp.maximum(jnp.sum(e, axis=0, keepdims=True), 1e-30))
            psum = psum + p
            if pending is not None:
                o_ref[:, sls[pending[0]]] = _dot(v_t, pending[1]).T
            pending = (z, p.astype(BF16))
        o_ref[:, sls[pending[0]]] = _dot(v_t, pending[1]).T
        p_hi = psum.astype(BF16)
        p_lo = (psum - p_hi.astype(F32)).astype(BF16)
        cov_t = cov_ref[:, 0:nk]
        imp_sc[...] = _dot(cov_t, p_hi) + _dot(cov_t, p_lo)

    visible = (t0 + tq - CMP_BLOCK) // CMP_STRIDE + 1
    n_var = n_chunk // CMP_KEY_STEP if n_chunk % CMP_KEY_STEP == 0 else 1
    if n_var <= 1:
        attend(n_chunk)
    else:
        for var in range(1, n_var + 1):
            lo, hi = (var - 1) * CMP_KEY_STEP, var * CMP_KEY_STEP
            if var == 1:
                cond = visible <= hi
            elif var < n_var:
                cond = (visible > lo) & (visible <= hi)
            else:
                cond = visible > lo
            pl.when(cond)(functools.partial(attend, hi))
    imp = imp_sc[...]

    j = lax.broadcasted_iota(jnp.int32, (N_BLK_LANES, tq), 0)
    cur = (t0 + lax.broadcasted_iota(jnp.int32, (N_BLK_LANES, tq), 1)) // SEL_BLOCK
    valid = j <= cur
    forced = (j == 0) | (j == cur) | (j == cur - 1)
    chosen_first = valid & forced
    val = jnp.where(valid, jnp.where(forced, -2.0, imp), -1.0)
    sel = chosen_first.astype(F32)
    jf = j.astype(F32)
    for _ in range(SEL_TOPK - 3):
        m = jnp.max(val, axis=0, keepdims=True)
        idx = jnp.min(jnp.where(val == m, jf, float(N_BLK_LANES)), axis=0, keepdims=True)
        hit = jf == idx
        sel = jnp.where(hit & (m > -0.5), 1.0, sel)
        val = jnp.where(hit, -2.0, val)
    sel_q = sel.T
    sel_ref[0] = sel_q.astype(BF16)
    any_ref[0, 0] = jnp.broadcast_to(jnp.max(sel_q, axis=0, keepdims=True), (SUBLANES, N_BLK_LANES))


def _cmp_attention(q, kv_cmp, slopes, tq=256):
    t = q.shape[0]
    n_chunk = kv_cmp.shape[2]
    n_cmp = n_chunk - CMP_BLOCK // CMP_STRIDE + 1
    n_blk = t // SEL_BLOCK
    assert n_blk <= N_BLK_LANES
    tq = min(tq, t)
    cs = jnp.arange(n_chunk)[:, None] * CMP_STRIDE
    bs = jnp.arange(N_BLK_LANES)[None, :] * SEL_BLOCK
    cover = ((cs <= bs + SEL_BLOCK - 1) & (cs + CMP_BLOCK - 1 >= bs)
             & (jnp.arange(n_chunk)[:, None] < n_cmp) & (jnp.arange(N_BLK_LANES)[None, :] < n_blk))
    cover = cover.astype(BF16).T
    return pl.pallas_call(
        functools.partial(_cmp_attn_kernel, tq=tq, n_chunk=n_chunk),
        grid=(NSA_KV_HEADS, t // tq),
        in_specs=[pl.BlockSpec(memory_space=pltpu.SMEM),
                  pl.BlockSpec((tq, NSA_GROUP * HEAD_DIM), lambda g, i: (i, g)),
                  pl.BlockSpec((1, 1, n_chunk, HEAD_DIM), lambda g, i: (0, g, 0, 0)),
                  pl.BlockSpec((1, 1, n_chunk, HEAD_DIM), lambda g, i: (1, g, 0, 0)),
                  pl.BlockSpec((N_BLK_LANES, n_chunk), lambda g, i: (0, 0))],
        out_specs=[pl.BlockSpec((tq, NSA_GROUP * HEAD_DIM), lambda g, i: (i, g)),
                   pl.BlockSpec((1, tq, N_BLK_LANES), lambda g, i: (g, i, 0)),
                   pl.BlockSpec((1, 1, SUBLANES, N_BLK_LANES), lambda g, i: (g, i, 0, 0))],
        out_shape=[jax.ShapeDtypeStruct((t, NSA_WIDTH), F32),
                   jax.ShapeDtypeStruct((NSA_KV_HEADS, t, N_BLK_LANES), BF16),
                   jax.ShapeDtypeStruct((NSA_KV_HEADS, t // tq, SUBLANES, N_BLK_LANES), F32)],
        scratch_shapes=[pltpu.VMEM((N_BLK_LANES, tq), F32)],
        compiler_params=_cparams(("parallel", "parallel"), 40),
        name="nsa_cmp_attn",
    )(slopes, q, kv_cmp, kv_cmp, cover)


def _sel_attn_kernel(wl_ref, slopes_ref, q_ref, k_ref, v_ref, eb_ref, sel_ref, o_ref,
                     qa_sc, m_sc, l_sc, acc_sc, *, tq, tk, max_tiles):
    g = pl.program_id(0)
    qi = pl.program_id(1)
    item = g * pl.num_programs(1) + qi
    t0 = qi * tq

    m_sc[...] = jnp.full(m_sc.shape, NEG_INF, F32)
    l_sc[...] = jnp.zeros(l_sc.shape, F32)
    acc_sc[...] = jnp.zeros(acc_sc.shape, F32)
    unselected = sel_ref[0] - 1.0
    for z in range(NSA_GROUP):
        qa_sc[z] = jnp.concatenate([q_ref[:, z * HEAD_DIM:(z + 1) * HEAD_DIM], unselected], axis=1)

    def tile(kj, causal):
        k0 = kj * tk
        rows = pl.ds(pl.multiple_of(k0, tk), tk)
        k_aug = jnp.concatenate([k_ref[rows, :], eb_ref[kj]], axis=1)
        v_t = v_ref[rows, :].T
        row = lax.broadcasted_iota(jnp.int32, (tk, tq), 0)
        k_rel = (k0 - t0 + row).astype(F32)
        if causal:
            col = lax.broadcasted_iota(jnp.int32, (tk, tq), 1)
            causal_bias = jnp.where(t0 + col >= k0 + row, 0.0, MASK_BIAS)
        qk_next = _dot_nt(k_aug, qa_sc[0])
        pending = None
        for z in range(NSA_GROUP):
            qk = qk_next
            if z + 1 < NSA_GROUP:
                qk_next = _dot_nt(k_aug, qa_sc[z + 1])
            s = qk + slopes_ref[g * NSA_GROUP + z] * k_rel
            if causal:
                s = s + causal_bias
            m_prev = m_sc[z]
            m_new = jnp.maximum(m_prev, jnp.max(s, axis=0, keepdims=True))
            alpha = jnp.exp(m_prev - m_new)
            e = jnp.exp(s - m_new)
            l_sc[z] = alpha * l_sc[z] + jnp.sum(e, axis=0, keepdims=True)
            m_sc[z] = m_new
            if pending is not None:
                pz, palpha, pe = pending
                acc_sc[pz] = palpha * acc_sc[pz] + _dot(v_t, pe)
            pending = (z, alpha, e.astype(BF16))
        pz, palpha, pe = pending
        acc_sc[pz] = palpha * acc_sc[pz] + _dot(v_t, pe)

    def visit(i, carry):
        tile(wl_ref[item, i], False)
        return carry

    lax.fori_loop(0, wl_ref[item, max_tiles], visit, 0)
    tile(t0 // tk, True)

    for z in range(NSA_GROUP):
        out_t = acc_sc[z] * (1.0 / jnp.maximum(l_sc[z], 1e-30))
        o_ref[:, z * HEAD_DIM:(z + 1) * HEAD_DIM] = out_t.T


def _sel_work_list(blk_any, t, tq, tk):
    nq, nk, bpt = t // tq, t // tk, tk // SEL_BLOCK
    n_blk = t // SEL_BLOCK
    picked = blk_any[:, :, 0, :n_blk].reshape(NSA_KV_HEADS, nq, nk, bpt).max(axis=-1) > 0.5
    diagonal = (jnp.arange(nq) * tq) // tk
    before = picked & (jnp.arange(nk)[None, :] < diagonal[:, None])[None]
    count = before.sum(axis=-1, keepdims=True).astype(jnp.int32)
    tiles = jnp.argsort(jnp.logical_not(before), axis=-1, stable=True).astype(jnp.int32)
    return jnp.concatenate([tiles, count], axis=-1).reshape(NSA_KV_HEADS * nq, nk + 1)


def _sel_attention(q, k_s, v_s, sel, blk_any, slopes, tq=256, tk=512):
    t = q.shape[0]
    tq = min(tq, t)
    tk = min(tk, t)
    assert tk % tq == 0
    nk = t // tk
    work_list = _sel_work_list(blk_any, t, tq, tk)

    blk_of_key = jnp.arange(t)[:, None] // SEL_BLOCK
    block_onehot = jnp.where(blk_of_key == jnp.arange(N_BLK_LANES)[None, :], -MASK_BIAS, 0.0)
    block_onehot = block_onehot.astype(BF16).reshape(nk, tk, N_BLK_LANES)

    smem = pl.BlockSpec(memory_space=pltpu.SMEM)
    return pl.pallas_call(
        functools.partial(_sel_attn_kernel, tq=tq, tk=tk, max_tiles=nk),
        grid=(NSA_KV_HEADS, t // tq),
        in_specs=[smem, smem,
                  pl.BlockSpec((tq, NSA_GROUP * HEAD_DIM), lambda g, i: (i, g)),
                  pl.BlockSpec((t, HEAD_DIM), lambda g, i: (0, g)),
                  pl.BlockSpec((t, HEAD_DIM), lambda g, i: (0, g)),
                  pl.BlockSpec((nk, tk, N_BLK_LANES), lambda g, i: (0, 0, 0)),
                  pl.BlockSpec((1, tq, N_BLK_LANES), lambda g, i: (g, i, 0))],
        out_specs=pl.BlockSpec((tq, NSA_GROUP * HEAD_DIM), lambda g, i: (i, g)),
        out_shape=jax.ShapeDtypeStruct((t, NSA_WIDTH), F32),
        scratch_shapes=[pltpu.VMEM((NSA_GROUP, tq, 2 * HEAD_DIM), BF16),
                        pltpu.VMEM((NSA_GROUP, 1, tq), F32), pltpu.VMEM((NSA_GROUP, 1, tq), F32),
                        pltpu.VMEM((NSA_GROUP, HEAD_DIM, tq), F32)],
        compiler_params=_cparams(("arbitrary", "arbitrary"), 40),
        name="nsa_sel_attn",
    )(work_list, slopes, q, k_s, v_s, block_onehot, sel)


def _win_attn_kernel(slopes_ref, q_ref, *refs, tq, n_tile):
    k_refs = refs[:n_tile]
    v_refs = refs[n_tile:2 * n_tile]
    o_ref = refs[2 * n_tile]
    g = pl.program_id(0)
    qi = pl.program_id(1)
    span = n_tile * tq
    k = jnp.concatenate([r[...] for r in k_refs], axis=0)
    v = jnp.concatenate([r[...] for r in v_refs], axis=0)
    t_pos = qi * tq + lax.broadcasted_iota(jnp.int32, (tq, span), 0)
    k_pos = (qi - (n_tile - 1)) * tq + lax.broadcasted_iota(jnp.int32, (tq, span), 1)
    dist = t_pos - k_pos
    mask_bias = jnp.where((dist >= 0) & (dist < WINDOW) & (k_pos >= 0), 0.0, MASK_BIAS)
    k_rel = (lax.broadcasted_iota(jnp.int32, (1, span), 1) - (n_tile - 1) * tq).astype(F32)
    sls = [slice(z * HEAD_DIM, (z + 1) * HEAD_DIM) for z in range(NSA_GROUP)]
    qk_next = _dot_nt(q_ref[:, sls[0]], k)
    pending = None
    for z in range(NSA_GROUP):
        qk = qk_next
        if z + 1 < NSA_GROUP:
            qk_next = _dot_nt(q_ref[:, sls[z + 1]], k)
        s = qk + (mask_bias + slopes_ref[g * NSA_GROUP + z] * k_rel)
        m = jnp.maximum(jnp.max(s, axis=-1, keepdims=True), NEG_INF)
        e = jnp.exp(s - m)
        inv_l = 1.0 / jnp.maximum(jnp.sum(e, axis=-1, keepdims=True), 1e-30)
        if pending is not None:
            o_ref[:, sls[pending[0]]] = _dot(pending[1], v) * pending[2]
        pending = (z, e.astype(BF16), inv_l)
    o_ref[:, sls[pending[0]]] = _dot(pending[1], v) * pending[2]


def _win_attention(q, k_w, v_w, slopes, tq=256):
    t = q.shape[0]
    tq = min(tq, t)
    n_tile = -(-WINDOW // tq) + 1

    def kv_spec(c):
        return pl.BlockSpec((tq, HEAD_DIM), lambda g, i: (jnp.maximum(i - (n_tile - 1) + c, 0), g))

    return pl.pallas_call(
        functools.partial(_win_attn_kernel, tq=tq, n_tile=n_tile),
        grid=(NSA_KV_HEADS, t // tq),
        in_specs=[pl.BlockSpec(memory_space=pltpu.SMEM),
                  pl.BlockSpec((tq, NSA_GROUP * HEAD_DIM), lambda g, i: (i, g))]
        + [kv_spec(c) for c in range(n_tile)] * 2,
        out_specs=pl.BlockSpec((tq, NSA_GROUP * HEAD_DIM), lambda g, i: (i, g)),
        out_shape=jax.ShapeDtypeStruct((t, NSA_WIDTH), F32),
        compiler_params=_cparams(("parallel", "parallel"), 40),
        name="nsa_win_attn",
    )(slopes, q, *([k_w] * n_tile), *([v_w] * n_tile))


def _nsa_combine_kernel(sm_ref, ex_ref, oc_ref, os_ref, ow_ref, z_ref, o_ref):
    logits = sm_ref[...]
    hi = logits.astype(BF16)
    lo = (logits - hi.astype(F32)).astype(BF16)
    acc = None
    for br, branch_ref in enumerate((oc_ref, os_ref, ow_ref)):
        ex = ex_ref[br]
        gate = _sigmoid(_dot(hi, ex) + _dot(lo, ex))
        term = gate * branch_ref[...]
        acc = term if acc is None else acc + term
    o_ref[...] = (acc * _silu(z_ref[...])).astype(o_ref.dtype)


def _nsa_combine(small, o_cmp, o_slc, o_win, z_a, tm=256):
    t = small.shape[0]
    tm = min(tm, t)
    lane = jnp.arange(LANES)[None, :, None]
    br = jnp.arange(N_BRANCH)[:, None, None]
    head = (jnp.arange(NSA_WIDTH) // HEAD_DIM)[None, None, :]
    expand = (lane == _SM_GATE + head * N_BRANCH + br).astype(BF16)
    row = lambda i: (i, 0)
    wide = pl.BlockSpec((tm, NSA_WIDTH), row)
    return pl.pallas_call(
        _nsa_combine_kernel,
        grid=(t // tm,),
        in_specs=[pl.BlockSpec((tm, LANES), row),
                  pl.BlockSpec((N_BRANCH, LANES, NSA_WIDTH), lambda i: (0, 0, 0)),
                  wide, wide, wide, wide],
        out_specs=wide,
        out_shape=jax.ShapeDtypeStruct((t, NSA_WIDTH), BF16),
        compiler_params=_cparams(("parallel",), 40),
        name="nsa_combine",
    )(small, expand, o_cmp, o_slc, o_win, z_a)


def _shift_rows(cur, halo, s):
    rolled = pltpu.roll(cur, shift=s, axis=0)
    halo_rolled = pltpu.roll(halo, shift=s, axis=0)
    row = lax.broadcasted_iota(jnp.int32, halo.shape, 0)
    head = jnp.where(row < s, halo_rolled, rolled[0:SUBLANES])
    return jnp.concatenate([head, rolled[SUBLANES:]], axis=0)


def _causal_conv(cur, halo, w_ref, k):
    acc = None
    for j in range(k):
        s = k - 1 - j
        term = (cur if s == 0 else _shift_rows(cur, halo, s)) * w_ref[j:j + 1, :]
        acc = term if acc is None else acc + term
    return acc


def _gdn_prep_kernel(p_ref, halo_ref, cw_ref, sm_ref, alog_ref, dtb_ref, q_ref, k_ref, v_ref, gb_ref, xx):
    i = pl.program_id(0)
    tm = p_ref.shape[0]
    xx[0:SUBLANES, :] = jnp.where(i > 0, halo_ref[...], 0.0)
    xx[SUBLANES:, :] = p_ref[...]

    def conv_silu(cols):
        acc = None
        for j in range(GDN_CONV):
            delay = GDN_CONV - 1 - j
            term = xx[SUBLANES - delay:SUBLANES - delay + tm, cols] * cw_ref[j:j + 1, cols]
            acc = term if acc is None else acc + term
        return _silu(acc)

    for h in range(GDN_HEADS):
        sl = slice(h * GDN_DK, (h + 1) * GDN_DK)
        qh = conv_silu(sl)
        q_ref[:, sl] = qh * lax.rsqrt(jnp.sum(qh * qh, axis=-1, keepdims=True) + NORM_EPS) * (GDN_DK ** -0.5)
        kh = conv_silu(slice(GDN_KEY_WIDTH + h * GDN_DK, GDN_KEY_WIDTH + (h + 1) * GDN_DK))
        k_ref[:, sl] = kh * lax.rsqrt(jnp.sum(kh * kh, axis=-1, keepdims=True) + NORM_EPS)
        v_ref[:, sl] = conv_silu(slice(2 * GDN_KEY_WIDTH + h * GDN_DV, 2 * GDN_KEY_WIDTH + (h + 1) * GDN_DV))
    sm = sm_ref[...]
    lane = lax.broadcasted_iota(jnp.int32, sm.shape, 1)
    gdecay = -jnp.exp(alog_ref[...]) * jax.nn.softplus(sm + dtb_ref[...])
    beta = _sigmoid(sm)
    gb_ref[...] = jnp.where(lane < _SM_BETA, gdecay, jnp.where(lane < _SM_GATE, beta, 0.0))


def _gdn_prep(p_gdn, small, conv_w, a_log, dt_bias, tm=256):
    t, width = p_gdn.shape
    tm = min(tm, t)
    hb = tm // SUBLANES
    row = lambda i: (i, 0)
    alog = jnp.zeros((1, LANES), F32).at[0, _SM_A:_SM_A + GDN_HEADS].set(a_log)
    dtb = jnp.zeros((1, LANES), F32).at[0, _SM_A:_SM_A + GDN_HEADS].set(dt_bias)
    const = lambda i: (0, 0)
    return pl.pallas_call(
        _gdn_prep_kernel,
        grid=(t // tm,),
        in_specs=[pl.BlockSpec((tm, width), row),
                  pl.BlockSpec((SUBLANES, width), lambda i: (jnp.maximum(i * hb - 1, 0), 0)),
                  pl.BlockSpec((GDN_CONV, width), const),
                  pl.BlockSpec((tm, LANES), row),
                  pl.BlockSpec((1, LANES), const), pl.BlockSpec((1, LANES), const)],
        out_specs=[pl.BlockSpec((tm, GDN_KEY_WIDTH), row), pl.BlockSpec((tm, GDN_KEY_WIDTH), row),
                   pl.BlockSpec((tm, GDN_VAL_WIDTH), row), pl.BlockSpec((tm, LANES), row)],
        out_shape=[jax.ShapeDtypeStruct((t, GDN_KEY_WIDTH), F32), jax.ShapeDtypeStruct((t, GDN_KEY_WIDTH), F32),
                   jax.ShapeDtypeStruct((t, GDN_VAL_WIDTH), F32), jax.ShapeDtypeStruct((t, LANES), F32)],
        scratch_shapes=[pltpu.VMEM((tm + SUBLANES, width), F32)],
        compiler_params=_cparams(("parallel",), 48),
        name="gdn_prep",
    )(p_gdn, p_gdn, conv_w, small, alog, dtb)


def _row_pad(x):
    return jnp.concatenate([x, jnp.zeros_like(x)], axis=0)


def _gdn_parallel_stages(q_ref, k_ref, v_ref, gb_ref, rows, heads, out):
    c = GDN_CHUNK
    gb = gb_ref[rows, :]
    row = lax.broadcasted_iota(jnp.int32, gb.shape, 0)
    gcum = gb
    shift = 1
    while shift < c:
        gcum = gcum + jnp.where(row >= shift, pltpu.roll(gcum, shift=shift, axis=0), 0.0)
        shift *= 2
    lane = lax.broadcasted_iota(jnp.int32, gb.shape, 1)
    ri = lax.broadcasted_iota(jnp.int32, (c, LANES), 0)
    ci = lax.broadcasted_iota(jnp.int32, (c, LANES), 1)
    tri = ri >= ci
    strict = ri > ci
    eye = ri == ci
    eyef = eye.astype(F32)
    hs = range(heads)
    sls = [slice(h * GDN_DK, (h + 1) * GDN_DK) for h in hs]

    gcol = [jnp.sum(jnp.where(lane == _SM_A + h, gcum, 0.0), axis=-1, keepdims=True) for h in hs]
    beta = [jnp.sum(jnp.where(lane == _SM_BETA + h, gb, 0.0), axis=-1, keepdims=True) for h in hs]
    decay = []
    for h in hs:
        gmat = jnp.broadcast_to(gcol[h], (c, LANES))
        grow = jnp.sum(jnp.where(eye, gmat, 0.0), axis=0, keepdims=True)
        decay.append(jnp.where(tri, jnp.exp(jnp.where(tri, gmat - grow, 0.0)), 0.0))
    glast = [gcol[h][c - 1:c, :] for h in hs]
    egc = [jnp.exp(gcol[h]) for h in hs]
    q = [q_ref[rows, sls[h]] for h in hs]
    k = [k_ref[rows, sls[h]] for h in hs]
    k16 = [k[h].astype(BF16) for h in hs]
    qk = [_dot_nt(jnp.concatenate([q[h].astype(BF16), k16[h]], axis=0), _row_pad(k16[h])) for h in hs]
    yield
    out["at16"] = [(qk[h][:c] * decay[h]).astype(BF16) for h in hs]

    pw = [-jnp.where(strict, beta[h] * qk[h][c:] * decay[h], 0.0) for h in hs]
    inv = [eyef + pw[h] for h in hs]
    pw16 = [pw[h].astype(BF16) for h in hs]
    pw = [_dot(pw16[h], _row_pad(pw16[h])) for h in hs]
    yield
    span = 2
    while span < c:
        pw16 = [pw[h].astype(BF16) for h in hs]
        if 2 * span < c:
            both = [_dot(jnp.concatenate([pw16[h], inv[h].astype(BF16)], axis=0), _row_pad(pw16[h])) for h in hs]
            pw = [both[h][:c] for h in hs]
            inv = [inv[h] + both[h][c:] for h in hs]
        else:
            inv = [inv[h] + _dot(inv[h].astype(BF16), _row_pad(pw16[h])) for h in hs]
        span *= 2
        yield

    uw = []
    for h in hs:
        rhs = jnp.concatenate([(v_ref[rows, sls[h]] * beta[h]).astype(BF16),
                               (k[h] * (beta[h] * egc[h])).astype(BF16)], axis=1)
        uw.append(_dot(inv[h].astype(BF16), _row_pad(rhs)))
    yield
    out["u"] = [uw[h][:, :GDN_DV] for h in hs]
    out["w16"] = [uw[h][:, GDN_DV:].astype(BF16) for h in hs]
    out["qe16"] = [(q[h] * egc[h]).astype(BF16) for h in hs]
    out["ke16"] = [(k[h] * jnp.exp(glast[h] - gcol[h])).astype(BF16) for h in hs]
    out["eg"] = [jnp.exp(glast[h]) for h in hs]


def _gdn_state_stages(parts, s_cur, z_ref, nw, o_ref, rows, heads):
    c = GDN_CHUNK
    hs = range(heads)
    sls = [slice(h * GDN_DK, (h + 1) * GDN_DK) for h in hs]
    ws_qs = [_dot(jnp.concatenate([parts["w16"][h], parts["qe16"][h]], axis=0), s_cur[h].astype(BF16))
             for h in hs]
    yield
    v_new16 = [(parts["u"][h] - ws_qs[h][:c]).astype(BF16) for h in hs]
    o = [ws_qs[h][c:] + _dot(parts["at16"][h], _row_pad(v_new16[h])) for h in hs]
    yield
    s_new = [s_cur[h] * parts["eg"][h] + _dot_tn(parts["ke16"][h], v_new16[h]) for h in hs]
    for h in hs:
        s_cur[h] = s_new[h]
    yield
    for h in hs:
        o_ref[rows, sls[h]] = (_head_rms(o[h], nw) * _silu(z_ref[rows, sls[h]])).astype(o_ref.dtype)


def _gdn_fused_kernel(q_ref, k_ref, v_ref, gb_ref, z_ref, nw_ref, o_ref, state, *, heads, cps):
    n = pl.program_id(0)
    c = GDN_CHUNK

    @pl.when(n == 0)
    def _():
        state[...] = jnp.zeros(state.shape, F32)

    nw = nw_ref[...]
    s_cur = [state[h] for h in range(heads)]
    prev = None
    for cc in range(cps):
        rows = slice(cc * c, (cc + 1) * c)
        parts = {}
        gens = [_gdn_parallel_stages(q_ref, k_ref, v_ref, gb_ref, rows, heads, parts)]
        if prev is not None:
            gens.insert(0, _gdn_state_stages(prev[0], s_cur, z_ref, nw, o_ref, prev[1], heads))
        while gens:
            for gen in list(gens):
                if next(gen, "done") == "done":
                    gens.remove(gen)
        prev = (parts, rows)
    for _ in _gdn_state_stages(prev[0], s_cur, z_ref, nw, o_ref, prev[1], heads):
        pass
    for h in range(heads):
        state[h] = s_cur[h]


def _gdn_fused(q, k, v, gb, z_b, norm_w, cps=4):
    t = q.shape[0]
    c = GDN_CHUNK
    n_chunk = t // c
    while n_chunk % cps:
        cps //= 2
    heads = GDN_HEADS
    wide = pl.BlockSpec((cps * c, GDN_VAL_WIDTH), lambda n: (n, 0))
    return pl.pallas_call(
        functools.partial(_gdn_fused_kernel, heads=heads, cps=cps),
        grid=(n_chunk // cps,),
        in_specs=[wide, wide, wide, pl.BlockSpec((cps * c, LANES), lambda n: (n, 0)), wide,
                  pl.BlockSpec((1, GDN_DV), lambda n: (0, 0))],
        out_specs=wide,
        out_shape=jax.ShapeDtypeStruct((t, GDN_VAL_WIDTH), BF16),
        scratch_shapes=[pltpu.VMEM((heads, GDN_DK, GDN_DV), F32)],
        compiler_params=_cparams(("arbitrary",), 48),
        name="gdn_fused",
    )(q, k, v, gb, z_b, norm_w.reshape(1, GDN_DV))


def _short_conv_kernel(bg_ref, cg_ref, x_ref, z_ref, cgh_ref, xh_ref, cw_ref, o_ref):
    i = pl.program_id(0)
    cur = cg_ref[...] * x_ref[...]
    halo = jnp.where(i > 0, cgh_ref[...] * xh_ref[...], 0.0)
    y = _causal_conv(cur, halo, cw_ref, SC_CONV)
    o_ref[...] = (bg_ref[...] * y * _silu(z_ref[...])).astype(o_ref.dtype)


def _short_conv(p_sc, conv_w, tm=256):
    t = p_sc.shape[0]
    tm = min(tm, t)
    hb = tm // SUBLANES
    main = lambda c: pl.BlockSpec((tm, SC_WIDTH), functools.partial(lambda i, c: (i, c), c=c))
    halo = lambda c: pl.BlockSpec((SUBLANES, SC_WIDTH),
                                  functools.partial(lambda i, c: (jnp.maximum(i * hb - 1, 0), c), c=c))
    return pl.pallas_call(
        _short_conv_kernel,
        grid=(t // tm,),
        in_specs=[main(0), main(1), main(2), main(3), halo(1), halo(2),
                  pl.BlockSpec((SC_CONV, SC_WIDTH), lambda i: (0, 0))],
        out_specs=pl.BlockSpec((tm, SC_WIDTH), lambda i: (i, 0)),
        out_shape=jax.ShapeDtypeStruct((t, SC_WIDTH), BF16),
        compiler_params=_cparams(("parallel",), 40),
        name="short_conv",
    )(p_sc, p_sc, p_sc, p_sc, p_sc, p_sc, conv_w)


_IN_GROUPS = (("nsa", _OFF_QA, _OFF_GATE), ("za", _OFF_ZA, _OFF_QB), ("gdn", _OFF_QB, _OFF_AB),
              ("zb", _OFF_ZB, _OFF_SC), ("sc", _OFF_SC, _OFF_MERGE), ("merge", _OFF_MERGE, None))


def _stage_w_in(w_in):
    depth, d, _ = w_in.shape
    w_t = jnp.transpose(w_in, (0, 2, 1)).astype(BF16)
    small_pad = LANES - 2 * GDN_HEADS - 3 * NSA_HEADS
    narrow = jnp.concatenate([w_in[:, :, _OFF_AB:_OFF_ZB], w_in[:, :, _OFF_GATE:_OFF_ZA],
                              jnp.zeros((depth, d, small_pad), F32)], axis=2)
    w_small_t = jnp.transpose(narrow, (0, 2, 1)).astype(BF16)
    return w_t, w_small_t


def _layer(x, layer, norm_w, w_t, w_small_t, nsa_qk_norm, cmp_pos, cmp_w1, cmp_w2, gdn_conv_w, gdn_a_log,
           gdn_dt_bias, gdn_norm_w, sc_conv_w, wb_nsa, wb_gdn, wb_sc, w_out, slopes):
    hn = _rmsnorm(x, norm_w)
    proj = {}
    for name, a, b in _IN_GROUPS:
        n = (w_t.shape[1] if b is None else b) - a
        proj[name] = _matmul_nt(hn, w_t, layer, a, n, F32, 1024, 1024, "in_proj_" + name)
    small = _matmul_nt(hn, w_small_t, layer, 0, LANES, F32, 1024, LANES, "in_proj_small")
    p_nsa, z_a, p_gdn, z_b, p_sc, p_mg = (proj[n] for n, _, _ in _IN_GROUPS)

    q, kv_c, k_s, v_s, k_w, v_w = _nsa_prep(p_nsa, nsa_qk_norm)
    kv_cmp = _compress(kv_c, cmp_pos, cmp_w1, cmp_w2, nsa_qk_norm[1])
    o_cmp, sel, blk_any = _cmp_attention(q, kv_cmp, slopes)
    o_slc = _sel_attention(q, k_s, v_s, sel, blk_any, slopes)
    o_win = _win_attention(q, k_w, v_w, slopes)
    o_a = _nsa_combine(small, o_cmp, o_slc, o_win, z_a)

    q_b, k_b, v_b, gb = _gdn_prep(p_gdn, small, gdn_conv_w, gdn_a_log, gdn_dt_bias)
    o_b = _gdn_fused(q_b, k_b, v_b, gb, z_b, gdn_norm_w)

    o_c = _short_conv(p_sc, sc_conv_w)

    merged = _merge(o_a, o_b, o_c, wb_nsa, wb_gdn, wb_sc, layer, p_mg)
    return _matmul_residual(merged, w_out, layer, x, 1024, 512)


def kernel(x, norm_w, w_in, nsa_qk_norm, cmp_pos, cmp_w1, cmp_w2, gdn_conv_w, gdn_a_log, gdn_dt_bias,
           gdn_norm_w, sc_conv_w, w_branch_nsa, w_branch_gdn, w_branch_sc, w_out):
    b, t, d = x.shape
    depth = norm_w.shape[0]
    heads = jnp.arange(1, NSA_HEADS + 1, dtype=F32)
    slopes = jnp.exp2(-8.0 * heads / NSA_HEADS)
    w_t, w_small_t = _stage_w_in(w_in)
    wb_nsa, wb_gdn, wb_sc, w_out16 = (w.astype(BF16) for w in (w_branch_nsa, w_branch_gdn, w_branch_sc, w_out))
    outs = []
    for bi in range(b):
        xb = x[bi]
        for l in range(depth):
            xb = _layer(xb, l, norm_w[l], w_t, w_small_t, nsa_qk_norm[l], cmp_pos[l], cmp_w1[l], cmp_w2[l],
                        gdn_conv_w[l], gdn_a_log[l], gdn_dt_bias[l], gdn_norm_w[l], sc_conv_w[l],
                        wb_nsa, wb_gdn, wb_sc, w_out16, slopes)
        outs.append(xb)
    return jnp.stack(outs, axis=0)
```

```python
import functools
import math

import jax
import jax.numpy as jnp
from jax import lax
from jax.experimental import pallas as pl
from jax.experimental.pallas import tpu as pltpu

F32 = jnp.float32
BF16 = jnp.bfloat16

HEAD_DIM = 128
NSA_HEADS = 16
NSA_KV_HEADS = 4
NSA_GROUP = NSA_HEADS // NSA_KV_HEADS
NSA_WIDTH = NSA_HEADS * HEAD_DIM
NSA_KV_WIDTH = NSA_KV_HEADS * HEAD_DIM
CMP_BLOCK = 32
CMP_STRIDE = 16
CMP_HIDDEN = 256
SEL_BLOCK = 64
SEL_TOPK = 16
WINDOW = 512
GDN_HEADS = 16
GDN_DK = 128
GDN_DV = 128
GDN_KEY_WIDTH = GDN_HEADS * GDN_DK
GDN_VAL_WIDTH = GDN_HEADS * GDN_DV
GDN_CONV = 4
GDN_CHUNK = 64
SC_WIDTH = 2048
SC_CONV = 3
N_BRANCH = 3
NORM_EPS = 1e-6
NEG_INF = -1e30
MASK_BIAS = -2e30
FORCE_SCORE = 1e6

LANES = 128
SUBLANES = 8
BF16_SUBLANES = 16
N_BLK_LANES = 128
CMP_KEY_STEP = 128
MIB = 1024 * 1024

_OFF_QA = 0
_OFF_KVC = _OFF_QA + NSA_WIDTH
_OFF_KVS = _OFF_KVC + 2 * NSA_KV_WIDTH
_OFF_KVW = _OFF_KVS + 2 * NSA_KV_WIDTH
_OFF_GATE = _OFF_KVW + 2 * NSA_KV_WIDTH
_OFF_ZA = _OFF_GATE + 3 * NSA_HEADS
_OFF_QB = _OFF_ZA + NSA_WIDTH
_OFF_AB = _OFF_QB + 2 * GDN_KEY_WIDTH + GDN_VAL_WIDTH
_OFF_BETA = _OFF_AB + GDN_HEADS
_OFF_ZB = _OFF_BETA + GDN_HEADS
_OFF_SC = _OFF_ZB + GDN_VAL_WIDTH
_OFF_MERGE = _OFF_SC + 4 * SC_WIDTH

_SM_A = 0
_SM_BETA = GDN_HEADS
_SM_GATE = 2 * GDN_HEADS


def _cparams(sem, vmem_mib):
    return pltpu.CompilerParams(dimension_semantics=sem, vmem_limit_bytes=vmem_mib * MIB)


def _sigmoid(x):
    return jax.nn.sigmoid(x)


def _silu(x):
    return x * jax.nn.sigmoid(x)


def _dot(a, b):
    return jnp.dot(a, b, preferred_element_type=F32)


def _dot_nt(a, b):
    return lax.dot_general(a, b, (((1,), (1,)), ((), ())), preferred_element_type=F32)


def _dot_tn(a, b):
    return lax.dot_general(a, b, (((0,), (0,)), ((), ())), preferred_element_type=F32)


def _head_rms(x, w):
    return x * lax.rsqrt(jnp.mean(x * x, axis=-1, keepdims=True) + NORM_EPS) * w


def _rmsnorm_kernel(x_ref, w_ref, o_ref):
    x = x_ref[...]
    y = x * lax.rsqrt(jnp.mean(x * x, axis=-1, keepdims=True) + NORM_EPS)
    o_ref[...] = (y * w_ref[...]).astype(o_ref.dtype)


def _rmsnorm(x, w, tm=256):
    t, d = x.shape
    return pl.pallas_call(
        _rmsnorm_kernel,
        grid=(t // tm,),
        in_specs=[pl.BlockSpec((tm, d), lambda i: (i, 0)), pl.BlockSpec((1, d), lambda i: (0, 0))],
        out_specs=pl.BlockSpec((tm, d), lambda i: (i, 0)),
        out_shape=jax.ShapeDtypeStruct((t, d), BF16),
        compiler_params=_cparams(("parallel",), 32),
        name="rmsnorm",
    )(x, w.reshape(1, d))


def _mm_kernel(a_ref, b_ref, o_ref):
    o_ref[...] = _dot(a_ref[...], b_ref[...]).astype(o_ref.dtype)


def _mm_nt_kernel(a_ref, b_ref, o_ref):
    o_ref[...] = _dot_nt(a_ref[...], b_ref[...]).astype(o_ref.dtype)


def _matmul_nt(a, b_t, layer, row_off, n, out_dtype, tm, tn, name):
    m, k = a.shape
    tm = min(tm, m)
    tn = min(tn, n)
    depth, rows, _ = b_t.shape
    first_row = layer * rows + row_off
    assert first_row % BF16_SUBLANES == 0 and n % tn == 0
    b_t = b_t.reshape(depth * rows, k)
    return pl.pallas_call(
        _mm_nt_kernel,
        grid=(m // tm, n // tn),
        in_specs=[pl.BlockSpec((tm, k), lambda i, j: (i, 0)),
                  pl.BlockSpec((pl.Element(tn), pl.Element(k)),
                               lambda i, j: (pl.multiple_of(first_row + j * tn, BF16_SUBLANES), 0))],
        out_specs=pl.BlockSpec((tm, tn), lambda i, j: (i, j)),
        out_shape=jax.ShapeDtypeStruct((m, n), out_dtype),
        compiler_params=_cparams(("parallel", "arbitrary"), 52),
        name=name,
    )(a, b_t)


def _mm_res_kernel(a_ref, b_ref, r_ref, o_ref):
    o_ref[...] = r_ref[...] + _dot(a_ref[...], b_ref[...])


def _matmul_residual(a, b, layer, r, tm, tn):
    m, k = a.shape
    n = b.shape[2]
    tm = min(tm, m)
    return pl.pallas_call(
        _mm_res_kernel,
        grid=(m // tm, n // tn),
        in_specs=[pl.BlockSpec((tm, k), lambda i, j: (i, 0)),
                  pl.BlockSpec((None, k, tn), lambda i, j: (layer, 0, j)),
                  pl.BlockSpec((tm, tn), lambda i, j: (i, j))],
        out_specs=pl.BlockSpec((tm, tn), lambda i, j: (i, j)),
        out_shape=jax.ShapeDtypeStruct((m, n), F32),
        compiler_params=_cparams(("parallel", "arbitrary"), 48),
        name="out_proj_residual",
    )(a, b, r)


def _merge_kernel(oa_ref, ob_ref, oc_ref, wa_ref, wb_ref, wc_ref, ga_ref, gb_ref, gc_ref, o_ref):
    half = o_ref.shape[1] // 2
    cols = [slice(0, half), slice(half, 2 * half)]
    prods = [[_dot(a_ref[...], w_ref[:, c]) for a_ref, w_ref in ((oa_ref, wa_ref), (ob_ref, wb_ref), (oc_ref, wc_ref))]
             for c in cols]
    for c, (pa, pb, pc) in zip(cols, prods):
        acc = _sigmoid(ga_ref[:, c]) * pa
        acc = acc + _sigmoid(gb_ref[:, c]) * pb
        acc = acc + _sigmoid(gc_ref[:, c]) * pc
        o_ref[:, c] = acc.astype(o_ref.dtype)


def _merge(o_a, o_b, o_c, wa, wb, wc, layer, gates, tm=512, tn=512):
    m, k = o_a.shape
    n = wa.shape[2]
    tm = min(tm, m)
    nb = n // tn
    a_spec = pl.BlockSpec((tm, k), lambda i, j: (i, 0))
    w_spec = pl.BlockSpec((None, k, tn), lambda i, j: (layer, 0, j))
    g_specs = [pl.BlockSpec((tm, tn), functools.partial(lambda i, j, br: (i, br * nb + j), br=br))
               for br in range(N_BRANCH)]
    return pl.pallas_call(
        _merge_kernel,
        grid=(m // tm, nb),
        in_specs=[a_spec, a_spec, a_spec, w_spec, w_spec, w_spec] + g_specs,
        out_specs=pl.BlockSpec((tm, tn), lambda i, j: (i, j)),
        out_shape=jax.ShapeDtypeStruct((m, n), BF16),
        compiler_params=_cparams(("parallel", "arbitrary"), 48),
        name="branch_merge",
    )(o_a, o_b, o_c, wa, wb, wc, gates, gates, gates)


def _nsa_prep_kernel(p_ref, nw_ref, q_ref, kvc_ref, ks_ref, vs_ref, kw_ref, vw_ref):
    nw = nw_ref[...]
    for h in range(NSA_HEADS):
        sl = slice(h * HEAD_DIM, (h + 1) * HEAD_DIM)
        q_ref[:, sl] = (_head_rms(p_ref[:, sl], nw[0:1]) * (HEAD_DIM ** -0.5)).astype(BF16)
    kvc_ref[...] = p_ref[:, _OFF_KVC:_OFF_KVS]
    for g in range(NSA_KV_HEADS):
        sl = slice(g * HEAD_DIM, (g + 1) * HEAD_DIM)
        ks = p_ref[:, _OFF_KVS + g * HEAD_DIM:_OFF_KVS + (g + 1) * HEAD_DIM]
        ks_ref[:, sl] = _head_rms(ks, nw[2:3]).astype(BF16)
        vs_ref[:, sl] = p_ref[:, _OFF_KVS + NSA_KV_WIDTH + g * HEAD_DIM:
                              _OFF_KVS + NSA_KV_WIDTH + (g + 1) * HEAD_DIM].astype(BF16)
        kw = p_ref[:, _OFF_KVW + g * HEAD_DIM:_OFF_KVW + (g + 1) * HEAD_DIM]
        kw_ref[:, sl] = _head_rms(kw, nw[3:4]).astype(BF16)
        vw_ref[:, sl] = p_ref[:, _OFF_KVW + NSA_KV_WIDTH + g * HEAD_DIM:
                              _OFF_KVW + NSA_KV_WIDTH + (g + 1) * HEAD_DIM].astype(BF16)


def _nsa_prep(p_nsa, qk_norm, tm=256):
    t, width = p_nsa.shape
    row = lambda i: (i, 0)
    return pl.pallas_call(
        _nsa_prep_kernel,
        grid=(t // tm,),
        in_specs=[pl.BlockSpec((tm, width), row), pl.BlockSpec((4, HEAD_DIM), lambda i: (0, 0))],
        out_specs=[pl.BlockSpec((tm, NSA_WIDTH), row)] + [pl.BlockSpec((tm, 2 * NSA_KV_WIDTH), row)]
        + [pl.BlockSpec((tm, NSA_KV_WIDTH), row)] * 4,
        out_shape=[jax.ShapeDtypeStruct((t, NSA_WIDTH), BF16), jax.ShapeDtypeStruct((t, 2 * NSA_KV_WIDTH), F32)]
        + [jax.ShapeDtypeStruct((t, NSA_KV_WIDTH), BF16)] * 4,
        compiler_params=_cparams(("parallel",), 40),
        name="nsa_prep",
    )(p_nsa, qk_norm)


def _compress_kernel(x_ref, pos_ref, w1lo_ref, w1hi_ref, w2_ref, nw_ref, o_ref, lo_acc, hi_acc, *, n_chunk):
    kv = pl.program_id(0)
    l = pl.program_id(1)

    @pl.when(l == 0)
    def _():
        lo_acc[...] = jnp.zeros(lo_acc.shape, F32)
        hi_acc[...] = jnp.zeros(hi_acc.shape, F32)

    pos_lo = pos_ref[0, pl.ds(l, 1), :]
    pos_hi = pos_ref[0, pl.ds(l + CMP_STRIDE, 1), :]
    w_lo = w1lo_ref[0, 0]
    w_hi = w1hi_ref[0, 0]
    for g in range(NSA_KV_HEADS):
        x = x_ref[:, g * HEAD_DIM:(g + 1) * HEAD_DIM]
        lo_acc[g] += _dot((x + pos_lo).astype(BF16), w_lo)
        hi_acc[g] += _dot((x + pos_hi).astype(BF16), w_hi)

    @pl.when(l == CMP_STRIDE - 1)
    def _():
        for g in range(NSA_KV_HEADS):
            hidden = _silu(lo_acc[g] + pltpu.roll(hi_acc[g], shift=n_chunk - 1, axis=0))
            out = _dot(hidden.astype(BF16), w2_ref[0])
            normed = _head_rms(out, nw_ref[...])
            o_ref[0, g] = jnp.where(kv == 0, normed, out).astype(BF16)


def _compress(kv_c, cmp_pos, cmp_w1, cmp_w2, k_norm_w):
    t = kv_c.shape[0]
    n_chunk = t // CMP_STRIDE
    x = kv_c.reshape(n_chunk, CMP_STRIDE * 2 * NSA_KV_WIDTH)
    w1 = cmp_w1.astype(BF16)
    w2 = cmp_w2.astype(BF16)
    return pl.pallas_call(
        functools.partial(_compress_kernel, n_chunk=n_chunk),
        grid=(2, CMP_STRIDE),
        in_specs=[pl.BlockSpec((n_chunk, NSA_KV_WIDTH), lambda a, l: (0, 2 * l + a)),
                  pl.BlockSpec((1, CMP_BLOCK, HEAD_DIM), lambda a, l: (a, 0, 0)),
                  pl.BlockSpec((1, 1, HEAD_DIM, CMP_HIDDEN), lambda a, l: (a, l, 0, 0)),
                  pl.BlockSpec((1, 1, HEAD_DIM, CMP_HIDDEN), lambda a, l: (a, l + CMP_STRIDE, 0, 0)),
                  pl.BlockSpec((1, CMP_HIDDEN, HEAD_DIM), lambda a, l: (a, 0, 0)),
                  pl.BlockSpec((1, HEAD_DIM), lambda a, l: (0, 0))],
        out_specs=pl.BlockSpec((1, NSA_KV_HEADS, n_chunk, HEAD_DIM), lambda a, l: (a, 0, 0, 0)),
        out_shape=jax.ShapeDtypeStruct((2, NSA_KV_HEADS, n_chunk, HEAD_DIM), BF16),
        scratch_shapes=[pltpu.VMEM((NSA_KV_HEADS, n_chunk, CMP_HIDDEN), F32),
                        pltpu.VMEM((NSA_KV_HEADS, n_chunk, CMP_HIDDEN), F32)],
        compiler_params=_cparams(("arbitrary", "arbitrary"), 40),
        name="nsa_compress",
    )(x, cmp_pos, w1, w1, w2, k_norm_w.reshape(1, HEAD_DIM))


def _cmp_attn_kernel(slopes_ref, q_ref, k_ref, v_ref, cov_ref, o_ref, sel_ref, any_ref, *, tq, n_chunk):
    g = pl.program_id(0)
    qi = pl.program_id(1)
    t0 = qi * tq
    sls = [slice(z * HEAD_DIM, (z + 1) * HEAD_DIM) for z in range(NSA_GROUP)]

    def attend(nk):
        k = k_ref[0, 0, 0:nk, :]
        v_t = v_ref[0, 0, 0:nk, :].T
        t_pos = t0 + lax.broadcasted_iota(jnp.int32, (nk, tq), 1)
        k_end = lax.broadcasted_iota(jnp.int32, (nk, tq), 0) * CMP_STRIDE + (CMP_BLOCK - 1)
        mask_bias = jnp.where(t_pos >= k_end, 0.0, MASK_BIAS)
        k_rel = (k_end - t0).astype(F32)
        psum = jnp.zeros((nk, tq), F32)
        qk_next = _dot_nt(k, q_ref[:, sls[0]])
        pending = None
        for z in range(NSA_GROUP):
            qk = qk_next
            if z + 1 < NSA_GROUP:
                qk_next = _dot_nt(k, q_ref[:, sls[z + 1]])
            s = qk + (mask_bias + slopes_ref[g * NSA_GROUP + z] * k_rel)
            m = jnp.maximum(jnp.max(s, axis=0, keepdims=True), NEG_INF)
            e = jnp.exp(s - m)
            p = e * (1.0 / jnp.maximum(jnp.sum(e, axis=0, keepdims=True), 1e-30))
            psum = psum + p
            if pending is not None:
                o_ref[:, sls[pending[0]]] = _dot(v_t, pending[1]).T
            pending = (z, p.astype(BF16))
        o_ref[:, sls[pending[0]]] = _dot(v_t, pending[1]).T
        p_hi = psum.astype(BF16)
        p_lo = (psum - p_hi.astype(F32)).astype(BF16)
        nb = min(N_BLK_LANES, nk * CMP_STRIDE // SEL_BLOCK + SUBLANES)
        cov_t = cov_ref[0:nb, 0:nk]
        select(_dot(cov_t, p_hi) + _dot(cov_t, p_lo), nb)

    def select(imp, nb):
        j = lax.broadcasted_iota(jnp.int32, (nb, tq), 0)
        cur = (t0 + lax.broadcasted_iota(jnp.int32, (nb, tq), 1)) // SEL_BLOCK
        valid = j <= cur
        forced = (j == 0) | (j == cur) | (j == cur - 1)
        val = jnp.where(valid, jnp.where(forced, -2.0, imp), -1.0)
        sel = (valid & forced).astype(F32)
        jf = j.astype(F32)
        for _ in range(SEL_TOPK - 3):
            m = jnp.max(val, axis=0, keepdims=True)
            idx = jnp.min(jnp.where(val == m, jf, float(N_BLK_LANES)), axis=0, keepdims=True)
            hit = jf == idx
            sel = jnp.where(hit & (m > -0.5), 1.0, sel)
            val = jnp.where(hit, -2.0, val)
        if nb < N_BLK_LANES:
            sel = jnp.concatenate([sel, jnp.zeros((N_BLK_LANES - nb, tq), F32)], axis=0)
        sel_q = sel.T
        sel_ref[0] = sel_q.astype(BF16)
        any_ref[0, 0] = jnp.broadcast_to(jnp.max(sel_q, axis=0, keepdims=True), (SUBLANES, N_BLK_LANES))

    visible = (t0 + tq - CMP_BLOCK) // CMP_STRIDE + 1
    n_var = n_chunk // CMP_KEY_STEP if n_chunk % CMP_KEY_STEP == 0 else 1
    if n_var <= 1:
        attend(n_chunk)
    else:
        for var in range(1, n_var + 1):
            lo, hi = (var - 1) * CMP_KEY_STEP, var * CMP_KEY_STEP
            if var == 1:
                cond = visible <= hi
            elif var < n_var:
                cond = (visible > lo) & (visible <= hi)
            else:
                cond = visible > lo
            pl.when(cond)(functools.partial(attend, hi))


def _cmp_attention(q, kv_cmp, slopes, tq=256):
    t = q.shape[0]
    n_chunk = kv_cmp.shape[2]
    n_cmp = n_chunk - CMP_BLOCK // CMP_STRIDE + 1
    n_blk = t // SEL_BLOCK
    assert n_blk <= N_BLK_LANES
    tq = min(tq, t)
    cs = jnp.arange(n_chunk)[:, None] * CMP_STRIDE
    bs = jnp.arange(N_BLK_LANES)[None, :] * SEL_BLOCK
    cover = ((cs <= bs + SEL_BLOCK - 1) & (cs + CMP_BLOCK - 1 >= bs)
             & (jnp.arange(n_chunk)[:, None] < n_cmp) & (jnp.arange(N_BLK_LANES)[None, :] < n_blk))
    cover = cover.astype(BF16).T
    return pl.pallas_call(
        functools.partial(_cmp_attn_kernel, tq=tq, n_chunk=n_chunk),
        grid=(NSA_KV_HEADS, t // tq),
        in_specs=[pl.BlockSpec(memory_space=pltpu.SMEM),
                  pl.BlockSpec((tq, NSA_GROUP * HEAD_DIM), lambda g, i: (i, g)),
                  pl.BlockSpec((1, 1, n_chunk, HEAD_DIM), lambda g, i: (0, g, 0, 0)),
                  pl.BlockSpec((1, 1, n_chunk, HEAD_DIM), lambda g, i: (1, g, 0, 0)),
                  pl.BlockSpec((N_BLK_LANES, n_chunk), lambda g, i: (0, 0))],
        out_specs=[pl.BlockSpec((tq, NSA_GROUP * HEAD_DIM), lambda g, i: (i, g)),
                   pl.BlockSpec((1, tq, N_BLK_LANES), lambda g, i: (g, i, 0)),
                   pl.BlockSpec((1, 1, SUBLANES, N_BLK_LANES), lambda g, i: (g, i, 0, 0))],
        out_shape=[jax.ShapeDtypeStruct((t, NSA_WIDTH), F32),
                   jax.ShapeDtypeStruct((NSA_KV_HEADS, t, N_BLK_LANES), BF16),
                   jax.ShapeDtypeStruct((NSA_KV_HEADS, t // tq, SUBLANES, N_BLK_LANES), F32)],
        compiler_params=_cparams(("parallel", "parallel"), 40),
        name="nsa_cmp_attn",
    )(slopes, q, kv_cmp, kv_cmp, cover)


def _sel_attn_kernel(wl_ref, slopes_ref, q_ref, k_ref, v_ref, eb_ref, sel_ref, o_ref,
                     qa_sc, m_sc, l_sc, acc_sc, *, tq, tk, max_tiles):
    g = pl.program_id(0)
    qi = pl.program_id(1)
    item = g * pl.num_programs(1) + qi
    t0 = qi * tq

    m_sc[...] = jnp.full(m_sc.shape, NEG_INF, F32)
    l_sc[...] = jnp.zeros(l_sc.shape, F32)
    acc_sc[...] = jnp.zeros(acc_sc.shape, F32)
    unselected = sel_ref[0] - 1.0
    for z in range(NSA_GROUP):
        qa_sc[z] = jnp.concatenate([q_ref[:, z * HEAD_DIM:(z + 1) * HEAD_DIM], unselected], axis=1)

    def tile(kj, causal):
        k0 = kj * tk
        rows = pl.ds(pl.multiple_of(k0, tk), tk)
        k_aug = jnp.concatenate([k_ref[rows, :], eb_ref[kj]], axis=1)
        v_t = v_ref[rows, :].T
        row = lax.broadcasted_iota(jnp.int32, (tk, tq), 0)
        k_rel = (k0 - t0 + row).astype(F32)
        if causal:
            col = lax.broadcasted_iota(jnp.int32, (tk, tq), 1)
            causal_bias = jnp.where(t0 + col >= k0 + row, 0.0, MASK_BIAS)
        qk_next = _dot_nt(k_aug, qa_sc[0])
        pending = None
        for z in range(NSA_GROUP):
            qk = qk_next
            if z + 1 < NSA_GROUP:
                qk_next = _dot_nt(k_aug, qa_sc[z + 1])
            s = qk + slopes_ref[g * NSA_GROUP + z] * k_rel
            if causal:
                s = s + causal_bias
            m_prev = m_sc[z]
            m_new = jnp.maximum(m_prev, jnp.max(s, axis=0, keepdims=True))
            alpha = jnp.exp(m_prev - m_new)
            e = jnp.exp(s - m_new)
            l_sc[z] = alpha * l_sc[z] + jnp.sum(e, axis=0, keepdims=True)
            m_sc[z] = m_new
            if pending is not None:
                pz, palpha, pe = pending
                acc_sc[pz] = palpha * acc_sc[pz] + _dot(v_t, pe)
            pending = (z, alpha, e.astype(BF16))
        pz, palpha, pe = pending
        acc_sc[pz] = palpha * acc_sc[pz] + _dot(v_t, pe)

    def visit(i, carry):
        tile(wl_ref[item, i], False)
        return carry

    lax.fori_loop(0, wl_ref[item, max_tiles], visit, 0)
    tile(t0 // tk, True)

    for z in range(NSA_GROUP):
        out_t = acc_sc[z] * (1.0 / jnp.maximum(l_sc[z], 1e-30))
        o_ref[:, z * HEAD_DIM:(z + 1) * HEAD_DIM] = out_t.T


def _sel_work_list(blk_any, t, tq, tk):
    nq, nk, bpt = t // tq, t // tk, tk // SEL_BLOCK
    n_blk = t // SEL_BLOCK
    picked = blk_any[:, :, 0, :n_blk].reshape(NSA_KV_HEADS, nq, nk, bpt).max(axis=-1) > 0.5
    diagonal = (jnp.arange(nq) * tq) // tk
    before = picked & (jnp.arange(nk)[None, :] < diagonal[:, None])[None]
    count = before.sum(axis=-1, keepdims=True).astype(jnp.int32)
    tiles = jnp.argsort(jnp.logical_not(before), axis=-1, stable=True).astype(jnp.int32)
    return jnp.concatenate([tiles, count], axis=-1).reshape(NSA_KV_HEADS * nq, nk + 1)


def _sel_attention(q, k_s, v_s, sel, blk_any, slopes, tq=256, tk=512):
    t = q.shape[0]
    tq = min(tq, t)
    tk = min(tk, t)
    assert tk % tq == 0
    nk = t // tk
    work_list = _sel_work_list(blk_any, t, tq, tk)

    blk_of_key = jnp.arange(t)[:, None] // SEL_BLOCK
    block_onehot = jnp.where(blk_of_key == jnp.arange(N_BLK_LANES)[None, :], -MASK_BIAS, 0.0)
    block_onehot = block_onehot.astype(BF16).reshape(nk, tk, N_BLK_LANES)

    smem = pl.BlockSpec(memory_space=pltpu.SMEM)
    return pl.pallas_call(
        functools.partial(_sel_attn_kernel, tq=tq, tk=tk, max_tiles=nk),
        grid=(NSA_KV_HEADS, t // tq),
        in_specs=[smem, smem,
                  pl.BlockSpec((tq, NSA_GROUP * HEAD_DIM), lambda g, i: (i, g)),
                  pl.BlockSpec((t, HEAD_DIM), lambda g, i: (0, g)),
                  pl.BlockSpec((t, HEAD_DIM), lambda g, i: (0, g)),
                  pl.BlockSpec((nk, tk, N_BLK_LANES), lambda g, i: (0, 0, 0)),
                  pl.BlockSpec((1, tq, N_BLK_LANES), lambda g, i: (g, i, 0))],
        out_specs=pl.BlockSpec((tq, NSA_GROUP * HEAD_DIM), lambda g, i: (i, g)),
        out_shape=jax.ShapeDtypeStruct((t, NSA_WIDTH), F32),
        scratch_shapes=[pltpu.VMEM((NSA_GROUP, tq, 2 * HEAD_DIM), BF16),
                        pltpu.VMEM((NSA_GROUP, 1, tq), F32), pltpu.VMEM((NSA_GROUP, 1, tq), F32),
                        pltpu.VMEM((NSA_GROUP, HEAD_DIM, tq), F32)],
        compiler_params=_cparams(("arbitrary", "arbitrary"), 40),
        name="nsa_sel_attn",
    )(work_list, slopes, q, k_s, v_s, block_onehot, sel)


def _win_attn_kernel(slopes_ref, q_ref, *refs, tq, n_tile):
    k_refs = refs[:n_tile]
    v_refs = refs[n_tile:2 * n_tile]
    o_ref = refs[2 * n_tile]
    g = pl.program_id(0)
    qi = pl.program_id(1)
    span = n_tile * tq
    k = jnp.concatenate([r[...] for r in k_refs], axis=0)
    v = jnp.concatenate([r[...] for r in v_refs], axis=0)
    t_pos = qi * tq + lax.broadcasted_iota(jnp.int32, (tq, span), 0)
    k_pos = (qi - (n_tile - 1)) * tq + lax.broadcasted_iota(jnp.int32, (tq, span), 1)
    dist = t_pos - k_pos
    mask_bias = jnp.where((dist >= 0) & (dist < WINDOW) & (k_pos >= 0), 0.0, MASK_BIAS)
    k_rel = (lax.broadcasted_iota(jnp.int32, (1, span), 1) - (n_tile - 1) * tq).astype(F32)
    sls = [slice(z * HEAD_DIM, (z + 1) * HEAD_DIM) for z in range(NSA_GROUP)]
    qk_next = _dot_nt(q_ref[:, sls[0]], k)
    pending = None
    for z in range(NSA_GROUP):
        qk = qk_next
        if z + 1 < NSA_GROUP:
            qk_next = _dot_nt(q_ref[:, sls[z + 1]], k)
        s = qk + (mask_bias + slopes_ref[g * NSA_GROUP + z] * k_rel)
        m = jnp.maximum(jnp.max(s, axis=-1, keepdims=True), NEG_INF)
        e = jnp.exp(s - m)
        inv_l = 1.0 / jnp.maximum(jnp.sum(e, axis=-1, keepdims=True), 1e-30)
        if pending is not None:
            o_ref[:, sls[pending[0]]] = _dot(pending[1], v) * pending[2]
        pending = (z, e.astype(BF16), inv_l)
    o_ref[:, sls[pending[0]]] = _dot(pending[1], v) * pending[2]


def _win_attention(q, k_w, v_w, slopes, tq=256):
    t = q.shape[0]
    tq = min(tq, t)
    n_tile = -(-WINDOW // tq) + 1

    def kv_spec(c):
        return pl.BlockSpec((tq, HEAD_DIM), lambda g, i: (jnp.maximum(i - (n_tile - 1) + c, 0), g))

    return pl.pallas_call(
        functools.partial(_win_attn_kernel, tq=tq, n_tile=n_tile),
        grid=(NSA_KV_HEADS, t // tq),
        in_specs=[pl.BlockSpec(memory_space=pltpu.SMEM),
                  pl.BlockSpec((tq, NSA_GROUP * HEAD_DIM), lambda g, i: (i, g))]
        + [kv_spec(c) for c in range(n_tile)] * 2,
        out_specs=pl.BlockSpec((tq, NSA_GROUP * HEAD_DIM), lambda g, i: (i, g)),
        out_shape=jax.ShapeDtypeStruct((t, NSA_WIDTH), F32),
        compiler_params=_cparams(("parallel", "parallel"), 40),
        name="nsa_win_attn",
    )(slopes, q, *([k_w] * n_tile), *([v_w] * n_tile))


def _nsa_combine_kernel(sm_ref, ex_ref, oc_ref, os_ref, ow_ref, z_ref, o_ref):
    logits = sm_ref[...]
    hi = logits.astype(BF16)
    lo = (logits - hi.astype(F32)).astype(BF16)
    acc = None
    for br, branch_ref in enumerate((oc_ref, os_ref, ow_ref)):
        ex = ex_ref[br]
        gate = _sigmoid(_dot(hi, ex) + _dot(lo, ex))
        term = gate * branch_ref[...]
        acc = term if acc is None else acc + term
    o_ref[...] = (acc * _silu(z_ref[...])).astype(o_ref.dtype)


def _nsa_combine(small, o_cmp, o_slc, o_win, z_a, tm=256):
    t = small.shape[0]
    tm = min(tm, t)
    lane = jnp.arange(LANES)[None, :, None]
    br = jnp.arange(N_BRANCH)[:, None, None]
    head = (jnp.arange(NSA_WIDTH) // HEAD_DIM)[None, None, :]
    expand = (lane == _SM_GATE + head * N_BRANCH + br).astype(BF16)
    row = lambda i: (i, 0)
    wide = pl.BlockSpec((tm, NSA_WIDTH), row)
    return pl.pallas_call(
        _nsa_combine_kernel,
        grid=(t // tm,),
        in_specs=[pl.BlockSpec((tm, LANES), row),
                  pl.BlockSpec((N_BRANCH, LANES, NSA_WIDTH), lambda i: (0, 0, 0)),
                  wide, wide, wide, wide],
        out_specs=wide,
        out_shape=jax.ShapeDtypeStruct((t, NSA_WIDTH), BF16),
        compiler_params=_cparams(("parallel",), 40),
        name="nsa_combine",
    )(small, expand, o_cmp, o_slc, o_win, z_a)


def _shift_rows(cur, halo, s):
    rolled = pltpu.roll(cur, shift=s, axis=0)
    halo_rolled = pltpu.roll(halo, shift=s, axis=0)
    row = lax.broadcasted_iota(jnp.int32, halo.shape, 0)
    head = jnp.where(row < s, halo_rolled, rolled[0:SUBLANES])
    return jnp.concatenate([head, rolled[SUBLANES:]], axis=0)


def _causal_conv(cur, halo, w_ref, k):
    acc = None
    for j in range(k):
        s = k - 1 - j
        term = (cur if s == 0 else _shift_rows(cur, halo, s)) * w_ref[j:j + 1, :]
        acc = term if acc is None else acc + term
    return acc


def _gdn_prep_kernel(p_ref, halo_ref, cw_ref, sm_ref, alog_ref, dtb_ref, q_ref, k_ref, v_ref, gb_ref, xx):
    i = pl.program_id(0)
    tm = p_ref.shape[0]
    xx[0:SUBLANES, :] = jnp.where(i > 0, halo_ref[...], 0.0)
    xx[SUBLANES:, :] = p_ref[...]

    def conv_silu(cols):
        acc = None
        for j in range(GDN_CONV):
            delay = GDN_CONV - 1 - j
            term = xx[SUBLANES - delay:SUBLANES - delay + tm, cols] * cw_ref[j:j + 1, cols]
            acc = term if acc is None else acc + term
        return _silu(acc)

    for h in range(GDN_HEADS):
        sl = slice(h * GDN_DK, (h + 1) * GDN_DK)
        qh = conv_silu(sl)
        q_ref[:, sl] = qh * lax.rsqrt(jnp.sum(qh * qh, axis=-1, keepdims=True) + NORM_EPS) * (GDN_DK ** -0.5)
        kh = conv_silu(slice(GDN_KEY_WIDTH + h * GDN_DK, GDN_KEY_WIDTH + (h + 1) * GDN_DK))
        k_ref[:, sl] = kh * lax.rsqrt(jnp.sum(kh * kh, axis=-1, keepdims=True) + NORM_EPS)
        v_ref[:, sl] = conv_silu(slice(2 * GDN_KEY_WIDTH + h * GDN_DV, 2 * GDN_KEY_WIDTH + (h + 1) * GDN_DV))
    sm = sm_ref[...]
    lane = lax.broadcasted_iota(jnp.int32, sm.shape, 1)
    gdecay = -jnp.exp(alog_ref[...]) * jax.nn.softplus(sm + dtb_ref[...])
    beta = _sigmoid(sm)
    gb_ref[...] = jnp.where(lane < _SM_BETA, gdecay, jnp.where(lane < _SM_GATE, beta, 0.0))


def _gdn_prep(p_gdn, small, conv_w, a_log, dt_bias, tm=256):
    t, width = p_gdn.shape
    tm = min(tm, t)
    hb = tm // SUBLANES
    row = lambda i: (i, 0)
    alog = jnp.zeros((1, LANES), F32).at[0, _SM_A:_SM_A + GDN_HEADS].set(a_log)
    dtb = jnp.zeros((1, LANES), F32).at[0, _SM_A:_SM_A + GDN_HEADS].set(dt_bias)
    const = lambda i: (0, 0)
    return pl.pallas_call(
        _gdn_prep_kernel,
        grid=(t // tm,),
        in_specs=[pl.BlockSpec((tm, width), row),
                  pl.BlockSpec((SUBLANES, width), lambda i: (jnp.maximum(i * hb - 1, 0), 0)),
                  pl.BlockSpec((GDN_CONV, width), const),
                  pl.BlockSpec((tm, LANES), row),
                  pl.BlockSpec((1, LANES), const), pl.BlockSpec((1, LANES), const)],
        out_specs=[pl.BlockSpec((tm, GDN_KEY_WIDTH), row), pl.BlockSpec((tm, GDN_KEY_WIDTH), row),
                   pl.BlockSpec((tm, GDN_VAL_WIDTH), row), pl.BlockSpec((tm, LANES), row)],
        out_shape=[jax.ShapeDtypeStruct((t, GDN_KEY_WIDTH), F32), jax.ShapeDtypeStruct((t, GDN_KEY_WIDTH), F32),
                   jax.ShapeDtypeStruct((t, GDN_VAL_WIDTH), F32), jax.ShapeDtypeStruct((t, LANES), F32)],
        scratch_shapes=[pltpu.VMEM((tm + SUBLANES, width), F32)],
        compiler_params=_cparams(("parallel",), 48),
        name="gdn_prep",
    )(p_gdn, p_gdn, conv_w, small, alog, dtb)


def _row_pad(x):
    return jnp.concatenate([x, jnp.zeros_like(x)], axis=0)


def _gdn_parallel_stages(q_ref, k_ref, v_ref, gb_ref, rows, heads, out):
    c = GDN_CHUNK
    gb = gb_ref[rows, :]
    row = lax.broadcasted_iota(jnp.int32, gb.shape, 0)
    gcum = gb
    shift = 1
    while shift < c:
        gcum = gcum + jnp.where(row >= shift, pltpu.roll(gcum, shift=shift, axis=0), 0.0)
        shift *= 2
    lane = lax.broadcasted_iota(jnp.int32, gb.shape, 1)
    ri = lax.broadcasted_iota(jnp.int32, (c, LANES), 0)
    ci = lax.broadcasted_iota(jnp.int32, (c, LANES), 1)
    tri = ri >= ci
    strict = ri > ci
    eye = ri == ci
    eyef = eye.astype(F32)
    hs = range(heads)
    sls = [slice(h * GDN_DK, (h + 1) * GDN_DK) for h in hs]

    gcol = [jnp.sum(jnp.where(lane == _SM_A + h, gcum, 0.0), axis=-1, keepdims=True) for h in hs]
    beta = [jnp.sum(jnp.where(lane == _SM_BETA + h, gb, 0.0), axis=-1, keepdims=True) for h in hs]
    decay = []
    for h in hs:
        gmat = jnp.broadcast_to(gcol[h], (c, LANES))
        grow = jnp.sum(jnp.where(eye, gmat, 0.0), axis=0, keepdims=True)
        decay.append(jnp.where(tri, jnp.exp(jnp.where(tri, gmat - grow, 0.0)), 0.0))
    glast = [gcol[h][c - 1:c, :] for h in hs]
    egc = [jnp.exp(gcol[h]) for h in hs]
    q = [q_ref[rows, sls[h]] for h in hs]
    k = [k_ref[rows, sls[h]] for h in hs]
    k16 = [k[h].astype(BF16) for h in hs]
    qk = [_dot_nt(jnp.concatenate([q[h].astype(BF16), k16[h]], axis=0), _row_pad(k16[h])) for h in hs]
    yield
    out["at16"] = [(qk[h][:c] * decay[h]).astype(BF16) for h in hs]

    pw = [-jnp.where(strict, beta[h] * qk[h][c:] * decay[h], 0.0) for h in hs]
    inv = [eyef + pw[h] for h in hs]
    pw16 = [pw[h].astype(BF16) for h in hs]
    pw = [_dot(pw16[h], _row_pad(pw16[h])) for h in hs]
    yield
    span = 2
    while span < c:
        pw16 = [pw[h].astype(BF16) for h in hs]
        if 2 * span < c:
            both = [_dot(jnp.concatenate([pw16[h], inv[h].astype(BF16)], axis=0), _row_pad(pw16[h])) for h in hs]
            pw = [both[h][:c] for h in hs]
            inv = [inv[h] + both[h][c:] for h in hs]
        else:
            inv = [inv[h] + _dot(inv[h].astype(BF16), _row_pad(pw16[h])) for h in hs]
        span *= 2
        yield

    uw = []
    for h in hs:
        rhs = jnp.concatenate([(v_ref[rows, sls[h]] * beta[h]).astype(BF16),
                               (k[h] * (beta[h] * egc[h])).astype(BF16)], axis=1)
        uw.append(_dot(inv[h].astype(BF16), _row_pad(rhs)))
    yield
    out["u"] = [uw[h][:, :GDN_DV] for h in hs]
    out["w16"] = [uw[h][:, GDN_DV:].astype(BF16) for h in hs]
    out["qe16"] = [(q[h] * egc[h]).astype(BF16) for h in hs]
    out["ke16"] = [(k[h] * jnp.exp(glast[h] - gcol[h])).astype(BF16) for h in hs]
    out["eg"] = [jnp.exp(glast[h]) for h in hs]


def _gdn_state_stages(parts, s_cur, z_ref, nw, o_ref, rows, heads):
    c = GDN_CHUNK
    hs = range(heads)
    sls = [slice(h * GDN_DK, (h + 1) * GDN_DK) for h in hs]
    ws_qs = [_dot(jnp.concatenate([parts["w16"][h], parts["qe16"][h]], axis=0), s_cur[h].astype(BF16))
             for h in hs]
    yield
    v_new16 = [(parts["u"][h] - ws_qs[h][:c]).astype(BF16) for h in hs]
    o = [ws_qs[h][c:] + _dot(parts["at16"][h], _row_pad(v_new16[h])) for h in hs]
    yield
    s_new = [s_cur[h] * parts["eg"][h] + _dot_tn(parts["ke16"][h], v_new16[h]) for h in hs]
    for h in hs:
        s_cur[h] = s_new[h]
    yield
    for h in hs:
        o_ref[rows, sls[h]] = (_head_rms(o[h], nw) * _silu(z_ref[rows, sls[h]])).astype(o_ref.dtype)


def _gdn_fused_kernel(q_ref, k_ref, v_ref, gb_ref, z_ref, nw_ref, o_ref, state, *, heads, cps):
    n = pl.program_id(0)
    c = GDN_CHUNK

    @pl.when(n == 0)
    def _():
        state[...] = jnp.zeros(state.shape, F32)

    nw = nw_ref[...]
    s_cur = [state[h] for h in range(heads)]
    prev = None
    for cc in range(cps):
        rows = slice(cc * c, (cc + 1) * c)
        parts = {}
        gens = [_gdn_parallel_stages(q_ref, k_ref, v_ref, gb_ref, rows, heads, parts)]
        if prev is not None:
            gens.insert(0, _gdn_state_stages(prev[0], s_cur, z_ref, nw, o_ref, prev[1], heads))
        while gens:
            for gen in list(gens):
                if next(gen, "done") == "done":
                    gens.remove(gen)
        prev = (parts, rows)
    for _ in _gdn_state_stages(prev[0], s_cur, z_ref, nw, o_ref, prev[1], heads):
        pass
    for h in range(heads):
        state[h] = s_cur[h]


def _gdn_fused(q, k, v, gb, z_b, norm_w, cps=4):
    t = q.shape[0]
    c = GDN_CHUNK
    n_chunk = t // c
    while n_chunk % cps:
        cps //= 2
    heads = GDN_HEADS
    wide = pl.BlockSpec((cps * c, GDN_VAL_WIDTH), lambda n: (n, 0))
    return pl.pallas_call(
        functools.partial(_gdn_fused_kernel, heads=heads, cps=cps),
        grid=(n_chunk // cps,),
        in_specs=[wide, wide, wide, pl.BlockSpec((cps * c, LANES), lambda n: (n, 0)), wide,
                  pl.BlockSpec((1, GDN_DV), lambda n: (0, 0))],
        out_specs=wide,
        out_shape=jax.ShapeDtypeStruct((t, GDN_VAL_WIDTH), BF16),
        scratch_shapes=[pltpu.VMEM((heads, GDN_DK, GDN_DV), F32)],
        compiler_params=_cparams(("arbitrary",), 48),
        name="gdn_fused",
    )(q, k, v, gb, z_b, norm_w.reshape(1, GDN_DV))


def _short_conv_kernel(bg_ref, cg_ref, x_ref, z_ref, cgh_ref, xh_ref, cw_ref, o_ref):
    i = pl.program_id(0)
    cur = cg_ref[...] * x_ref[...]
    halo = jnp.where(i > 0, cgh_ref[...] * xh_ref[...], 0.0)
    y = _causal_conv(cur, halo, cw_ref, SC_CONV)
    o_ref[...] = (bg_ref[...] * y * _silu(z_ref[...])).astype(o_ref.dtype)


def _short_conv(p_sc, conv_w, tm=256):
    t = p_sc.shape[0]
    tm = min(tm, t)
    hb = tm // SUBLANES
    main = lambda c: pl.BlockSpec((tm, SC_WIDTH), functools.partial(lambda i, c: (i, c), c=c))
    halo = lambda c: pl.BlockSpec((SUBLANES, SC_WIDTH),
                                  functools.partial(lambda i, c: (jnp.maximum(i * hb - 1, 0), c), c=c))
    return pl.pallas_call(
        _short_conv_kernel,
        grid=(t // tm,),
        in_specs=[main(0), main(1), main(2), main(3), halo(1), halo(2),
                  pl.BlockSpec((SC_CONV, SC_WIDTH), lambda i: (0, 0))],
        out_specs=pl.BlockSpec((tm, SC_WIDTH), lambda i: (i, 0)),
        out_shape=jax.ShapeDtypeStruct((t, SC_WIDTH), BF16),
        compiler_params=_cparams(("parallel",), 40),
        name="short_conv",
    )(p_sc, p_sc, p_sc, p_sc, p_sc, p_sc, conv_w)


_IN_GROUPS = (("nsa", _OFF_QA, _OFF_GATE), ("za", _OFF_ZA, _OFF_QB), ("gdn", _OFF_QB, _OFF_AB),
              ("zb", _OFF_ZB, _OFF_SC), ("sc", _OFF_SC, _OFF_MERGE), ("merge", _OFF_MERGE, None))


def _stage_w_in(w_in):
    depth, d, _ = w_in.shape
    w_t = jnp.transpose(w_in, (0, 2, 1)).astype(BF16)
    small_pad = LANES - 2 * GDN_HEADS - 3 * NSA_HEADS
    narrow = jnp.concatenate([w_in[:, :, _OFF_AB:_OFF_ZB], w_in[:, :, _OFF_GATE:_OFF_ZA],
                              jnp.zeros((depth, d, small_pad), F32)], axis=2)
    w_small_t = jnp.transpose(narrow, (0, 2, 1)).astype(BF16)
    return w_t, w_small_t


def _layer(x, layer, norm_w, w_t, w_small_t, nsa_qk_norm, cmp_pos, cmp_w1, cmp_w2, gdn_conv_w, gdn_a_log,
           gdn_dt_bias, gdn_norm_w, sc_conv_w, wb_nsa, wb_gdn, wb_sc, w_out, slopes):
    hn = _rmsnorm(x, norm_w)
    proj = {}
    for name, a, b in _IN_GROUPS:
        n = (w_t.shape[1] if b is None else b) - a
        proj[name] = _matmul_nt(hn, w_t, layer, a, n, F32, 1024, 1024, "in_proj_" + name)
    small = _matmul_nt(hn, w_small_t, layer, 0, LANES, F32, 1024, LANES, "in_proj_small")
    p_nsa, z_a, p_gdn, z_b, p_sc, p_mg = (proj[n] for n, _, _ in _IN_GROUPS)

    q, kv_c, k_s, v_s, k_w, v_w = _nsa_prep(p_nsa, nsa_qk_norm)
    kv_cmp = _compress(kv_c, cmp_pos, cmp_w1, cmp_w2, nsa_qk_norm[1])
    o_cmp, sel, blk_any = _cmp_attention(q, kv_cmp, slopes)
    o_slc = _sel_attention(q, k_s, v_s, sel, blk_any, slopes)
    o_win = _win_attention(q, k_w, v_w, slopes)
    o_a = _nsa_combine(small, o_cmp, o_slc, o_win, z_a)

    q_b, k_b, v_b, gb = _gdn_prep(p_gdn, small, gdn_conv_w, gdn_a_log, gdn_dt_bias)
    o_b = _gdn_fused(q_b, k_b, v_b, gb, z_b, gdn_norm_w)

    o_c = _short_conv(p_sc, sc_conv_w)

    merged = _merge(o_a, o_b, o_c, wb_nsa, wb_gdn, wb_sc, layer, p_mg)
    return _matmul_residual(merged, w_out, layer, x, 1024, 512)


def kernel(x, norm_w, w_in, nsa_qk_norm, cmp_pos, cmp_w1, cmp_w2, gdn_conv_w, gdn_a_log, gdn_dt_bias,
           gdn_norm_w, sc_conv_w, w_branch_nsa, w_branch_gdn, w_branch_sc, w_out):
    b, t, d = x.shape
    depth = norm_w.shape[0]
    heads = jnp.arange(1, NSA_HEADS + 1, dtype=F32)
    slopes = jnp.exp2(-8.0 * heads / NSA_HEADS)
    w_t, w_small_t = _stage_w_in(w_in)
    wb_nsa, wb_gdn, wb_sc, w_out16 = (w.astype(BF16) for w in (w_branch_nsa, w_branch_gdn, w_branch_sc, w_out))
    outs = []
    for bi in range(b):
        xb = x[bi]
        for l in range(depth):
            xb = _layer(xb, l, norm_w[l], w_t, w_small_t, nsa_qk_norm[l], cmp_pos[l], cmp_w1[l], cmp_w2[l],
                        gdn_conv_w[l], gdn_a_log[l], gdn_dt_bias[l], gdn_norm_w[l], sc_conv_w[l],
                        wb_nsa, wb_gdn, wb_sc, w_out16, slopes)
        outs.append(xb)
    return jnp.stack(outs, axis=0)
```

```python
import functools
import math

import jax
import jax.numpy as jnp
from jax import lax
from jax.experimental import pallas as pl
from jax.experimental.pallas import tpu as pltpu

F32 = jnp.float32
BF16 = jnp.bfloat16

HEAD_DIM = 128
NSA_HEADS = 16
NSA_KV_HEADS = 4
NSA_GROUP = NSA_HEADS // NSA_KV_HEADS
NSA_WIDTH = NSA_HEADS * HEAD_DIM
NSA_KV_WIDTH = NSA_KV_HEADS * HEAD_DIM
CMP_BLOCK = 32
CMP_STRIDE = 16
CMP_HIDDEN = 256
SEL_BLOCK = 64
SEL_TOPK = 16
WINDOW = 512
GDN_HEADS = 16
GDN_DK = 128
GDN_DV = 128
GDN_KEY_WIDTH = GDN_HEADS * GDN_DK
GDN_VAL_WIDTH = GDN_HEADS * GDN_DV
GDN_CONV = 4
GDN_CHUNK = 64
SC_WIDTH = 2048
SC_CONV = 3
N_BRANCH = 3
NORM_EPS = 1e-6
NEG_INF = -1e30
MASK_BIAS = -2e30
FORCE_SCORE = 1e6

LANES = 128
SUBLANES = 8
BF16_SUBLANES = 16
N_BLK_LANES = 128
CMP_KEY_STEP = 128
MIB = 1024 * 1024

_OFF_QA = 0
_OFF_KVC = _OFF_QA + NSA_WIDTH
_OFF_KVS = _OFF_KVC + 2 * NSA_KV_WIDTH
_OFF_KVW = _OFF_KVS + 2 * NSA_KV_WIDTH
_OFF_GATE = _OFF_KVW + 2 * NSA_KV_WIDTH
_OFF_ZA = _OFF_GATE + 3 * NSA_HEADS
_OFF_QB = _OFF_ZA + NSA_WIDTH
_OFF_AB = _OFF_QB + 2 * GDN_KEY_WIDTH + GDN_VAL_WIDTH
_OFF_BETA = _OFF_AB + GDN_HEADS
_OFF_ZB = _OFF_BETA + GDN_HEADS
_OFF_SC = _OFF_ZB + GDN_VAL_WIDTH
_OFF_MERGE = _OFF_SC + 4 * SC_WIDTH

_SM_A = 0
_SM_BETA = GDN_HEADS
_SM_GATE = 2 * GDN_HEADS


def _cparams(sem, vmem_mib):
    return pltpu.CompilerParams(dimension_semantics=sem, vmem_limit_bytes=vmem_mib * MIB)


def _sigmoid(x):
    return jax.nn.sigmoid(x)


def _silu(x):
    return x * jax.nn.sigmoid(x)


def _dot(a, b):
    return jnp.dot(a, b, preferred_element_type=F32)


def _dot_nt(a, b):
    return lax.dot_general(a, b, (((1,), (1,)), ((), ())), preferred_element_type=F32)


def _dot_tn(a, b):
    return lax.dot_general(a, b, (((0,), (0,)), ((), ())), preferred_element_type=F32)


def _head_rms(x, w):
    return x * lax.rsqrt(jnp.mean(x * x, axis=-1, keepdims=True) + NORM_EPS) * w


def _rmsnorm_kernel(x_ref, w_ref, o_ref):
    x = x_ref[...]
    y = x * lax.rsqrt(jnp.mean(x * x, axis=-1, keepdims=True) + NORM_EPS)
    o_ref[...] = (y * w_ref[...]).astype(o_ref.dtype)


def _rmsnorm(x, w, tm=256):
    t, d = x.shape
    return pl.pallas_call(
        _rmsnorm_kernel,
        grid=(t // tm,),
        in_specs=[pl.BlockSpec((tm, d), lambda i: (i, 0)), pl.BlockSpec((1, d), lambda i: (0, 0))],
        out_specs=pl.BlockSpec((tm, d), lambda i: (i, 0)),
        out_shape=jax.ShapeDtypeStruct((t, d), BF16),
        compiler_params=_cparams(("parallel",), 32),
        name="rmsnorm",
    )(x, w.reshape(1, d))


def _mm_kernel(a_ref, b_ref, o_ref):
    o_ref[...] = _dot(a_ref[...], b_ref[...]).astype(o_ref.dtype)


def _mm_nt_kernel(a_ref, b_ref, o_ref):
    o_ref[...] = _dot_nt(a_ref[...], b_ref[...]).astype(o_ref.dtype)


def _matmul_nt(a, b_t, layer, row_off, n, out_dtype, tm, tn, name):
    m, k = a.shape
    tm = min(tm, m)
    tn = min(tn, n)
    depth, rows, _ = b_t.shape
    first_row = layer * rows + row_off
    assert first_row % BF16_SUBLANES == 0 and n % tn == 0
    b_t = b_t.reshape(depth * rows, k)
    return pl.pallas_call(
        _mm_nt_kernel,
        grid=(m // tm, n // tn),
        in_specs=[pl.BlockSpec((tm, k), lambda i, j: (i, 0)),
                  pl.BlockSpec((pl.Element(tn), pl.Element(k)),
                               lambda i, j: (pl.multiple_of(first_row + j * tn, BF16_SUBLANES), 0))],
        out_specs=pl.BlockSpec((tm, tn), lambda i, j: (i, j)),
        out_shape=jax.ShapeDtypeStruct((m, n), out_dtype),
        compiler_params=_cparams(("parallel", "arbitrary"), 52),
        name=name,
    )(a, b_t)


def _mm_res_kernel(a_ref, b_ref, r_ref, o_ref):
    o_ref[...] = r_ref[...] + _dot(a_ref[...], b_ref[...])


def _matmul_residual(a, b, layer, r, tm, tn):
    m, k = a.shape
    n = b.shape[2]
    tm = min(tm, m)
    return pl.pallas_call(
        _mm_res_kernel,
        grid=(m // tm, n // tn),
        in_specs=[pl.BlockSpec((tm, k), lambda i, j: (i, 0)),
                  pl.BlockSpec((None, k, tn), lambda i, j: (layer, 0, j)),
                  pl.BlockSpec((tm, tn), lambda i, j: (i, j))],
        out_specs=pl.BlockSpec((tm, tn), lambda i, j: (i, j)),
        out_shape=jax.ShapeDtypeStruct((m, n), F32),
        compiler_params=_cparams(("parallel", "arbitrary"), 48),
        name="out_proj_residual",
    )(a, b, r)


def _merge_kernel(oa_ref, ob_ref, oc_ref, wa_ref, wb_ref, wc_ref, ga_ref, gb_ref, gc_ref, o_ref):
    half = o_ref.shape[1] // 2
    cols = [slice(0, half), slice(half, 2 * half)]
    prods = [[_dot(a_ref[...], w_ref[:, c]) for a_ref, w_ref in ((oa_ref, wa_ref), (ob_ref, wb_ref), (oc_ref, wc_ref))]
             for c in cols]
    for c, (pa, pb, pc) in zip(cols, prods):
        acc = _sigmoid(ga_ref[:, c]) * pa
        acc = acc + _sigmoid(gb_ref[:, c]) * pb
        acc = acc + _sigmoid(gc_ref[:, c]) * pc
        o_ref[:, c] = acc.astype(o_ref.dtype)


def _merge(o_a, o_b, o_c, wa, wb, wc, layer, gates, tm=512, tn=512):
    m, k = o_a.shape
    n = wa.shape[2]
    tm = min(tm, m)
    nb = n // tn
    a_spec = pl.BlockSpec((tm, k), lambda i, j: (i, 0))
    w_spec = pl.BlockSpec((None, k, tn), lambda i, j: (layer, 0, j))
    g_specs = [pl.BlockSpec((tm, tn), functools.partial(lambda i, j, br: (i, br * nb + j), br=br))
               for br in range(N_BRANCH)]
    return pl.pallas_call(
        _merge_kernel,
        grid=(m // tm, nb),
        in_specs=[a_spec, a_spec, a_spec, w_spec, w_spec, w_spec] + g_specs,
        out_specs=pl.BlockSpec((tm, tn), lambda i, j: (i, j)),
        out_shape=jax.ShapeDtypeStruct((m, n), BF16),
        compiler_params=_cparams(("parallel", "arbitrary"), 48),
        name="branch_merge",
    )(o_a, o_b, o_c, wa, wb, wc, gates, gates, gates)


def _nsa_prep_kernel(p_ref, nw_ref, q_ref, kvc_ref, ks_ref, vs_ref, kw_ref, vw_ref):
    nw = nw_ref[...]
    for h in range(NSA_HEADS):
        sl = slice(h * HEAD_DIM, (h + 1) * HEAD_DIM)
        q_ref[:, sl] = (_head_rms(p_ref[:, sl], nw[0:1]) * (HEAD_DIM ** -0.5)).astype(BF16)
    kvc_ref[...] = p_ref[:, _OFF_KVC:_OFF_KVS]
    for g in range(NSA_KV_HEADS):
        sl = slice(g * HEAD_DIM, (g + 1) * HEAD_DIM)
        ks = p_ref[:, _OFF_KVS + g * HEAD_DIM:_OFF_KVS + (g + 1) * HEAD_DIM]
        ks_ref[:, sl] = _head_rms(ks, nw[2:3]).astype(BF16)
        vs_ref[:, sl] = p_ref[:, _OFF_KVS + NSA_KV_WIDTH + g * HEAD_DIM:
                              _OFF_KVS + NSA_KV_WIDTH + (g + 1) * HEAD_DIM].astype(BF16)
        kw = p_ref[:, _OFF_KVW + g * HEAD_DIM:_OFF_KVW + (g + 1) * HEAD_DIM]
        kw_ref[:, sl] = _head_rms(kw, nw[3:4]).astype(BF16)
        vw_ref[:, sl] = p_ref[:, _OFF_KVW + NSA_KV_WIDTH + g * HEAD_DIM:
                              _OFF_KVW + NSA_KV_WIDTH + (g + 1) * HEAD_DIM].astype(BF16)


def _nsa_prep(p_nsa, qk_norm, tm=256):
    t, width = p_nsa.shape
    row = lambda i: (i, 0)
    return pl.pallas_call(
        _nsa_prep_kernel,
        grid=(t // tm,),
        in_specs=[pl.BlockSpec((tm, width), row), pl.BlockSpec((4, HEAD_DIM), lambda i: (0, 0))],
        out_specs=[pl.BlockSpec((tm, NSA_WIDTH), row)] + [pl.BlockSpec((tm, 2 * NSA_KV_WIDTH), row)]
        + [pl.BlockSpec((tm, NSA_KV_WIDTH), row)] * 4,
        out_shape=[jax.ShapeDtypeStruct((t, NSA_WIDTH), BF16), jax.ShapeDtypeStruct((t, 2 * NSA_KV_WIDTH), F32)]
        + [jax.ShapeDtypeStruct((t, NSA_KV_WIDTH), BF16)] * 4,
        compiler_params=_cparams(("parallel",), 40),
        name="nsa_prep",
    )(p_nsa, qk_norm)


def _compress_kernel(x_ref, pos_ref, w1lo_ref, w1hi_ref, w2_ref, nw_ref, o_ref, lo_acc, hi_acc, *, n_chunk):
    kv = pl.program_id(0)
    l = pl.program_id(1)

    @pl.when(l == 0)
    def _():
        lo_acc[...] = jnp.zeros(lo_acc.shape, F32)
        hi_acc[...] = jnp.zeros(hi_acc.shape, F32)

    pos_lo = pos_ref[0, pl.ds(l, 1), :]
    pos_hi = pos_ref[0, pl.ds(l + CMP_STRIDE, 1), :]
    w_lo = w1lo_ref[0, 0]
    w_hi = w1hi_ref[0, 0]
    for g in range(NSA_KV_HEADS):
        x = x_ref[:, g * HEAD_DIM:(g + 1) * HEAD_DIM]
        lo_acc[g] += _dot((x + pos_lo).astype(BF16), w_lo)
        hi_acc[g] += _dot((x + pos_hi).astype(BF16), w_hi)

    @pl.when(l == CMP_STRIDE - 1)
    def _():
        for g in range(NSA_KV_HEADS):
            hidden = _silu(lo_acc[g] + pltpu.roll(hi_acc[g], shift=n_chunk - 1, axis=0))
            out = _dot(hidden.astype(BF16), w2_ref[0])
            normed = _head_rms(out, nw_ref[...])
            o_ref[0, g] = jnp.where(kv == 0, normed, out).astype(BF16)


def _compress(kv_c, cmp_pos, cmp_w1, cmp_w2, k_norm_w):
    t = kv_c.shape[0]
    n_chunk = t // CMP_STRIDE
    x = kv_c.reshape(n_chunk, CMP_STRIDE * 2 * NSA_KV_WIDTH)
    w1 = cmp_w1.astype(BF16)
    w2 = cmp_w2.astype(BF16)
    return pl.pallas_call(
        functools.partial(_compress_kernel, n_chunk=n_chunk),
        grid=(2, CMP_STRIDE),
        in_specs=[pl.BlockSpec((n_chunk, NSA_KV_WIDTH), lambda a, l: (0, 2 * l + a)),
                  pl.BlockSpec((1, CMP_BLOCK, HEAD_DIM), lambda a, l: (a, 0, 0)),
                  pl.BlockSpec((1, 1, HEAD_DIM, CMP_HIDDEN), lambda a, l: (a, l, 0, 0)),
                  pl.BlockSpec((1, 1, HEAD_DIM, CMP_HIDDEN), lambda a, l: (a, l + CMP_STRIDE, 0, 0)),
                  pl.BlockSpec((1, CMP_HIDDEN, HEAD_DIM), lambda a, l: (a, 0, 0)),
                  pl.BlockSpec((1, HEAD_DIM), lambda a, l: (0, 0))],
        out_specs=pl.BlockSpec((1, NSA_KV_HEADS, n_chunk, HEAD_DIM), lambda a, l: (a, 0, 0, 0)),
        out_shape=jax.ShapeDtypeStruct((2, NSA_KV_HEADS, n_chunk, HEAD_DIM), BF16),
        scratch_shapes=[pltpu.VMEM((NSA_KV_HEADS, n_chunk, CMP_HIDDEN), F32),
                        pltpu.VMEM((NSA_KV_HEADS, n_chunk, CMP_HIDDEN), F32)],
        compiler_params=_cparams(("arbitrary", "arbitrary"), 40),
        name="nsa_compress",
    )(x, cmp_pos, w1, w1, w2, k_norm_w.reshape(1, HEAD_DIM))


def _cmp_attn_kernel(slopes_ref, q_ref, k_ref, v_ref, cov_ref, o_ref, sel_ref, any_ref, *, tq, n_chunk):
    g = pl.program_id(0)
    qi = pl.program_id(1)
    t0 = qi * tq
    sls = [slice(z * HEAD_DIM, (z + 1) * HEAD_DIM) for z in range(NSA_GROUP)]

    def attend(nk):
        k = k_ref[0, 0, 0:nk, :]
        v_t = v_ref[0, 0, 0:nk, :].T
        t_pos = t0 + lax.broadcasted_iota(jnp.int32, (nk, tq), 1)
        k_end = lax.broadcasted_iota(jnp.int32, (nk, tq), 0) * CMP_STRIDE + (CMP_BLOCK - 1)
        mask_bias = jnp.where(t_pos >= k_end, 0.0, MASK_BIAS)
        k_rel = (k_end - t0).astype(F32)
        psum = jnp.zeros((nk, tq), F32)
        qk_next = _dot_nt(k, q_ref[:, sls[0]])
        pending = None
        for z in range(NSA_GROUP):
            qk = qk_next
            if z + 1 < NSA_GROUP:
                qk_next = _dot_nt(k, q_ref[:, sls[z + 1]])
            s = qk + (mask_bias + slopes_ref[g * NSA_GROUP + z] * k_rel)
            m = jnp.maximum(jnp.max(s, axis=0, keepdims=True), NEG_INF)
            e = jnp.exp(s - m)
            p = e * (1.0 / jnp.maximum(jnp.sum(e, axis=0, keepdims=True), 1e-30))
            psum = psum + p
            if pending is not None:
                o_ref[:, sls[pending[0]]] = _dot(v_t, pending[1]).T
            pending = (z, p.astype(BF16))
        o_ref[:, sls[pending[0]]] = _dot(v_t, pending[1]).T
        p_hi = psum.astype(BF16)
        p_lo = (psum - p_hi.astype(F32)).astype(BF16)
        nb = min(N_BLK_LANES, nk * CMP_STRIDE // SEL_BLOCK + SUBLANES)
        cov_t = cov_ref[0:nb, 0:nk]
        select(_dot(cov_t, p_hi) + _dot(cov_t, p_lo), nb)

    def select(imp, nb):
        j = lax.broadcasted_iota(jnp.int32, (nb, tq), 0)
        cur = (t0 + lax.broadcasted_iota(jnp.int32, (nb, tq), 1)) // SEL_BLOCK
        valid = j <= cur
        forced = (j == 0) | (j == cur) | (j == cur - 1)
        val = jnp.where(valid, jnp.where(forced, -2.0, imp), -1.0)
        sel = (valid & forced).astype(F32)
        jf = j.astype(F32)
        for _ in range(SEL_TOPK - 3):
            m = jnp.max(val, axis=0, keepdims=True)
            idx = jnp.min(jnp.where(val == m, jf, float(N_BLK_LANES)), axis=0, keepdims=True)
            hit = jf == idx
            sel = jnp.where(hit & (m > -0.5), 1.0, sel)
            val = jnp.where(hit, -2.0, val)
        if nb < N_BLK_LANES:
            sel = jnp.concatenate([sel, jnp.zeros((N_BLK_LANES - nb, tq), F32)], axis=0)
        sel_q = sel.T
        sel_ref[0] = sel_q.astype(BF16)
        any_ref[0, 0] = jnp.broadcast_to(jnp.max(sel_q, axis=0, keepdims=True), (SUBLANES, N_BLK_LANES))

    visible = (t0 + tq - CMP_BLOCK) // CMP_STRIDE + 1
    n_var = n_chunk // CMP_KEY_STEP if n_chunk % CMP_KEY_STEP == 0 else 1
    if n_var <= 1:
        attend(n_chunk)
    else:
        for var in range(1, n_var + 1):
            lo, hi = (var - 1) * CMP_KEY_STEP, var * CMP_KEY_STEP
            if var == 1:
                cond = visible <= hi
            elif var < n_var:
                cond = (visible > lo) & (visible <= hi)
            else:
                cond = visible > lo
            pl.when(cond)(functools.partial(attend, hi))


def _cmp_attention(q, kv_cmp, slopes, tq=256):
    t = q.shape[0]
    n_chunk = kv_cmp.shape[2]
    n_cmp = n_chunk - CMP_BLOCK // CMP_STRIDE + 1
    n_blk = t // SEL_BLOCK
    assert n_blk <= N_BLK_LANES
    tq = min(tq, t)
    cs = jnp.arange(n_chunk)[:, None] * CMP_STRIDE
    bs = jnp.arange(N_BLK_LANES)[None, :] * SEL_BLOCK
    cover = ((cs <= bs + SEL_BLOCK - 1) & (cs + CMP_BLOCK - 1 >= bs)
             & (jnp.arange(n_chunk)[:, None] < n_cmp) & (jnp.arange(N_BLK_LANES)[None, :] < n_blk))
    cover = cover.astype(BF16).T
    return pl.pallas_call(
        functools.partial(_cmp_attn_kernel, tq=tq, n_chunk=n_chunk),
        grid=(NSA_KV_HEADS, t // tq),
        in_specs=[pl.BlockSpec(memory_space=pltpu.SMEM),
                  pl.BlockSpec((tq, NSA_GROUP * HEAD_DIM), lambda g, i: (i, g)),
                  pl.BlockSpec((1, 1, n_chunk, HEAD_DIM), lambda g, i: (0, g, 0, 0)),
                  pl.BlockSpec((1, 1, n_chunk, HEAD_DIM), lambda g, i: (1, g, 0, 0)),
                  pl.BlockSpec((N_BLK_LANES, n_chunk), lambda g, i: (0, 0))],
        out_specs=[pl.BlockSpec((tq, NSA_GROUP * HEAD_DIM), lambda g, i: (i, g)),
                   pl.BlockSpec((1, tq, N_BLK_LANES), lambda g, i: (g, i, 0)),
                   pl.BlockSpec((1, 1, SUBLANES, N_BLK_LANES), lambda g, i: (g, i, 0, 0))],
        out_shape=[jax.ShapeDtypeStruct((t, NSA_WIDTH), F32),
                   jax.ShapeDtypeStruct((NSA_KV_HEADS, t, N_BLK_LANES), BF16),
                   jax.ShapeDtypeStruct((NSA_KV_HEADS, t // tq, SUBLANES, N_BLK_LANES), F32)],
        compiler_params=_cparams(("parallel", "parallel"), 40),
        name="nsa_cmp_attn",
    )(slopes, q, kv_cmp, kv_cmp, cover)


def _sel_attn_kernel(wl_ref, slopes_ref, q_ref, k_ref, v_ref, eb_ref, sel_ref, o_ref,
                     qa_sc, m_sc, l_sc, acc_sc, *, tq, tk, max_tiles):
    g = pl.program_id(0)
    qi = pl.program_id(1)
    item = g * pl.num_programs(1) + qi
    t0 = qi * tq

    m_sc[...] = jnp.full(m_sc.shape, NEG_INF, F32)
    l_sc[...] = jnp.zeros(l_sc.shape, F32)
    acc_sc[...] = jnp.zeros(acc_sc.shape, F32)
    unselected = sel_ref[0] - 1.0
    for z in range(NSA_GROUP):
        qa_sc[z] = jnp.concatenate([q_ref[:, z * HEAD_DIM:(z + 1) * HEAD_DIM], unselected], axis=1)

    def tile(kj, causal, n_keys=tk):
        k0 = kj * tk
        rows = pl.ds(pl.multiple_of(k0, tk), n_keys)
        k_aug = jnp.concatenate([k_ref[rows, :], eb_ref[kj, 0:n_keys, :]], axis=1)
        v_t = v_ref[rows, :].T
        row = lax.broadcasted_iota(jnp.int32, (n_keys, tq), 0)
        k_rel = (k0 - t0 + row).astype(F32)
        if causal:
            col = lax.broadcasted_iota(jnp.int32, (n_keys, tq), 1)
            causal_bias = jnp.where(t0 + col >= k0 + row, 0.0, MASK_BIAS)
        qk_next = _dot_nt(k_aug, qa_sc[0])
        pending = None
        for z in range(NSA_GROUP):
            qk = qk_next
            if z + 1 < NSA_GROUP:
                qk_next = _dot_nt(k_aug, qa_sc[z + 1])
            s = qk + slopes_ref[g * NSA_GROUP + z] * k_rel
            if causal:
                s = s + causal_bias
            m_prev = m_sc[z]
            m_new = jnp.maximum(m_prev, jnp.max(s, axis=0, keepdims=True))
            alpha = jnp.exp(m_prev - m_new)
            e = jnp.exp(s - m_new)
            l_sc[z] = alpha * l_sc[z] + jnp.sum(e, axis=0, keepdims=True)
            m_sc[z] = m_new
            if pending is not None:
                pz, palpha, pe = pending
                acc_sc[pz] = palpha * acc_sc[pz] + _dot(v_t, pe)
            pending = (z, alpha, e.astype(BF16))
        pz, palpha, pe = pending
        acc_sc[pz] = palpha * acc_sc[pz] + _dot(v_t, pe)

    def visit(i, carry):
        tile(wl_ref[item, i], False)
        return carry

    lax.fori_loop(0, wl_ref[item, max_tiles], visit, 0)
    if tk == 2 * tq:
        in_first_half = t0 % tk == 0
        pl.when(in_first_half)(lambda: tile(t0 // tk, True, tq))
        pl.when(jnp.logical_not(in_first_half))(lambda: tile(t0 // tk, True))
    else:
        tile(t0 // tk, True)

    for z in range(NSA_GROUP):
        out_t = acc_sc[z] * (1.0 / jnp.maximum(l_sc[z], 1e-30))
        o_ref[:, z * HEAD_DIM:(z + 1) * HEAD_DIM] = out_t.T


def _sel_work_list(blk_any, t, tq, tk):
    nq, nk, bpt = t // tq, t // tk, tk // SEL_BLOCK
    n_blk = t // SEL_BLOCK
    picked = blk_any[:, :, 0, :n_blk].reshape(NSA_KV_HEADS, nq, nk, bpt).max(axis=-1) > 0.5
    diagonal = (jnp.arange(nq) * tq) // tk
    before = picked & (jnp.arange(nk)[None, :] < diagonal[:, None])[None]
    count = before.sum(axis=-1, keepdims=True).astype(jnp.int32)
    tiles = jnp.argsort(jnp.logical_not(before), axis=-1, stable=True).astype(jnp.int32)
    return jnp.concatenate([tiles, count], axis=-1).reshape(NSA_KV_HEADS * nq, nk + 1)


def _sel_attention(q, k_s, v_s, sel, blk_any, slopes, tq=256, tk=512):
    t = q.shape[0]
    tq = min(tq, t)
    tk = min(tk, t)
    assert tk % tq == 0
    nk = t // tk
    work_list = _sel_work_list(blk_any, t, tq, tk)

    blk_of_key = jnp.arange(t)[:, None] // SEL_BLOCK
    block_onehot = jnp.where(blk_of_key == jnp.arange(N_BLK_LANES)[None, :], -MASK_BIAS, 0.0)
    block_onehot = block_onehot.astype(BF16).reshape(nk, tk, N_BLK_LANES)

    smem = pl.BlockSpec(memory_space=pltpu.SMEM)
    return pl.pallas_call(
        functools.partial(_sel_attn_kernel, tq=tq, tk=tk, max_tiles=nk),
        grid=(NSA_KV_HEADS, t // tq),
        in_specs=[smem, smem,
                  pl.BlockSpec((tq, NSA_GROUP * HEAD_DIM), lambda g, i: (i, g)),
                  pl.BlockSpec((t, HEAD_DIM), lambda g, i: (0, g)),
                  pl.BlockSpec((t, HEAD_DIM), lambda g, i: (0, g)),
                  pl.BlockSpec((nk, tk, N_BLK_LANES), lambda g, i: (0, 0, 0)),
                  pl.BlockSpec((1, tq, N_BLK_LANES), lambda g, i: (g, i, 0))],
        out_specs=pl.BlockSpec((tq, NSA_GROUP * HEAD_DIM), lambda g, i: (i, g)),
        out_shape=jax.ShapeDtypeStruct((t, NSA_WIDTH), F32),
        scratch_shapes=[pltpu.VMEM((NSA_GROUP, tq, 2 * HEAD_DIM), BF16),
                        pltpu.VMEM((NSA_GROUP, 1, tq), F32), pltpu.VMEM((NSA_GROUP, 1, tq), F32),
                        pltpu.VMEM((NSA_GROUP, HEAD_DIM, tq), F32)],
        compiler_params=_cparams(("arbitrary", "arbitrary"), 40),
        name="nsa_sel_attn",
    )(work_list, slopes, q, k_s, v_s, block_onehot, sel)


def _win_attn_kernel(slopes_ref, q_ref, *refs, tq, n_tile):
    k_refs = refs[:n_tile]
    v_refs = refs[n_tile:2 * n_tile]
    o_ref = refs[2 * n_tile]
    g = pl.program_id(0)
    qi = pl.program_id(1)
    span = n_tile * tq
    k = jnp.concatenate([r[...] for r in k_refs], axis=0)
    v = jnp.concatenate([r[...] for r in v_refs], axis=0)
    t_pos = qi * tq + lax.broadcasted_iota(jnp.int32, (tq, span), 0)
    k_pos = (qi - (n_tile - 1)) * tq + lax.broadcasted_iota(jnp.int32, (tq, span), 1)
    dist = t_pos - k_pos
    mask_bias = jnp.where((dist >= 0) & (dist < WINDOW) & (k_pos >= 0), 0.0, MASK_BIAS)
    k_rel = (lax.broadcasted_iota(jnp.int32, (1, span), 1) - (n_tile - 1) * tq).astype(F32)
    sls = [slice(z * HEAD_DIM, (z + 1) * HEAD_DIM) for z in range(NSA_GROUP)]
    qk_next = _dot_nt(q_ref[:, sls[0]], k)
    pending = None
    for z in range(NSA_GROUP):
        qk = qk_next
        if z + 1 < NSA_GROUP:
            qk_next = _dot_nt(q_ref[:, sls[z + 1]], k)
        s = qk + (mask_bias + slopes_ref[g * NSA_GROUP + z] * k_rel)
        m = jnp.maximum(jnp.max(s, axis=-1, keepdims=True), NEG_INF)
        e = jnp.exp(s - m)
        inv_l = 1.0 / jnp.maximum(jnp.sum(e, axis=-1, keepdims=True), 1e-30)
        if pending is not None:
            o_ref[:, sls[pending[0]]] = _dot(pending[1], v) * pending[2]
        pending = (z, e.astype(BF16), inv_l)
    o_ref[:, sls[pending[0]]] = _dot(pending[1], v) * pending[2]


def _win_attention(q, k_w, v_w, slopes, tq=256):
    t = q.shape[0]
    tq = min(tq, t)
    n_tile = -(-WINDOW // tq) + 1

    def kv_spec(c):
        return pl.BlockSpec((tq, HEAD_DIM), lambda g, i: (jnp.maximum(i - (n_tile - 1) + c, 0), g))

    return pl.pallas_call(
        functools.partial(_win_attn_kernel, tq=tq, n_tile=n_tile),
        grid=(NSA_KV_HEADS, t // tq),
        in_specs=[pl.BlockSpec(memory_space=pltpu.SMEM),
                  pl.BlockSpec((tq, NSA_GROUP * HEAD_DIM), lambda g, i: (i, g))]
        + [kv_spec(c) for c in range(n_tile)] * 2,
        out_specs=pl.BlockSpec((tq, NSA_GROUP * HEAD_DIM), lambda g, i: (i, g)),
        out_shape=jax.ShapeDtypeStruct((t, NSA_WIDTH), F32),
        compiler_params=_cparams(("parallel", "parallel"), 40),
        name="nsa_win_attn",
    )(slopes, q, *([k_w] * n_tile), *([v_w] * n_tile))


def _nsa_combine_kernel(sm_ref, ex_ref, oc_ref, os_ref, ow_ref, z_ref, o_ref):
    logits = sm_ref[...]
    hi = logits.astype(BF16)
    lo = (logits - hi.astype(F32)).astype(BF16)
    acc = None
    for br, branch_ref in enumerate((oc_ref, os_ref, ow_ref)):
        ex = ex_ref[br]
        gate = _sigmoid(_dot(hi, ex) + _dot(lo, ex))
        term = gate * branch_ref[...]
        acc = term if acc is None else acc + term
    o_ref[...] = (acc * _silu(z_ref[...])).astype(o_ref.dtype)


def _nsa_combine(small, o_cmp, o_slc, o_win, z_a, tm=256):
    t = small.shape[0]
    tm = min(tm, t)
    lane = jnp.arange(LANES)[None, :, None]
    br = jnp.arange(N_BRANCH)[:, None, None]
    head = (jnp.arange(NSA_WIDTH) // HEAD_DIM)[None, None, :]
    expand = (lane == _SM_GATE + head * N_BRANCH + br).astype(BF16)
    row = lambda i: (i, 0)
    wide = pl.BlockSpec((tm, NSA_WIDTH), row)
    return pl.pallas_call(
        _nsa_combine_kernel,
        grid=(t // tm,),
        in_specs=[pl.BlockSpec((tm, LANES), row),
                  pl.BlockSpec((N_BRANCH, LANES, NSA_WIDTH), lambda i: (0, 0, 0)),
                  wide, wide, wide, wide],
        out_specs=wide,
        out_shape=jax.ShapeDtypeStruct((t, NSA_WIDTH), BF16),
        compiler_params=_cparams(("parallel",), 40),
        name="nsa_combine",
    )(small, expand, o_cmp, o_slc, o_win, z_a)


def _shift_rows(cur, halo, s):
    rolled = pltpu.roll(cur, shift=s, axis=0)
    halo_rolled = pltpu.roll(halo, shift=s, axis=0)
    row = lax.broadcasted_iota(jnp.int32, halo.shape, 0)
    head = jnp.where(row < s, halo_rolled, rolled[0:SUBLANES])
    return jnp.concatenate([head, rolled[SUBLANES:]], axis=0)


def _causal_conv(cur, halo, w_ref, k):
    acc = None
    for j in range(k):
        s = k - 1 - j
        term = (cur if s == 0 else _shift_rows(cur, halo, s)) * w_ref[j:j + 1, :]
        acc = term if acc is None else acc + term
    return acc


def _gdn_prep_kernel(p_ref, halo_ref, cw_ref, sm_ref, alog_ref, dtb_ref, q_ref, k_ref, v_ref, gb_ref, xx):
    i = pl.program_id(0)
    tm = p_ref.shape[0]
    xx[0:SUBLANES, :] = jnp.where(i > 0, halo_ref[...], 0.0)
    xx[SUBLANES:, :] = p_ref[...]

    def conv_silu(cols):
        acc = None
        for j in range(GDN_CONV):
            delay = GDN_CONV - 1 - j
            term = xx[SUBLANES - delay:SUBLANES - delay + tm, cols] * cw_ref[j:j + 1, cols]
            acc = term if acc is None else acc + term
        return _silu(acc)

    for h in range(GDN_HEADS):
        sl = slice(h * GDN_DK, (h + 1) * GDN_DK)
        qh = conv_silu(sl)
        q_ref[:, sl] = qh * lax.rsqrt(jnp.sum(qh * qh, axis=-1, keepdims=True) + NORM_EPS) * (GDN_DK ** -0.5)
        kh = conv_silu(slice(GDN_KEY_WIDTH + h * GDN_DK, GDN_KEY_WIDTH + (h + 1) * GDN_DK))
        k_ref[:, sl] = kh * lax.rsqrt(jnp.sum(kh * kh, axis=-1, keepdims=True) + NORM_EPS)
        v_ref[:, sl] = conv_silu(slice(2 * GDN_KEY_WIDTH + h * GDN_DV, 2 * GDN_KEY_WIDTH + (h + 1) * GDN_DV))
    sm = sm_ref[...]
    lane = lax.broadcasted_iota(jnp.int32, sm.shape, 1)
    gdecay = -jnp.exp(alog_ref[...]) * jax.nn.softplus(sm + dtb_ref[...])
    beta = _sigmoid(sm)
    gb_ref[...] = jnp.where(lane < _SM_BETA, gdecay, jnp.where(lane < _SM_GATE, beta, 0.0))


def _gdn_prep(p_gdn, small, conv_w, a_log, dt_bias, tm=256):
    t, width = p_gdn.shape
    tm = min(tm, t)
    hb = tm // SUBLANES
    row = lambda i: (i, 0)
    alog = jnp.zeros((1, LANES), F32).at[0, _SM_A:_SM_A + GDN_HEADS].set(a_log)
    dtb = jnp.zeros((1, LANES), F32).at[0, _SM_A:_SM_A + GDN_HEADS].set(dt_bias)
    const = lambda i: (0, 0)
    return pl.pallas_call(
        _gdn_prep_kernel,
        grid=(t // tm,),
        in_specs=[pl.BlockSpec((tm, width), row),
                  pl.BlockSpec((SUBLANES, width), lambda i: (jnp.maximum(i * hb - 1, 0), 0)),
                  pl.BlockSpec((GDN_CONV, width), const),
                  pl.BlockSpec((tm, LANES), row),
                  pl.BlockSpec((1, LANES), const), pl.BlockSpec((1, LANES), const)],
        out_specs=[pl.BlockSpec((tm, GDN_KEY_WIDTH), row), pl.BlockSpec((tm, GDN_KEY_WIDTH), row),
                   pl.BlockSpec((tm, GDN_VAL_WIDTH), row), pl.BlockSpec((tm, LANES), row)],
        out_shape=[jax.ShapeDtypeStruct((t, GDN_KEY_WIDTH), F32), jax.ShapeDtypeStruct((t, GDN_KEY_WIDTH), F32),
                   jax.ShapeDtypeStruct((t, GDN_VAL_WIDTH), F32), jax.ShapeDtypeStruct((t, LANES), F32)],
        scratch_shapes=[pltpu.VMEM((tm + SUBLANES, width), F32)],
        compiler_params=_cparams(("parallel",), 48),
        name="gdn_prep",
    )(p_gdn, p_gdn, conv_w, small, alog, dtb)


def _row_pad(x):
    return jnp.concatenate([x, jnp.zeros_like(x)], axis=0)


def _gdn_parallel_stages(q_ref, k_ref, v_ref, gb_ref, rows, heads, out):
    c = GDN_CHUNK
    gb = gb_ref[rows, :]
    row = lax.broadcasted_iota(jnp.int32, gb.shape, 0)
    gcum = gb
    shift = 1
    while shift < c:
        gcum = gcum + jnp.where(row >= shift, pltpu.roll(gcum, shift=shift, axis=0), 0.0)
        shift *= 2
    lane = lax.broadcasted_iota(jnp.int32, gb.shape, 1)
    ri = lax.broadcasted_iota(jnp.int32, (c, LANES), 0)
    ci = lax.broadcasted_iota(jnp.int32, (c, LANES), 1)
    tri = ri >= ci
    strict = ri > ci
    eye = ri == ci
    eyef = eye.astype(F32)
    hs = range(heads)
    sls = [slice(h * GDN_DK, (h + 1) * GDN_DK) for h in hs]

    gcol = [jnp.sum(jnp.where(lane == _SM_A + h, gcum, 0.0), axis=-1, keepdims=True) for h in hs]
    beta = [jnp.sum(jnp.where(lane == _SM_BETA + h, gb, 0.0), axis=-1, keepdims=True) for h in hs]
    decay = []
    for h in hs:
        gmat = jnp.broadcast_to(gcol[h], (c, LANES))
        grow = jnp.sum(jnp.where(eye, gmat, 0.0), axis=0, keepdims=True)
        decay.append(jnp.where(tri, jnp.exp(jnp.where(tri, gmat - grow, 0.0)), 0.0))
    glast = [gcol[h][c - 1:c, :] for h in hs]
    egc = [jnp.exp(gcol[h]) for h in hs]
    q = [q_ref[rows, sls[h]] for h in hs]
    k = [k_ref[rows, sls[h]] for h in hs]
    k16 = [k[h].astype(BF16) for h in hs]
    qk = [_dot_nt(jnp.concatenate([q[h].astype(BF16), k16[h]], axis=0), _row_pad(k16[h])) for h in hs]
    yield
    out["at16"] = [(qk[h][:c] * decay[h]).astype(BF16) for h in hs]

    pw = [-jnp.where(strict, beta[h] * qk[h][c:] * decay[h], 0.0) for h in hs]
    inv = [eyef + pw[h] for h in hs]
    pw16 = [pw[h].astype(BF16) for h in hs]
    pw = [_dot(pw16[h], _row_pad(pw16[h])) for h in hs]
    yield
    span = 2
    while span < c:
        pw16 = [pw[h].astype(BF16) for h in hs]
        if 2 * span < c:
            both = [_dot(jnp.concatenate([pw16[h], inv[h].astype(BF16)], axis=0), _row_pad(pw16[h])) for h in hs]
            pw = [both[h][:c] for h in hs]
            inv = [inv[h] + both[h][c:] for h in hs]
        else:
            inv = [inv[h] + _dot(inv[h].astype(BF16), _row_pad(pw16[h])) for h in hs]
        span *= 2
        yield

    uw = []
    for h in hs:
        rhs = jnp.concatenate([(v_ref[rows, sls[h]] * beta[h]).astype(BF16),
                               (k[h] * (beta[h] * egc[h])).astype(BF16)], axis=1)
        uw.append(_dot(inv[h].astype(BF16), _row_pad(rhs)))
    yield
    out["u"] = [uw[h][:, :GDN_DV] for h in hs]
    out["w16"] = [uw[h][:, GDN_DV:].astype(BF16) for h in hs]
    out["qe16"] = [(q[h] * egc[h]).astype(BF16) for h in hs]
    out["ke16"] = [(k[h] * jnp.exp(glast[h] - gcol[h])).astype(BF16) for h in hs]
    out["eg"] = [jnp.exp(glast[h]) for h in hs]


def _gdn_state_stages(parts, s_cur, z_ref, nw, o_ref, rows, heads):
    c = GDN_CHUNK
    hs = range(heads)
    sls = [slice(h * GDN_DK, (h + 1) * GDN_DK) for h in hs]
    ws_qs = [_dot(jnp.concatenate([parts["w16"][h], parts["qe16"][h]], axis=0), s_cur[h].astype(BF16))
             for h in hs]
    yield
    v_new16 = [(parts["u"][h] - ws_qs[h][:c]).astype(BF16) for h in hs]
    o = [ws_qs[h][c:] + _dot(parts["at16"][h], _row_pad(v_new16[h])) for h in hs]
    yield
    s_new = [s_cur[h] * parts["eg"][h] + _dot_tn(parts["ke16"][h], v_new16[h]) for h in hs]
    for h in hs:
        s_cur[h] = s_new[h]
    yield
    for h in hs:
        o_ref[rows, sls[h]] = (_head_rms(o[h], nw) * _silu(z_ref[rows, sls[h]])).astype(o_ref.dtype)


def _gdn_fused_kernel(q_ref, k_ref, v_ref, gb_ref, z_ref, nw_ref, o_ref, state, *, heads, cps):
    n = pl.program_id(0)
    c = GDN_CHUNK

    @pl.when(n == 0)
    def _():
        state[...] = jnp.zeros(state.shape, F32)

    nw = nw_ref[...]
    s_cur = [state[h] for h in range(heads)]
    prev = None
    for cc in range(cps):
        rows = slice(cc * c, (cc + 1) * c)
        parts = {}
        gens = [_gdn_parallel_stages(q_ref, k_ref, v_ref, gb_ref, rows, heads, parts)]
        if prev is not None:
            gens.insert(0, _gdn_state_stages(prev[0], s_cur, z_ref, nw, o_ref, prev[1], heads))
        while gens:
            for gen in list(gens):
                if next(gen, "done") == "done":
                    gens.remove(gen)
        prev = (parts, rows)
    for _ in _gdn_state_stages(prev[0], s_cur, z_ref, nw, o_ref, prev[1], heads):
        pass
    for h in range(heads):
        state[h] = s_cur[h]


def _gdn_fused(q, k, v, gb, z_b, norm_w, cps=4):
    t = q.shape[0]
    c = GDN_CHUNK
    n_chunk = t // c
    while n_chunk % cps:
        cps //= 2
    heads = GDN_HEADS
    wide = pl.BlockSpec((cps * c, GDN_VAL_WIDTH), lambda n: (n, 0))
    return pl.pallas_call(
        functools.partial(_gdn_fused_kernel, heads=heads, cps=cps),
        grid=(n_chunk // cps,),
        in_specs=[wide, wide, wide, pl.BlockSpec((cps * c, LANES), lambda n: (n, 0)), wide,
                  pl.BlockSpec((1, GDN_DV), lambda n: (0, 0))],
        out_specs=wide,
        out_shape=jax.ShapeDtypeStruct((t, GDN_VAL_WIDTH), BF16),
        scratch_shapes=[pltpu.VMEM((heads, GDN_DK, GDN_DV), F32)],
        compiler_params=_cparams(("arbitrary",), 48),
        name="gdn_fused",
    )(q, k, v, gb, z_b, norm_w.reshape(1, GDN_DV))


def _short_conv_kernel(bg_ref, cg_ref, x_ref, z_ref, cgh_ref, xh_ref, cw_ref, o_ref):
    i = pl.program_id(0)
    cur = cg_ref[...] * x_ref[...]
    halo = jnp.where(i > 0, cgh_ref[...] * xh_ref[...], 0.0)
    y = _causal_conv(cur, halo, cw_ref, SC_CONV)
    o_ref[...] = (bg_ref[...] * y * _silu(z_ref[...])).astype(o_ref.dtype)


def _short_conv(p_sc, conv_w, tm=256):
    t = p_sc.shape[0]
    tm = min(tm, t)
    hb = tm // SUBLANES
    main = lambda c: pl.BlockSpec((tm, SC_WIDTH), functools.partial(lambda i, c: (i, c), c=c))
    halo = lambda c: pl.BlockSpec((SUBLANES, SC_WIDTH),
                                  functools.partial(lambda i, c: (jnp.maximum(i * hb - 1, 0), c), c=c))
    return pl.pallas_call(
        _short_conv_kernel,
        grid=(t // tm,),
        in_specs=[main(0), main(1), main(2), main(3), halo(1), halo(2),
                  pl.BlockSpec((SC_CONV, SC_WIDTH), lambda i: (0, 0))],
        out_specs=pl.BlockSpec((tm, SC_WIDTH), lambda i: (i, 0)),
        out_shape=jax.ShapeDtypeStruct((t, SC_WIDTH), BF16),
        compiler_params=_cparams(("parallel",), 40),
        name="short_conv",
    )(p_sc, p_sc, p_sc, p_sc, p_sc, p_sc, conv_w)


_IN_GROUPS = (("nsa", _OFF_QA, _OFF_GATE), ("za", _OFF_ZA, _OFF_QB), ("gdn", _OFF_QB, _OFF_AB),
              ("zb", _OFF_ZB, _OFF_SC), ("sc", _OFF_SC, _OFF_MERGE), ("merge", _OFF_MERGE, None))


def _stage_w_in(w_in):
    depth, d, _ = w_in.shape
    w_t = jnp.transpose(w_in, (0, 2, 1)).astype(BF16)
    small_pad = LANES - 2 * GDN_HEADS - 3 * NSA_HEADS
    narrow = jnp.concatenate([w_in[:, :, _OFF_AB:_OFF_ZB], w_in[:, :, _OFF_GATE:_OFF_ZA],
                              jnp.zeros((depth, d, small_pad), F32)], axis=2)
    w_small_t = jnp.transpose(narrow, (0, 2, 1)).astype(BF16)
    return w_t, w_small_t


def _layer(x, layer, norm_w, w_t, w_small_t, nsa_qk_norm, cmp_pos, cmp_w1, cmp_w2, gdn_conv_w, gdn_a_log,
           gdn_dt_bias, gdn_norm_w, sc_conv_w, wb_nsa, wb_gdn, wb_sc, w_out, slopes):
    hn = _rmsnorm(x, norm_w)
    proj = {}
    for name, a, b in _IN_GROUPS:
        n = (w_t.shape[1] if b is None else b) - a
        proj[name] = _matmul_nt(hn, w_t, layer, a, n, F32, 1024, 1024, "in_proj_" + name)
    small = _matmul_nt(hn, w_small_t, layer, 0, LANES, F32, 1024, LANES, "in_proj_small")
    p_nsa, z_a, p_gdn, z_b, p_sc, p_mg = (proj[n] for n, _, _ in _IN_GROUPS)

    q, kv_c, k_s, v_s, k_w, v_w = _nsa_prep(p_nsa, nsa_qk_norm)
    kv_cmp = _compress(kv_c, cmp_pos, cmp_w1, cmp_w2, nsa_qk_norm[1])
    o_cmp, sel, blk_any = _cmp_attention(q, kv_cmp, slopes)
    o_slc = _sel_attention(q, k_s, v_s, sel, blk_any, slopes)
    o_win = _win_attention(q, k_w, v_w, slopes)
    o_a = _nsa_combine(small, o_cmp, o_slc, o_win, z_a)

    q_b, k_b, v_b, gb = _gdn_prep(p_gdn, small, gdn_conv_w, gdn_a_log, gdn_dt_bias)
    o_b = _gdn_fused(q_b, k_b, v_b, gb, z_b, gdn_norm_w)

    o_c = _short_conv(p_sc, sc_conv_w)

    merged = _merge(o_a, o_b, o_c, wb_nsa, wb_gdn, wb_sc, layer, p_mg)
    return _matmul_residual(merged, w_out, layer, x, 1024, 512)


def kernel(x, norm_w, w_in, nsa_qk_norm, cmp_pos, cmp_w1, cmp_w2, gdn_conv_w, gdn_a_log, gdn_dt_bias,
           gdn_norm_w, sc_conv_w, w_branch_nsa, w_branch_gdn, w_branch_sc, w_out):
    b, t, d = x.shape
    depth = norm_w.shape[0]
    heads = jnp.arange(1, NSA_HEADS + 1, dtype=F32)
    slopes = jnp.exp2(-8.0 * heads / NSA_HEADS)
    w_t, w_small_t = _stage_w_in(w_in)
    wb_nsa, wb_gdn, wb_sc, w_out16 = (w.astype(BF16) for w in (w_branch_nsa, w_branch_gdn, w_branch_sc, w_out))
    outs = []
    for bi in range(b):
        xb = x[bi]
        for l in range(depth):
            xb = _layer(xb, l, norm_w[l], w_t, w_small_t, nsa_qk_norm[l], cmp_pos[l], cmp_w1[l], cmp_w2[l],
                        gdn_conv_w[l], gdn_a_log[l], gdn_dt_bias[l], gdn_norm_w[l], sc_conv_w[l],
                        wb_nsa, wb_gdn, wb_sc, w_out16, slopes)
        outs.append(xb)
    return jnp.stack(outs, axis=0)
```

```python
import functools
import math

import jax
import jax.numpy as jnp
from jax import lax
from jax.experimental import pallas as pl
from jax.experimental.pallas import tpu as pltpu

F32 = jnp.float32
BF16 = jnp.bfloat16

HEAD_DIM = 128
NSA_HEADS = 16
NSA_KV_HEADS = 4
NSA_GROUP = NSA_HEADS // NSA_KV_HEADS
NSA_WIDTH = NSA_HEADS * HEAD_DIM
NSA_KV_WIDTH = NSA_KV_HEADS * HEAD_DIM
CMP_BLOCK = 32
CMP_STRIDE = 16
CMP_HIDDEN = 256
SEL_BLOCK = 64
SEL_TOPK = 16
WINDOW = 512
GDN_HEADS = 16
GDN_DK = 128
GDN_DV = 128
GDN_KEY_WIDTH = GDN_HEADS * GDN_DK
GDN_VAL_WIDTH = GDN_HEADS * GDN_DV
GDN_CONV = 4
GDN_CHUNK = 64
SC_WIDTH = 2048
SC_CONV = 3
N_BRANCH = 3
NORM_EPS = 1e-6
NEG_INF = -1e30
MASK_BIAS = -2e30
FORCE_SCORE = 1e6

LANES = 128
SUBLANES = 8
BF16_SUBLANES = 16
N_BLK_LANES = 128
CMP_KEY_STEP = 128
MIB = 1024 * 1024

_OFF_QA = 0
_OFF_KVC = _OFF_QA + NSA_WIDTH
_OFF_KVS = _OFF_KVC + 2 * NSA_KV_WIDTH
_OFF_KVW = _OFF_KVS + 2 * NSA_KV_WIDTH
_OFF_GATE = _OFF_KVW + 2 * NSA_KV_WIDTH
_OFF_ZA = _OFF_GATE + 3 * NSA_HEADS
_OFF_QB = _OFF_ZA + NSA_WIDTH
_OFF_AB = _OFF_QB + 2 * GDN_KEY_WIDTH + GDN_VAL_WIDTH
_OFF_BETA = _OFF_AB + GDN_HEADS
_OFF_ZB = _OFF_BETA + GDN_HEADS
_OFF_SC = _OFF_ZB + GDN_VAL_WIDTH
_OFF_MERGE = _OFF_SC + 4 * SC_WIDTH

_SM_A = 0
_SM_BETA = GDN_HEADS
_SM_GATE = 2 * GDN_HEADS


def _cparams(sem, vmem_mib):
    return pltpu.CompilerParams(dimension_semantics=sem, vmem_limit_bytes=vmem_mib * MIB)


def _sigmoid(x):
    return jax.nn.sigmoid(x)


def _silu(x):
    return x * jax.nn.sigmoid(x)


def _dot(a, b):
    return jnp.dot(a, b, preferred_element_type=F32)


def _dot_nt(a, b):
    return lax.dot_general(a, b, (((1,), (1,)), ((), ())), preferred_element_type=F32)


def _dot_tn(a, b):
    return lax.dot_general(a, b, (((0,), (0,)), ((), ())), preferred_element_type=F32)


def _head_rms(x, w):
    return x * lax.rsqrt(jnp.mean(x * x, axis=-1, keepdims=True) + NORM_EPS) * w


def _rmsnorm_kernel(x_ref, w_ref, o_ref):
    x = x_ref[...]
    y = x * lax.rsqrt(jnp.mean(x * x, axis=-1, keepdims=True) + NORM_EPS)
    o_ref[...] = (y * w_ref[...]).astype(o_ref.dtype)


def _rmsnorm(x, w, tm=256):
    t, d = x.shape
    return pl.pallas_call(
        _rmsnorm_kernel,
        grid=(t // tm,),
        in_specs=[pl.BlockSpec((tm, d), lambda i: (i, 0)), pl.BlockSpec((1, d), lambda i: (0, 0))],
        out_specs=pl.BlockSpec((tm, d), lambda i: (i, 0)),
        out_shape=jax.ShapeDtypeStruct((t, d), BF16),
        compiler_params=_cparams(("parallel",), 32),
        name="rmsnorm",
    )(x, w.reshape(1, d))


def _mm_kernel(a_ref, b_ref, o_ref):
    o_ref[...] = _dot(a_ref[...], b_ref[...]).astype(o_ref.dtype)


def _mm_nt_kernel(a_ref, b_ref, o_ref):
    o_ref[...] = _dot_nt(a_ref[...], b_ref[...]).astype(o_ref.dtype)


def _matmul_nt(a, b_t, layer, row_off, n, out_dtype, tm, tn, name):
    m, k = a.shape
    tm = min(tm, m)
    tn = min(tn, n)
    depth, rows, _ = b_t.shape
    first_row = layer * rows + row_off
    assert first_row % BF16_SUBLANES == 0 and n % tn == 0
    b_t = b_t.reshape(depth * rows, k)
    return pl.pallas_call(
        _mm_nt_kernel,
        grid=(m // tm, n // tn),
        in_specs=[pl.BlockSpec((tm, k), lambda i, j: (i, 0)),
                  pl.BlockSpec((pl.Element(tn), pl.Element(k)),
                               lambda i, j: (pl.multiple_of(first_row + j * tn, BF16_SUBLANES), 0))],
        out_specs=pl.BlockSpec((tm, tn), lambda i, j: (i, j)),
        out_shape=jax.ShapeDtypeStruct((m, n), out_dtype),
        compiler_params=_cparams(("parallel", "arbitrary"), 52),
        name=name,
    )(a, b_t)


def _mm_res_kernel(a_ref, b_ref, r_ref, o_ref):
    o_ref[...] = r_ref[...] + _dot(a_ref[...], b_ref[...])


def _matmul_residual(a, b, layer, r, tm, tn):
    m, k = a.shape
    n = b.shape[2]
    tm = min(tm, m)
    return pl.pallas_call(
        _mm_res_kernel,
        grid=(m // tm, n // tn),
        in_specs=[pl.BlockSpec((tm, k), lambda i, j: (i, 0)),
                  pl.BlockSpec((None, k, tn), lambda i, j: (layer, 0, j)),
                  pl.BlockSpec((tm, tn), lambda i, j: (i, j))],
        out_specs=pl.BlockSpec((tm, tn), lambda i, j: (i, j)),
        out_shape=jax.ShapeDtypeStruct((m, n), F32),
        compiler_params=_cparams(("parallel", "arbitrary"), 48),
        name="out_proj_residual",
    )(a, b, r)


def _merge_kernel(oa_ref, ob_ref, oc_ref, wa_ref, wb_ref, wc_ref, ga_ref, gb_ref, gc_ref, o_ref):
    half = o_ref.shape[1] // 2
    cols = [slice(0, half), slice(half, 2 * half)]
    prods = [[_dot(a_ref[...], w_ref[:, c]) for a_ref, w_ref in ((oa_ref, wa_ref), (ob_ref, wb_ref), (oc_ref, wc_ref))]
             for c in cols]
    for c, (pa, pb, pc) in zip(cols, prods):
        acc = _sigmoid(ga_ref[:, c]) * pa
        acc = acc + _sigmoid(gb_ref[:, c]) * pb
        acc = acc + _sigmoid(gc_ref[:, c]) * pc
        o_ref[:, c] = acc.astype(o_ref.dtype)


def _merge(o_a, o_b, o_c, wa, wb, wc, layer, gates, tm=512, tn=512):
    m, k = o_a.shape
    n = wa.shape[2]
    tm = min(tm, m)
    nb = n // tn
    a_spec = pl.BlockSpec((tm, k), lambda i, j: (i, 0))
    w_spec = pl.BlockSpec((None, k, tn), lambda i, j: (layer, 0, j))
    g_specs = [pl.BlockSpec((tm, tn), functools.partial(lambda i, j, br: (i, br * nb + j), br=br))
               for br in range(N_BRANCH)]
    return pl.pallas_call(
        _merge_kernel,
        grid=(m // tm, nb),
        in_specs=[a_spec, a_spec, a_spec, w_spec, w_spec, w_spec] + g_specs,
        out_specs=pl.BlockSpec((tm, tn), lambda i, j: (i, j)),
        out_shape=jax.ShapeDtypeStruct((m, n), BF16),
        compiler_params=_cparams(("parallel", "arbitrary"), 48),
        name="branch_merge",
    )(o_a, o_b, o_c, wa, wb, wc, gates, gates, gates)


def _nsa_prep_kernel(p_ref, nw_ref, q_ref, kvc_ref, ks_ref, vs_ref, kw_ref, vw_ref):
    nw = nw_ref[...]
    for h in range(NSA_HEADS):
        sl = slice(h * HEAD_DIM, (h + 1) * HEAD_DIM)
        q_ref[:, sl] = (_head_rms(p_ref[:, sl], nw[0:1]) * (HEAD_DIM ** -0.5)).astype(BF16)
    kvc_ref[...] = p_ref[:, _OFF_KVC:_OFF_KVS]
    for g in range(NSA_KV_HEADS):
        sl = slice(g * HEAD_DIM, (g + 1) * HEAD_DIM)
        ks = p_ref[:, _OFF_KVS + g * HEAD_DIM:_OFF_KVS + (g + 1) * HEAD_DIM]
        ks_ref[:, sl] = _head_rms(ks, nw[2:3]).astype(BF16)
        vs_ref[:, sl] = p_ref[:, _OFF_KVS + NSA_KV_WIDTH + g * HEAD_DIM:
                              _OFF_KVS + NSA_KV_WIDTH + (g + 1) * HEAD_DIM].astype(BF16)
        kw = p_ref[:, _OFF_KVW + g * HEAD_DIM:_OFF_KVW + (g + 1) * HEAD_DIM]
        kw_ref[:, sl] = _head_rms(kw, nw[3:4]).astype(BF16)
        vw_ref[:, sl] = p_ref[:, _OFF_KVW + NSA_KV_WIDTH + g * HEAD_DIM:
                              _OFF_KVW + NSA_KV_WIDTH + (g + 1) * HEAD_DIM].astype(BF16)


def _nsa_prep(p_nsa, qk_norm, tm=256):
    t, width = p_nsa.shape
    row = lambda i: (i, 0)
    return pl.pallas_call(
        _nsa_prep_kernel,
        grid=(t // tm,),
        in_specs=[pl.BlockSpec((tm, width), row), pl.BlockSpec((4, HEAD_DIM), lambda i: (0, 0))],
        out_specs=[pl.BlockSpec((tm, NSA_WIDTH), row)] + [pl.BlockSpec((tm, 2 * NSA_KV_WIDTH), row)]
        + [pl.BlockSpec((tm, NSA_KV_WIDTH), row)] * 4,
        out_shape=[jax.ShapeDtypeStruct((t, NSA_WIDTH), BF16), jax.ShapeDtypeStruct((t, 2 * NSA_KV_WIDTH), F32)]
        + [jax.ShapeDtypeStruct((t, NSA_KV_WIDTH), BF16)] * 4,
        compiler_params=_cparams(("parallel",), 40),
        name="nsa_prep",
    )(p_nsa, qk_norm)


def _compress_kernel(x_ref, pos_ref, w1lo_ref, w1hi_ref, w2_ref, nw_ref, o_ref, lo_acc, hi_acc, *, n_chunk):
    kv = pl.program_id(0)
    l = pl.program_id(1)

    @pl.when(l == 0)
    def _():
        lo_acc[...] = jnp.zeros(lo_acc.shape, F32)
        hi_acc[...] = jnp.zeros(hi_acc.shape, F32)

    pos_lo = pos_ref[0, pl.ds(l, 1), :]
    pos_hi = pos_ref[0, pl.ds(l + CMP_STRIDE, 1), :]
    w_lo = w1lo_ref[0, 0]
    w_hi = w1hi_ref[0, 0]
    for g in range(NSA_KV_HEADS):
        x = x_ref[:, g * HEAD_DIM:(g + 1) * HEAD_DIM]
        lo_acc[g] += _dot((x + pos_lo).astype(BF16), w_lo)
        hi_acc[g] += _dot((x + pos_hi).astype(BF16), w_hi)

    @pl.when(l == CMP_STRIDE - 1)
    def _():
        for g in range(NSA_KV_HEADS):
            hidden = _silu(lo_acc[g] + pltpu.roll(hi_acc[g], shift=n_chunk - 1, axis=0))
            out = _dot(hidden.astype(BF16), w2_ref[0])
            normed = _head_rms(out, nw_ref[...])
            o_ref[0, g] = jnp.where(kv == 0, normed, out).astype(BF16)


def _compress(kv_c, cmp_pos, cmp_w1, cmp_w2, k_norm_w):
    t = kv_c.shape[0]
    n_chunk = t // CMP_STRIDE
    x = kv_c.reshape(n_chunk, CMP_STRIDE * 2 * NSA_KV_WIDTH)
    w1 = cmp_w1.astype(BF16)
    w2 = cmp_w2.astype(BF16)
    return pl.pallas_call(
        functools.partial(_compress_kernel, n_chunk=n_chunk),
        grid=(2, CMP_STRIDE),
        in_specs=[pl.BlockSpec((n_chunk, NSA_KV_WIDTH), lambda a, l: (0, 2 * l + a)),
                  pl.BlockSpec((1, CMP_BLOCK, HEAD_DIM), lambda a, l: (a, 0, 0)),
                  pl.BlockSpec((1, 1, HEAD_DIM, CMP_HIDDEN), lambda a, l: (a, l, 0, 0)),
                  pl.BlockSpec((1, 1, HEAD_DIM, CMP_HIDDEN), lambda a, l: (a, l + CMP_STRIDE, 0, 0)),
                  pl.BlockSpec((1, CMP_HIDDEN, HEAD_DIM), lambda a, l: (a, 0, 0)),
                  pl.BlockSpec((1, HEAD_DIM), lambda a, l: (0, 0))],
        out_specs=pl.BlockSpec((1, NSA_KV_HEADS, n_chunk, HEAD_DIM), lambda a, l: (a, 0, 0, 0)),
        out_shape=jax.ShapeDtypeStruct((2, NSA_KV_HEADS, n_chunk, HEAD_DIM), BF16),
        scratch_shapes=[pltpu.VMEM((NSA_KV_HEADS, n_chunk, CMP_HIDDEN), F32),
                        pltpu.VMEM((NSA_KV_HEADS, n_chunk, CMP_HIDDEN), F32)],
        compiler_params=_cparams(("arbitrary", "arbitrary"), 40),
        name="nsa_compress",
    )(x, cmp_pos, w1, w1, w2, k_norm_w.reshape(1, HEAD_DIM))


def _cmp_attn_kernel(slopes_ref, q_ref, k_ref, v_ref, cov_ref, o_ref, sel_ref, any_ref, *, tq, n_chunk):
    g = pl.program_id(0)
    qi = pl.program_id(1)
    t0 = qi * tq
    sls = [slice(z * HEAD_DIM, (z + 1) * HEAD_DIM) for z in range(NSA_GROUP)]

    def attend(nk):
        k = k_ref[0, 0, 0:nk, :]
        v_t = v_ref[0, 0, 0:nk, :].T
        t_pos = t0 + lax.broadcasted_iota(jnp.int32, (nk, tq), 1)
        k_end = lax.broadcasted_iota(jnp.int32, (nk, tq), 0) * CMP_STRIDE + (CMP_BLOCK - 1)
        mask_bias = jnp.where(t_pos >= k_end, 0.0, MASK_BIAS)
        k_rel = (k_end - t0).astype(F32)
        psum = jnp.zeros((nk, tq), F32)
        qk_next = _dot_nt(k, q_ref[:, sls[0]])
        pending = None
        for z in range(NSA_GROUP):
            qk = qk_next
            if z + 1 < NSA_GROUP:
                qk_next = _dot_nt(k, q_ref[:, sls[z + 1]])
            s = qk + (mask_bias + slopes_ref[g * NSA_GROUP + z] * k_rel)
            m = jnp.maximum(jnp.max(s, axis=0, keepdims=True), NEG_INF)
            e = jnp.exp(s - m)
            p = e * (1.0 / jnp.maximum(jnp.sum(e, axis=0, keepdims=True), 1e-30))
            psum = psum + p
            if pending is not None:
                o_ref[:, sls[pending[0]]] = _dot(v_t, pending[1]).T
            pending = (z, p.astype(BF16))
        o_ref[:, sls[pending[0]]] = _dot(v_t, pending[1]).T
        p_hi = psum.astype(BF16)
        p_lo = (psum - p_hi.astype(F32)).astype(BF16)
        nb = min(N_BLK_LANES, nk * CMP_STRIDE // SEL_BLOCK + SUBLANES)
        cov_t = cov_ref[0:nb, 0:nk]
        select(_dot(cov_t, p_hi) + _dot(cov_t, p_lo), nb)

    def select(imp, nb):
        j = lax.broadcasted_iota(jnp.int32, (nb, tq), 0)
        cur = (t0 + lax.broadcasted_iota(jnp.int32, (nb, tq), 1)) // SEL_BLOCK
        valid = j <= cur
        forced = (j == 0) | (j == cur) | (j == cur - 1)
        val = jnp.where(valid, jnp.where(forced, -2.0, imp), -1.0)
        sel = (valid & forced).astype(F32)
        jf = j.astype(F32)
        for _ in range(SEL_TOPK - 3):
            m = jnp.max(val, axis=0, keepdims=True)
            idx = jnp.min(jnp.where(val == m, jf, float(N_BLK_LANES)), axis=0, keepdims=True)
            hit = jf == idx
            sel = jnp.where(hit & (m > -0.5), 1.0, sel)
            val = jnp.where(hit, -2.0, val)
        if nb < N_BLK_LANES:
            sel = jnp.concatenate([sel, jnp.zeros((N_BLK_LANES - nb, tq), F32)], axis=0)
        sel_q = sel.T
        sel_ref[0] = sel_q.astype(BF16)
        any_ref[0, 0] = jnp.broadcast_to(jnp.max(sel_q, axis=0, keepdims=True), (SUBLANES, N_BLK_LANES))

    visible = (t0 + tq - CMP_BLOCK) // CMP_STRIDE + 1
    n_var = n_chunk // CMP_KEY_STEP if n_chunk % CMP_KEY_STEP == 0 else 1
    if n_var <= 1:
        attend(n_chunk)
    else:
        for var in range(1, n_var + 1):
            lo, hi = (var - 1) * CMP_KEY_STEP, var * CMP_KEY_STEP
            if var == 1:
                cond = visible <= hi
            elif var < n_var:
                cond = (visible > lo) & (visible <= hi)
            else:
                cond = visible > lo
            pl.when(cond)(functools.partial(attend, hi))


def _cmp_attention(q, kv_cmp, slopes, tq=256):
    t = q.shape[0]
    n_chunk = kv_cmp.shape[2]
    n_cmp = n_chunk - CMP_BLOCK // CMP_STRIDE + 1
    n_blk = t // SEL_BLOCK
    assert n_blk <= N_BLK_LANES
    tq = min(tq, t)
    cs = jnp.arange(n_chunk)[:, None] * CMP_STRIDE
    bs = jnp.arange(N_BLK_LANES)[None, :] * SEL_BLOCK
    cover = ((cs <= bs + SEL_BLOCK - 1) & (cs + CMP_BLOCK - 1 >= bs)
             & (jnp.arange(n_chunk)[:, None] < n_cmp) & (jnp.arange(N_BLK_LANES)[None, :] < n_blk))
    cover = cover.astype(BF16).T
    return pl.pallas_call(
        functools.partial(_cmp_attn_kernel, tq=tq, n_chunk=n_chunk),
        grid=(NSA_KV_HEADS, t // tq),
        in_specs=[pl.BlockSpec(memory_space=pltpu.SMEM),
                  pl.BlockSpec((tq, NSA_GROUP * HEAD_DIM), lambda g, i: (i, g)),
                  pl.BlockSpec((1, 1, n_chunk, HEAD_DIM), lambda g, i: (0, g, 0, 0)),
                  pl.BlockSpec((1, 1, n_chunk, HEAD_DIM), lambda g, i: (1, g, 0, 0)),
                  pl.BlockSpec((N_BLK_LANES, n_chunk), lambda g, i: (0, 0))],
        out_specs=[pl.BlockSpec((tq, NSA_GROUP * HEAD_DIM), lambda g, i: (i, g)),
                   pl.BlockSpec((1, tq, N_BLK_LANES), lambda g, i: (g, i, 0)),
                   pl.BlockSpec((1, 1, SUBLANES, N_BLK_LANES), lambda g, i: (g, i, 0, 0))],
        out_shape=[jax.ShapeDtypeStruct((t, NSA_WIDTH), F32),
                   jax.ShapeDtypeStruct((NSA_KV_HEADS, t, N_BLK_LANES), BF16),
                   jax.ShapeDtypeStruct((NSA_KV_HEADS, t // tq, SUBLANES, N_BLK_LANES), F32)],
        compiler_params=_cparams(("parallel", "parallel"), 40),
        name="nsa_cmp_attn",
    )(slopes, q, kv_cmp, kv_cmp, cover)


def _sel_attn_kernel(wl_ref, slopes_ref, q_ref, k_ref, v_ref, eb_ref, sel_ref, o_ref,
                     qa_sc, m_sc, l_sc, acc_sc, *, tq, tk, max_tiles):
    g = pl.program_id(0)
    qi = pl.program_id(1)
    item = g * pl.num_programs(1) + qi
    t0 = qi * tq

    m_sc[...] = jnp.full(m_sc.shape, NEG_INF, F32)
    l_sc[...] = jnp.zeros(l_sc.shape, F32)
    acc_sc[...] = jnp.zeros(acc_sc.shape, F32)
    unselected = sel_ref[0] - 1.0
    for z in range(NSA_GROUP):
        qa_sc[z] = jnp.concatenate([q_ref[:, z * HEAD_DIM:(z + 1) * HEAD_DIM], unselected], axis=1)

    def tile(kj, causal, n_keys=tk):
        k0 = kj * tk
        rows = pl.ds(pl.multiple_of(k0, tk), n_keys)
        k_aug = jnp.concatenate([k_ref[rows, :], eb_ref[kj, 0:n_keys, :]], axis=1)
        v_t = v_ref[rows, :].T
        row = lax.broadcasted_iota(jnp.int32, (n_keys, tq), 0)
        k_rel = (k0 - t0 + row).astype(F32)
        if causal:
            col = lax.broadcasted_iota(jnp.int32, (n_keys, tq), 1)
            causal_bias = jnp.where(t0 + col >= k0 + row, 0.0, MASK_BIAS)
        qk_next = _dot_nt(k_aug, qa_sc[0])
        pending = None
        for z in range(NSA_GROUP):
            qk = qk_next
            if z + 1 < NSA_GROUP:
                qk_next = _dot_nt(k_aug, qa_sc[z + 1])
            s = qk + slopes_ref[g * NSA_GROUP + z] * k_rel
            if causal:
                s = s + causal_bias
            m_prev = m_sc[z]
            m_new = jnp.maximum(m_prev, jnp.max(s, axis=0, keepdims=True))
            alpha = jnp.exp(m_prev - m_new)
            e = jnp.exp(s - m_new)
            l_sc[z] = alpha * l_sc[z] + jnp.sum(e, axis=0, keepdims=True)
            m_sc[z] = m_new
            if pending is not None:
                pz, palpha, pe = pending
                acc_sc[pz] = palpha * acc_sc[pz] + _dot(v_t, pe)
            pending = (z, alpha, e.astype(BF16))
        pz, palpha, pe = pending
        acc_sc[pz] = palpha * acc_sc[pz] + _dot(v_t, pe)

    def visit(i, carry):
        tile(wl_ref[item, i], False)
        return carry

    lax.fori_loop(0, wl_ref[item, max_tiles], visit, 0)
    if tk == 2 * tq:
        in_first_half = t0 % tk == 0
        pl.when(in_first_half)(lambda: tile(t0 // tk, True, tq))
        pl.when(jnp.logical_not(in_first_half))(lambda: tile(t0 // tk, True))
    else:
        tile(t0 // tk, True)

    for z in range(NSA_GROUP):
        out_t = acc_sc[z] * (1.0 / jnp.maximum(l_sc[z], 1e-30))
        o_ref[:, z * HEAD_DIM:(z + 1) * HEAD_DIM] = out_t.T


def _sel_work_list(blk_any, t, tq, tk):
    nq, nk, bpt = t // tq, t // tk, tk // SEL_BLOCK
    n_blk = t // SEL_BLOCK
    picked = blk_any[:, :, 0, :n_blk].reshape(NSA_KV_HEADS, nq, nk, bpt).max(axis=-1) > 0.5
    diagonal = (jnp.arange(nq) * tq) // tk
    before = picked & (jnp.arange(nk)[None, :] < diagonal[:, None])[None]
    count = before.sum(axis=-1, keepdims=True).astype(jnp.int32)
    tiles = jnp.argsort(jnp.logical_not(before), axis=-1, stable=True).astype(jnp.int32)
    return jnp.concatenate([tiles, count], axis=-1).reshape(NSA_KV_HEADS * nq, nk + 1)


def _sel_attention(q, k_s, v_s, sel, blk_any, slopes, tq=256, tk=512):
    t = q.shape[0]
    tq = min(tq, t)
    tk = min(tk, t)
    assert tk % tq == 0
    nk = t // tk
    work_list = _sel_work_list(blk_any, t, tq, tk)

    blk_of_key = jnp.arange(t)[:, None] // SEL_BLOCK
    block_onehot = jnp.where(blk_of_key == jnp.arange(N_BLK_LANES)[None, :], -MASK_BIAS, 0.0)
    block_onehot = block_onehot.astype(BF16).reshape(nk, tk, N_BLK_LANES)

    smem = pl.BlockSpec(memory_space=pltpu.SMEM)
    return pl.pallas_call(
        functools.partial(_sel_attn_kernel, tq=tq, tk=tk, max_tiles=nk),
        grid=(NSA_KV_HEADS, t // tq),
        in_specs=[smem, smem,
                  pl.BlockSpec((tq, NSA_GROUP * HEAD_DIM), lambda g, i: (i, g)),
                  pl.BlockSpec((t, HEAD_DIM), lambda g, i: (0, g)),
                  pl.BlockSpec((t, HEAD_DIM), lambda g, i: (0, g)),
                  pl.BlockSpec((nk, tk, N_BLK_LANES), lambda g, i: (0, 0, 0)),
                  pl.BlockSpec((1, tq, N_BLK_LANES), lambda g, i: (g, i, 0))],
        out_specs=pl.BlockSpec((tq, NSA_GROUP * HEAD_DIM), lambda g, i: (i, g)),
        out_shape=jax.ShapeDtypeStruct((t, NSA_WIDTH), F32),
        scratch_shapes=[pltpu.VMEM((NSA_GROUP, tq, 2 * HEAD_DIM), BF16),
                        pltpu.VMEM((NSA_GROUP, 1, tq), F32), pltpu.VMEM((NSA_GROUP, 1, tq), F32),
                        pltpu.VMEM((NSA_GROUP, HEAD_DIM, tq), F32)],
        compiler_params=_cparams(("arbitrary", "arbitrary"), 40),
        name="nsa_sel_attn",
    )(work_list, slopes, q, k_s, v_s, block_onehot, sel)


def _win_attn_kernel(slopes_ref, q_ref, *refs, tq, n_tile):
    k_refs = refs[:n_tile]
    v_refs = refs[n_tile:2 * n_tile]
    sm_ref, ex_ref, oc_ref, os_ref, z_ref, o_ref = refs[2 * n_tile:]
    outs = []
    g = pl.program_id(0)
    qi = pl.program_id(1)
    span = n_tile * tq
    k = jnp.concatenate([r[...] for r in k_refs], axis=0)
    v = jnp.concatenate([r[...] for r in v_refs], axis=0)
    t_pos = qi * tq + lax.broadcasted_iota(jnp.int32, (tq, span), 0)
    k_pos = (qi - (n_tile - 1)) * tq + lax.broadcasted_iota(jnp.int32, (tq, span), 1)
    dist = t_pos - k_pos
    mask_bias = jnp.where((dist >= 0) & (dist < WINDOW) & (k_pos >= 0), 0.0, MASK_BIAS)
    k_rel = (lax.broadcasted_iota(jnp.int32, (1, span), 1) - (n_tile - 1) * tq).astype(F32)
    sls = [slice(z * HEAD_DIM, (z + 1) * HEAD_DIM) for z in range(NSA_GROUP)]
    qk_next = _dot_nt(q_ref[:, sls[0]], k)
    pending = None
    for z in range(NSA_GROUP):
        qk = qk_next
        if z + 1 < NSA_GROUP:
            qk_next = _dot_nt(q_ref[:, sls[z + 1]], k)
        s = qk + (mask_bias + slopes_ref[g * NSA_GROUP + z] * k_rel)
        m = jnp.maximum(jnp.max(s, axis=-1, keepdims=True), NEG_INF)
        e = jnp.exp(s - m)
        inv_l = 1.0 / jnp.maximum(jnp.sum(e, axis=-1, keepdims=True), 1e-30)
        if pending is not None:
            outs.append(_dot(pending[1], v) * pending[2])
        pending = (z, e.astype(BF16), inv_l)
    outs.append(_dot(pending[1], v) * pending[2])
    o_win = jnp.concatenate(outs, axis=1)

    logits = sm_ref[...]
    hi = logits.astype(BF16)
    lo = (logits - hi.astype(F32)).astype(BF16)
    acc = None
    for br, branch in enumerate((oc_ref[...], os_ref[...], o_win)):
        ex = ex_ref[br]
        gate = _sigmoid(_dot(hi, ex) + _dot(lo, ex))
        acc = gate * branch if acc is None else acc + gate * branch
    o_ref[...] = (acc * _silu(z_ref[...])).astype(o_ref.dtype)


def _win_attention(q, k_w, v_w, slopes, small, o_cmp, o_slc, z_a, tq=256):
    t = q.shape[0]
    tq = min(tq, t)
    n_tile = -(-WINDOW // tq) + 1
    gw = NSA_GROUP * HEAD_DIM

    def kv_spec(c):
        return pl.BlockSpec((tq, HEAD_DIM), lambda g, i: (jnp.maximum(i - (n_tile - 1) + c, 0), g))

    lane = jnp.arange(LANES)[None, :, None]
    br = jnp.arange(N_BRANCH)[:, None, None]
    head = (jnp.arange(NSA_WIDTH) // HEAD_DIM)[None, None, :]
    expand = (lane == _SM_GATE + head * N_BRANCH + br).astype(BF16)
    tile = pl.BlockSpec((tq, gw), lambda g, i: (i, g))
    return pl.pallas_call(
        functools.partial(_win_attn_kernel, tq=tq, n_tile=n_tile),
        grid=(NSA_KV_HEADS, t // tq),
        in_specs=[pl.BlockSpec(memory_space=pltpu.SMEM), tile]
        + [kv_spec(c) for c in range(n_tile)] * 2
        + [pl.BlockSpec((tq, LANES), lambda g, i: (i, 0)),
           pl.BlockSpec((N_BRANCH, LANES, gw), lambda g, i: (0, 0, g)), tile, tile, tile],
        out_specs=tile,
        out_shape=jax.ShapeDtypeStruct((t, NSA_WIDTH), BF16),
        compiler_params=_cparams(("parallel", "parallel"), 40),
        name="nsa_win_attn",
    )(slopes, q, *([k_w] * n_tile), *([v_w] * n_tile), small, expand, o_cmp, o_slc, z_a)


def _shift_rows(cur, halo, s):
    rolled = pltpu.roll(cur, shift=s, axis=0)
    halo_rolled = pltpu.roll(halo, shift=s, axis=0)
    row = lax.broadcasted_iota(jnp.int32, halo.shape, 0)
    head = jnp.where(row < s, halo_rolled, rolled[0:SUBLANES])
    return jnp.concatenate([head, rolled[SUBLANES:]], axis=0)


def _causal_conv(cur, halo, w_ref, k):
    acc = None
    for j in range(k):
        s = k - 1 - j
        term = (cur if s == 0 else _shift_rows(cur, halo, s)) * w_ref[j:j + 1, :]
        acc = term if acc is None else acc + term
    return acc


def _gdn_prep_kernel(p_ref, halo_ref, cw_ref, sm_ref, alog_ref, dtb_ref, q_ref, k_ref, v_ref, gb_ref, xx):
    i = pl.program_id(0)
    tm = p_ref.shape[0]
    xx[0:SUBLANES, :] = jnp.where(i > 0, halo_ref[...], 0.0)
    xx[SUBLANES:, :] = p_ref[...]

    def conv_silu(cols):
        acc = None
        for j in range(GDN_CONV):
            delay = GDN_CONV - 1 - j
            term = xx[SUBLANES - delay:SUBLANES - delay + tm, cols] * cw_ref[j:j + 1, cols]
            acc = term if acc is None else acc + term
        return _silu(acc)

    for h in range(GDN_HEADS):
        sl = slice(h * GDN_DK, (h + 1) * GDN_DK)
        qh = conv_silu(sl)
        q_ref[:, sl] = qh * lax.rsqrt(jnp.sum(qh * qh, axis=-1, keepdims=True) + NORM_EPS) * (GDN_DK ** -0.5)
        kh = conv_silu(slice(GDN_KEY_WIDTH + h * GDN_DK, GDN_KEY_WIDTH + (h + 1) * GDN_DK))
        k_ref[:, sl] = kh * lax.rsqrt(jnp.sum(kh * kh, axis=-1, keepdims=True) + NORM_EPS)
        v_ref[:, sl] = conv_silu(slice(2 * GDN_KEY_WIDTH + h * GDN_DV, 2 * GDN_KEY_WIDTH + (h + 1) * GDN_DV))
    sm = sm_ref[...]
    lane = lax.broadcasted_iota(jnp.int32, sm.shape, 1)
    gdecay = -jnp.exp(alog_ref[...]) * jax.nn.softplus(sm + dtb_ref[...])
    beta = _sigmoid(sm)
    gb_ref[...] = jnp.where(lane < _SM_BETA, gdecay, jnp.where(lane < _SM_GATE, beta, 0.0))


def _gdn_prep(p_gdn, small, conv_w, a_log, dt_bias, tm=256):
    t, width = p_gdn.shape
    tm = min(tm, t)
    hb = tm // SUBLANES
    row = lambda i: (i, 0)
    alog = jnp.zeros((1, LANES), F32).at[0, _SM_A:_SM_A + GDN_HEADS].set(a_log)
    dtb = jnp.zeros((1, LANES), F32).at[0, _SM_A:_SM_A + GDN_HEADS].set(dt_bias)
    const = lambda i: (0, 0)
    return pl.pallas_call(
        _gdn_prep_kernel,
        grid=(t // tm,),
        in_specs=[pl.BlockSpec((tm, width), row),
                  pl.BlockSpec((SUBLANES, width), lambda i: (jnp.maximum(i * hb - 1, 0), 0)),
                  pl.BlockSpec((GDN_CONV, width), const),
                  pl.BlockSpec((tm, LANES), row),
                  pl.BlockSpec((1, LANES), const), pl.BlockSpec((1, LANES), const)],
        out_specs=[pl.BlockSpec((tm, GDN_KEY_WIDTH), row), pl.BlockSpec((tm, GDN_KEY_WIDTH), row),
                   pl.BlockSpec((tm, GDN_VAL_WIDTH), row), pl.BlockSpec((tm, LANES), row)],
        out_shape=[jax.ShapeDtypeStruct((t, GDN_KEY_WIDTH), F32), jax.ShapeDtypeStruct((t, GDN_KEY_WIDTH), F32),
                   jax.ShapeDtypeStruct((t, GDN_VAL_WIDTH), F32), jax.ShapeDtypeStruct((t, LANES), F32)],
        scratch_shapes=[pltpu.VMEM((tm + SUBLANES, width), F32)],
        compiler_params=_cparams(("parallel",), 48),
        name="gdn_prep",
    )(p_gdn, p_gdn, conv_w, small, alog, dtb)


def _row_pad(x):
    return jnp.concatenate([x, jnp.zeros_like(x)], axis=0)


def _gdn_parallel_stages(q_ref, k_ref, v_ref, gb_ref, rows, heads, out):
    c = GDN_CHUNK
    gb = gb_ref[rows, :]
    row = lax.broadcasted_iota(jnp.int32, gb.shape, 0)
    gcum = gb
    shift = 1
    while shift < c:
        gcum = gcum + jnp.where(row >= shift, pltpu.roll(gcum, shift=shift, axis=0), 0.0)
        shift *= 2
    lane = lax.broadcasted_iota(jnp.int32, gb.shape, 1)
    ri = lax.broadcasted_iota(jnp.int32, (c, LANES), 0)
    ci = lax.broadcasted_iota(jnp.int32, (c, LANES), 1)
    tri = ri >= ci
    strict = ri > ci
    eye = ri == ci
    eyef = eye.astype(F32)
    hs = range(heads)
    sls = [slice(h * GDN_DK, (h + 1) * GDN_DK) for h in hs]

    gcol = [jnp.sum(jnp.where(lane == _SM_A + h, gcum, 0.0), axis=-1, keepdims=True) for h in hs]
    beta = [jnp.sum(jnp.where(lane == _SM_BETA + h, gb, 0.0), axis=-1, keepdims=True) for h in hs]
    decay = []
    for h in hs:
        gmat = jnp.broadcast_to(gcol[h], (c, LANES))
        grow = jnp.sum(jnp.where(eye, gmat, 0.0), axis=0, keepdims=True)
        decay.append(jnp.where(tri, jnp.exp(jnp.where(tri, gmat - grow, 0.0)), 0.0))
    glast = [gcol[h][c - 1:c, :] for h in hs]
    egc = [jnp.exp(gcol[h]) for h in hs]
    q = [q_ref[rows, sls[h]] for h in hs]
    k = [k_ref[rows, sls[h]] for h in hs]
    k16 = [k[h].astype(BF16) for h in hs]
    qk = [_dot_nt(jnp.concatenate([q[h].astype(BF16), k16[h]], axis=0), _row_pad(k16[h])) for h in hs]
    yield
    out["at16"] = [(qk[h][:c] * decay[h]).astype(BF16) for h in hs]

    pw = [-jnp.where(strict, beta[h] * qk[h][c:] * decay[h], 0.0) for h in hs]
    inv = [eyef + pw[h] for h in hs]
    pw16 = [pw[h].astype(BF16) for h in hs]
    pw = [_dot(pw16[h], _row_pad(pw16[h])) for h in hs]
    yield
    span = 2
    while span < c:
        pw16 = [pw[h].astype(BF16) for h in hs]
        if 2 * span < c:
            both = [_dot(jnp.concatenate([pw16[h], inv[h].astype(BF16)], axis=0), _row_pad(pw16[h])) for h in hs]
            pw = [both[h][:c] for h in hs]
            inv = [inv[h] + both[h][c:] for h in hs]
        else:
            inv = [inv[h] + _dot(inv[h].astype(BF16), _row_pad(pw16[h])) for h in hs]
        span *= 2
        yield

    uw = []
    for h in hs:
        rhs = jnp.concatenate([(v_ref[rows, sls[h]] * beta[h]).astype(BF16),
                               (k[h] * (beta[h] * egc[h])).astype(BF16)], axis=1)
        uw.append(_dot(inv[h].astype(BF16), _row_pad(rhs)))
    yield
    out["u"] = [uw[h][:, :GDN_DV] for h in hs]
    out["w16"] = [uw[h][:, GDN_DV:].astype(BF16) for h in hs]
    out["qe16"] = [(q[h] * egc[h]).astype(BF16) for h in hs]
    out["ke16"] = [(k[h] * jnp.exp(glast[h] - gcol[h])).astype(BF16) for h in hs]
    out["eg"] = [jnp.exp(glast[h]) for h in hs]


def _gdn_state_stages(parts, s_cur, z_ref, nw, o_ref, rows, heads):
    c = GDN_CHUNK
    hs = range(heads)
    sls = [slice(h * GDN_DK, (h + 1) * GDN_DK) for h in hs]
    ws_qs = [_dot(jnp.concatenate([parts["w16"][h], parts["qe16"][h]], axis=0), s_cur[h].astype(BF16))
             for h in hs]
    yield
    v_new16 = [(parts["u"][h] - ws_qs[h][:c]).astype(BF16) for h in hs]
    o = [ws_qs[h][c:] + _dot(parts["at16"][h], _row_pad(v_new16[h])) for h in hs]
    yield
    s_new = [s_cur[h] * parts["eg"][h] + _dot_tn(parts["ke16"][h], v_new16[h]) for h in hs]
    for h in hs:
        s_cur[h] = s_new[h]
    yield
    for h in hs:
        o_ref[rows, sls[h]] = (_head_rms(o[h], nw) * _silu(z_ref[rows, sls[h]])).astype(o_ref.dtype)


def _gdn_fused_kernel(q_ref, k_ref, v_ref, gb_ref, z_ref, nw_ref, o_ref, state, *, heads, cps):
    n = pl.program_id(0)
    c = GDN_CHUNK

    @pl.when(n == 0)
    def _():
        state[...] = jnp.zeros(state.shape, F32)

    nw = nw_ref[...]
    s_cur = [state[h] for h in range(heads)]
    prev = None
    for cc in range(cps):
        rows = slice(cc * c, (cc + 1) * c)
        parts = {}
        gens = [_gdn_parallel_stages(q_ref, k_ref, v_ref, gb_ref, rows, heads, parts)]
        if prev is not None:
            gens.insert(0, _gdn_state_stages(prev[0], s_cur, z_ref, nw, o_ref, prev[1], heads))
        while gens:
            for gen in list(gens):
                if next(gen, "done") == "done":
                    gens.remove(gen)
        prev = (parts, rows)
    for _ in _gdn_state_stages(prev[0], s_cur, z_ref, nw, o_ref, prev[1], heads):
        pass
    for h in range(heads):
        state[h] = s_cur[h]


def _gdn_fused(q, k, v, gb, z_b, norm_w, cps=4):
    t = q.shape[0]
    c = GDN_CHUNK
    n_chunk = t // c
    while n_chunk % cps:
        cps //= 2
    heads = GDN_HEADS
    wide = pl.BlockSpec((cps * c, GDN_VAL_WIDTH), lambda n: (n, 0))
    return pl.pallas_call(
        functools.partial(_gdn_fused_kernel, heads=heads, cps=cps),
        grid=(n_chunk // cps,),
        in_specs=[wide, wide, wide, pl.BlockSpec((cps * c, LANES), lambda n: (n, 0)), wide,
                  pl.BlockSpec((1, GDN_DV), lambda n: (0, 0))],
        out_specs=wide,
        out_shape=jax.ShapeDtypeStruct((t, GDN_VAL_WIDTH), BF16),
        scratch_shapes=[pltpu.VMEM((heads, GDN_DK, GDN_DV), F32)],
        compiler_params=_cparams(("arbitrary",), 48),
        name="gdn_fused",
    )(q, k, v, gb, z_b, norm_w.reshape(1, GDN_DV))


def _short_conv_kernel(bg_ref, cg_ref, x_ref, z_ref, cgh_ref, xh_ref, cw_ref, o_ref):
    i = pl.program_id(0)
    cur = cg_ref[...] * x_ref[...]
    halo = jnp.where(i > 0, cgh_ref[...] * xh_ref[...], 0.0)
    y = _causal_conv(cur, halo, cw_ref, SC_CONV)
    o_ref[...] = (bg_ref[...] * y * _silu(z_ref[...])).astype(o_ref.dtype)


def _short_conv(p_sc, conv_w, tm=256):
    t = p_sc.shape[0]
    tm = min(tm, t)
    hb = tm // SUBLANES
    main = lambda c: pl.BlockSpec((tm, SC_WIDTH), functools.partial(lambda i, c: (i, c), c=c))
    halo = lambda c: pl.BlockSpec((SUBLANES, SC_WIDTH),
                                  functools.partial(lambda i, c: (jnp.maximum(i * hb - 1, 0), c), c=c))
    return pl.pallas_call(
        _short_conv_kernel,
        grid=(t // tm,),
        in_specs=[main(0), main(1), main(2), main(3), halo(1), halo(2),
                  pl.BlockSpec((SC_CONV, SC_WIDTH), lambda i: (0, 0))],
        out_specs=pl.BlockSpec((tm, SC_WIDTH), lambda i: (i, 0)),
        out_shape=jax.ShapeDtypeStruct((t, SC_WIDTH), BF16),
        compiler_params=_cparams(("parallel",), 40),
        name="short_conv",
    )(p_sc, p_sc, p_sc, p_sc, p_sc, p_sc, conv_w)


_IN_GROUPS = (("nsa", _OFF_QA, _OFF_GATE), ("za", _OFF_ZA, _OFF_QB), ("gdn", _OFF_QB, _OFF_AB),
              ("zb", _OFF_ZB, _OFF_SC), ("sc", _OFF_SC, _OFF_MERGE), ("merge", _OFF_MERGE, None))


def _stage_w_in(w_in):
    depth, d, _ = w_in.shape
    w_t = jnp.transpose(w_in, (0, 2, 1)).astype(BF16)
    small_pad = LANES - 2 * GDN_HEADS - 3 * NSA_HEADS
    narrow = jnp.concatenate([w_in[:, :, _OFF_AB:_OFF_ZB], w_in[:, :, _OFF_GATE:_OFF_ZA],
                              jnp.zeros((depth, d, small_pad), F32)], axis=2)
    w_small_t = jnp.transpose(narrow, (0, 2, 1)).astype(BF16)
    return w_t, w_small_t


def _layer(x, layer, norm_w, w_t, w_small_t, nsa_qk_norm, cmp_pos, cmp_w1, cmp_w2, gdn_conv_w, gdn_a_log,
           gdn_dt_bias, gdn_norm_w, sc_conv_w, wb_nsa, wb_gdn, wb_sc, w_out, slopes):
    hn = _rmsnorm(x, norm_w)
    proj = {}
    for name, a, b in _IN_GROUPS:
        n = (w_t.shape[1] if b is None else b) - a
        proj[name] = _matmul_nt(hn, w_t, layer, a, n, F32, 1024, 1024, "in_proj_" + name)
    small = _matmul_nt(hn, w_small_t, layer, 0, LANES, F32, 1024, LANES, "in_proj_small")
    p_nsa, z_a, p_gdn, z_b, p_sc, p_mg = (proj[n] for n, _, _ in _IN_GROUPS)

    q, kv_c, k_s, v_s, k_w, v_w = _nsa_prep(p_nsa, nsa_qk_norm)
    kv_cmp = _compress(kv_c, cmp_pos, cmp_w1, cmp_w2, nsa_qk_norm[1])
    o_cmp, sel, blk_any = _cmp_attention(q, kv_cmp, slopes)
    o_slc = _sel_attention(q, k_s, v_s, sel, blk_any, slopes)
    o_a = _win_attention(q, k_w, v_w, slopes, small, o_cmp, o_slc, z_a)

    q_b, k_b, v_b, gb = _gdn_prep(p_gdn, small, gdn_conv_w, gdn_a_log, gdn_dt_bias)
    o_b = _gdn_fused(q_b, k_b, v_b, gb, z_b, gdn_norm_w)

    o_c = _short_conv(p_sc, sc_conv_w)

    merged = _merge(o_a, o_b, o_c, wb_nsa, wb_gdn, wb_sc, layer, p_mg)
    return _matmul_residual(merged, w_out, layer, x, 1024, 512)


def kernel(x, norm_w, w_in, nsa_qk_norm, cmp_pos, cmp_w1, cmp_w2, gdn_conv_w, gdn_a_log, gdn_dt_bias,
           gdn_norm_w, sc_conv_w, w_branch_nsa, w_branch_gdn, w_branch_sc, w_out):
    b, t, d = x.shape
    depth = norm_w.shape[0]
    heads = jnp.arange(1, NSA_HEADS + 1, dtype=F32)
    slopes = jnp.exp2(-8.0 * heads / NSA_HEADS)
    w_t, w_small_t = _stage_w_in(w_in)
    wb_nsa, wb_gdn, wb_sc, w_out16 = (w.astype(BF16) for w in (w_branch_nsa, w_branch_gdn, w_branch_sc, w_out))
    outs = []
    for bi in range(b):
        xb = x[bi]
        for l in range(depth):
            xb = _layer(xb, l, norm_w[l], w_t, w_small_t, nsa_qk_norm[l], cmp_pos[l], cmp_w1[l], cmp_w2[l],
                        gdn_conv_w[l], gdn_a_log[l], gdn_dt_bias[l], gdn_norm_w[l], sc_conv_w[l],
                        wb_nsa, wb_gdn, wb_sc, w_out16, slopes)
        outs.append(xb)
    return jnp.stack(outs, axis=0)
```
